```python
import math
import jax, jax.numpy as jnp
from jax import lax
import numpy as np

D_MODEL = 1024
BATCH = 8
SEQ = 2048
DEPTH = 1

CHUNK = 64
Q_BLOCK = 2 * CHUNK
SB_HEADS = 8
SB_HEAD_DIM = 64
SB_WIDTH = SB_HEADS * SB_HEAD_DIM
SSM_GROUP = 16
SSM_WIDTH = 512
SSM_GROUPS = SSM_WIDTH // SSM_GROUP
SSM_STATE = 64
PLE_DIM = 256
N_EXPERTS = 32
TOP_K = 4
D_FF = D_MODEL
SWIGLU_LIMIT = 7.0
SWIGLU_ALPHA = 1.702
MOE_BLOCK = 256
EPS = 1e-6
IN_COLS = 3 * SB_WIDTH + SSM_WIDTH + 2 * D_MODEL

kernel_name = "hybrid_stickbreak_s5_moe_ple"


def rmsnorm(x, g):
    xf = x.astype(jnp.float32)
    y = xf * lax.rsqrt(jnp.mean(xf * xf, axis=-1, keepdims=True) + EPS)
    return (y * g.astype(jnp.float32)).astype(x.dtype)


def stick_breaking_attention(q, k, v):
    s_len, dh = q.shape[2], q.shape[3]
    scale = dh ** -0.5
    outs = []
    for start in range(0, s_len, Q_BLOCK):
        end = start + Q_BLOCK
        qb = q[:, :, start:end].astype(jnp.float32)
        kb = k[:, :, :end].astype(jnp.float32)
        vb = v[:, :, :end].astype(jnp.float32)
        z = jnp.einsum('bhtd,bhsd->bhts', qb, kb) * scale
        t_idx = start + jnp.arange(Q_BLOCK)
        s_idx = jnp.arange(end)
        mask = s_idx[None, :] < t_idx[:, None]
        log_keep = jnp.where(mask, jax.nn.log_sigmoid(-z), 0.0)
        later = lax.cumsum(log_keep, axis=3, reverse=True) - log_keep
        w = jnp.where(mask, jnp.exp(jax.nn.log_sigmoid(z) + later), 0.0)
        outs.append(jnp.einsum('bhts,bhsd->bhtd', w, vb))
    return jnp.concatenate(outs, axis=2).astype(q.dtype)


def s5_ssm(u, a_re, a_im, log_dt, b_re, b_im, c_re, c_im, d_skip):
    f32 = jnp.float32
    bsz, s_len, _ = u.shape
    uf = u.astype(f32)
    ug = uf.reshape(bsz, s_len, SSM_GROUPS, SSM_GROUP)
    lr = a_re.astype(f32)
    li = a_im.astype(f32)
    dt = jnp.exp(log_dt.astype(f32))[:, None]
    mag = jnp.exp(lr * dt)
    abar_r = mag * jnp.cos(li * dt)
    abar_i = mag * jnp.sin(li * dt)
    den = lr * lr + li * li
    nr = abar_r - 1.0
    ni = abar_i
    fr = (nr * lr + ni * li) / den
    fi = (ni * lr - nr * li) / den
    br = b_re.astype(f32)
    bi = b_im.astype(f32)
    bbar_r = fr[..., None] * br - fi[..., None] * bi
    bbar_i = fr[..., None] * bi + fi[..., None] * br
    xr = jnp.einsum('bsgc,gpc->sbgp', ug, bbar_r)
    xi = jnp.einsum('bsgc,gpc->sbgp', ug, bbar_i)
    ar = jnp.broadcast_to(abar_r[None, None], (s_len, 1, SSM_GROUPS, SSM_STATE))
    ai = jnp.broadcast_to(abar_i[None, None], (s_len, 1, SSM_GROUPS, SSM_STATE))

    def combine(e1, e2):
        a1r, a1i, b1r, b1i = e1
        a2r, a2i, b2r, b2i = e2
        return (a2r * a1r - a2i * a1i,
                a2r * a1i + a2i * a1r,
                a2r * b1r - a2i * b1i + b2r,
                a2r * b1i + a2i * b1r + b2i)

    _, _, hr, hi = lax.associative_scan(combine, (ar, ai, xr, xi), axis=0)
    y = (jnp.einsum('sbgp,gcp->bsgc', hr, c_re.astype(f32))
         - jnp.einsum('sbgp,gcp->bsgc', hi, c_im.astype(f32)))
    y = y.reshape(bsz, s_len, SSM_WIDTH) + d_skip.astype(f32) * uf
    return y.astype(u.dtype)


def moe_swiglu(h, w_router, b_router, w_up, b_up, w_down, b_down):
    f32 = jnp.float32
    bsz, s_len, d = h.shape
    n = bsz * s_len
    nk = n * TOP_K
    hf = h.reshape(n, d)
    logits = hf.astype(f32) @ w_router.astype(f32) + b_router.astype(f32)
    top_val, top_idx = lax.top_k(logits, TOP_K)
    gate = jax.nn.softmax(top_val, axis=-1)
    expert = top_idx.reshape(-1)
    weight = gate.reshape(-1)
    token = jnp.arange(nk, dtype=jnp.int32) // TOP_K
    order = jnp.argsort(expert, stable=True)
    e_sorted = expert[order]
    counts = jnp.bincount(expert, length=N_EXPERTS)
    padded = (counts + MOE_BLOCK - 1) // MOE_BLOCK * MOE_BLOCK
    start = jnp.cumsum(counts) - counts
    pad_end = jnp.cumsum(padded)
    pad_start = pad_end - padded
    dst = pad_start[e_sorted] + jnp.arange(nk) - start[e_sorted]
    n_blocks = -(-nk // MOE_BLOCK) + N_EXPERTS
    rows = n_blocks * MOE_BLOCK
    row_token = jnp.zeros((rows,), jnp.int32).at[dst].set(token[order])
    row_weight = jnp.zeros((rows,), f32).at[dst].set(weight[order])
    block_expert = jnp.minimum(
        jnp.searchsorted(pad_end, jnp.arange(n_blocks) * MOE_BLOCK, side='right'),
        N_EXPERTS - 1)
    xs = hf[row_token].reshape(n_blocks, MOE_BLOCK, d)

    def expert_block(args):
        xb, e = args
        gu = xb.astype(f32) @ w_up[e].astype(f32) + b_up[e].astype(f32)
        g = jnp.minimum(gu[:, :D_FF], SWIGLU_LIMIT)
        up = jnp.clip(gu[:, D_FF:], -SWIGLU_LIMIT, SWIGLU_LIMIT)
        act = (up + 1.0) * g * jax.nn.sigmoid(SWIGLU_ALPHA * g)
        return act @ w_down[e].astype(f32) + b_down[e].astype(f32)

    ys = lax.map(expert_block, (xs, block_expert)).reshape(rows, d)
    out = jax.ops.segment_sum(ys * row_weight[:, None], row_token, num_segments=n)
    return out.reshape(bsz, s_len, d).astype(h.dtype)


def setup_inputs(seed: int = 0) -> dict:
    key = jax.random.key(seed)
    ks = iter(jax.random.split(key, 40))
    f32 = jnp.float32
    L = DEPTH

    def nrm(shape, scale):
        return jax.random.normal(next(ks), shape, f32) * scale

    def gain(shape):
        return 1.0 + 0.05 * jax.random.normal(next(ks), shape, f32)

    n_idx = jnp.arange(SSM_STATE, dtype=f32)
    return {
        "x": nrm((BATCH, SEQ, D_MODEL), 1.0),
        "p": nrm((DEPTH, BATCH, SEQ, PLE_DIM), 1.0),
        "g_mix": gain((L, D_MODEL)),
        "w_in": nrm((L, D_MODEL, IN_COLS), D_MODEL ** -0.5),
        "g_q": gain((L, SB_HEAD_DIM)),
        "g_k": gain((L, SB_HEAD_DIM)),
        "w_attn_branch": nrm((L, SB_WIDTH, D_MODEL), SB_WIDTH ** -0.5),
        "a_re": -0.5 + nrm((L, SSM_GROUPS, SSM_STATE), 0.01),
        "a_im": math.pi * n_idx + nrm((L, SSM_GROUPS, SSM_STATE), 0.01),
        "log_dt": jax.random.uniform(next(ks), (L, SSM_GROUPS), f32,
                                     math.log(1e-3), math.log(1e-1)),
        "b_re": nrm((L, SSM_GROUPS, SSM_STATE, SSM_GROUP), (2 * SSM_GROUP) ** -0.5),
        "b_im": nrm((L, SSM_GROUPS, SSM_STATE, SSM_GROUP), (2 * SSM_GROUP) ** -0.5),
        "c_re": nrm((L, SSM_GROUPS, SSM_GROUP, SSM_STATE), (2 * SSM_STATE) ** -0.5),
        "c_im": nrm((L, SSM_GROUPS, SSM_GROUP, SSM_STATE), (2 * SSM_STATE) ** -0.5),
        "d_skip": gain((L, SSM_WIDTH)),
        "w_glu": nrm((L, SSM_WIDTH, 2 * D_MODEL), SSM_WIDTH ** -0.5),
        "b_glu": nrm((L, 2 * D_MODEL), 0.01),
        "w_out": nrm((L, D_MODEL, D_MODEL), D_MODEL ** -0.5),
        "g_ffn": gain((L, D_MODEL)),
        "w_router": nrm((L, D_MODEL, N_EXPERTS), D_MODEL ** -0.5),
        "b_router": nrm((L, N_EXPERTS), 0.01),
        "w_up": nrm((L, N_EXPERTS, D_MODEL, 2 * D_FF), D_MODEL ** -0.5),
        "b_up": nrm((L, N_EXPERTS, 2 * D_FF), 0.01),
        "w_down": nrm((L, N_EXPERTS, D_FF, D_MODEL), D_FF ** -0.5),
        "b_down": nrm((L, N_EXPERTS, D_MODEL), 0.01),
        "g_ple_gate": gain((L, D_MODEL)),
        "w_ple_gate": nrm((L, D_MODEL, D_MODEL), D_MODEL ** -0.5),
        "w_ple_proj": nrm((L, PLE_DIM, D_MODEL), PLE_DIM ** -0.5),
        "g_ple_post": gain((L, D_MODEL)),
    }


def reference(x, p, g_mix, w_in, g_q, g_k, w_attn_branch, a_re, a_im, log_dt,
              b_re, b_im, c_re, c_im, d_skip, w_glu, b_glu, w_out, g_ffn,
              w_router, b_router, w_up, b_up, w_down, b_down,
              g_ple_gate, w_ple_gate, w_ple_proj, g_ple_post):
    bsz, s_len, d = x.shape
    splits = [SB_WIDTH, 2 * SB_WIDTH, 3 * SB_WIDTH,
              3 * SB_WIDTH + SSM_WIDTH, 3 * SB_WIDTH + SSM_WIDTH + D_MODEL]

    def heads(t):
        return t.reshape(bsz, s_len, SB_HEADS, SB_HEAD_DIM).transpose(0, 2, 1, 3)

    for i in range(DEPTH):
        h = rmsnorm(x, g_mix[i])
        proj = h @ w_in[i]
        q, k, v, u, gate_a, gate_s = jnp.split(proj, splits, axis=-1)
        qh = rmsnorm(heads(q), g_q[i])
        kh = rmsnorm(heads(k), g_k[i])
        o = stick_breaking_attention(qh, kh, heads(v))
        o = o.transpose(0, 2, 1, 3).reshape(bsz, s_len, SB_WIDTH)
        attn_branch = o @ w_attn_branch[i]

        y = s5_ssm(u, a_re[i], a_im[i], log_dt[i], b_re[i], b_im[i],
                   c_re[i], c_im[i], d_skip[i])
        zg = jax.nn.gelu(y) @ w_glu[i] + b_glu[i]
        ssm_branch = zg[..., :D_MODEL] * jax.nn.sigmoid(zg[..., D_MODEL:])

        mixed = jax.nn.sigmoid(gate_a) * attn_branch + jax.nn.sigmoid(gate_s) * ssm_branch
        x = x + mixed @ w_out[i]

        x = x + moe_swiglu(rmsnorm(x, g_ffn[i]), w_router[i], b_router[i],
                           w_up[i], b_up[i], w_down[i], b_down[i])

        ple_gate = jax.nn.sigmoid(rmsnorm(x, g_ple_gate[i]) @ w_ple_gate[i])
        ple = rmsnorm(p[i] @ w_ple_proj[i], g_ple_post[i])
        x = x + ple_gate * ple
    return x
```

```python
import functools
import math

import jax
import jax.numpy as jnp
from jax import lax
from jax.experimental import pallas as pl
from jax.experimental.pallas import tpu as pltpu

F32 = jnp.float32
BF16 = jnp.bfloat16

D_MODEL = 1024
SB_HEADS = 8
SB_HEAD_DIM = 64
SB_WIDTH = SB_HEADS * SB_HEAD_DIM
SSM_GROUP = 16
SSM_WIDTH = 512
SSM_GROUPS = SSM_WIDTH // SSM_GROUP
SSM_STATE = 64
PLE_DIM = 256
N_EXPERTS = 32
TOP_K = 4
D_FF = D_MODEL
SWIGLU_LIMIT = 7.0
SWIGLU_ALPHA = 1.702
EPS = 1e-6

LANES = 128
SUBLANES = 8
VMEM_LIMIT = 56 * 1024 * 1024

TOK_TILE = 256
ATT_BLK = 128
SCAN_CHUNK = 64
SCAN_LANES = 512
HALF_U = SSM_WIDTH // 2
HALF_STATE = SSM_GROUPS // 2 * SSM_STATE
ROW_BLK = 256
N_ITEMS_EXTRA = N_EXPERTS - 1


def _cparams(n_axes):
    return pltpu.CompilerParams(
        dimension_semantics=("arbitrary",) * n_axes,
        vmem_limit_bytes=VMEM_LIMIT)


def _rms(x, g):
    ms = jnp.mean(x * x, axis=-1, keepdims=True)
    return x * lax.rsqrt(ms + EPS) * g


def _inproj_kernel(x_ref, g_ref, w_ref, qkv_ref, u_ref, gates_ref):
    h = _rms(x_ref[...], g_ref[...]).astype(BF16)
    n_qkv = 3 * SB_WIDTH
    n_u = n_qkv + SSM_WIDTH
    qkv_ref[...] = jnp.dot(h, w_ref[:, :n_qkv], preferred_element_type=F32)
    u_ref[...] = jnp.dot(h, w_ref[:, n_qkv:n_u], preferred_element_type=F32)
    gates_ref[...] = jnp.dot(h, w_ref[:, n_u:], preferred_element_type=F32)


def _inproj(x2d, g_mix, w_in_bf, bsz, s_len):
    n = x2d.shape[0]
    tiles_per_seq = s_len // TOK_TILE
    in_cols = w_in_bf.shape[1]
    return pl.pallas_call(
        _inproj_kernel,
        grid=(n // TOK_TILE,),
        in_specs=[
            pl.BlockSpec((TOK_TILE, D_MODEL), lambda i: (i, 0)),
            pl.BlockSpec((1, D_MODEL), lambda i: (0, 0)),
            pl.BlockSpec((D_MODEL, in_cols), lambda i: (0, 0)),
        ],
        out_specs=[
            pl.BlockSpec((TOK_TILE, 3 * SB_WIDTH), lambda i: (i, 0)),
            pl.BlockSpec((TOK_TILE, SSM_WIDTH),
                         lambda i: (i % tiles_per_seq, i // tiles_per_seq)),
            pl.BlockSpec((TOK_TILE, 2 * D_MODEL), lambda i: (i, 0)),
        ],
        out_shape=[
            jax.ShapeDtypeStruct((n, 3 * SB_WIDTH), F32),
            jax.ShapeDtypeStruct((s_len, bsz * SSM_WIDTH), F32),
            jax.ShapeDtypeStruct((n, 2 * D_MODEL), F32),
        ],
        compiler_params=_cparams(1),
        name="inproj",
    )(x2d, g_mix, w_in_bf)


def _attn_kernel(q_ref, k_ref, v_ref, gq_ref, gk_ref, o_ref,
                 qs_ref, ks_ref, vs_ref, w2_ref, c_ref, acc_ref):
    s_len = q_ref.shape[1]
    n_blk = s_len // ATT_BLK
    lane = lax.broadcasted_iota(jnp.int32, (1, LANES), 1)
    head0 = lane < SB_HEAD_DIM

    def head_rms(t, g):
        sq = t * t
        s0 = jnp.sum(jnp.where(head0, sq, 0.0), axis=-1, keepdims=True)
        s1 = jnp.sum(jnp.where(head0, 0.0, sq), axis=-1, keepdims=True)
        ms = jnp.where(head0, s0, s1) * (1.0 / SB_HEAD_DIM)
        return t * lax.rsqrt(ms + EPS) * g

    scale = SB_HEAD_DIM ** -0.5
    norm_rows = 256

    def prep(i, _):
        rows = pl.ds(pl.multiple_of(i * norm_rows, norm_rows), norm_rows)
        qn = head_rms(q_ref[0, rows, :], gq_ref[...]) * scale
        qs_ref[0, rows, :] = jnp.where(head0, qn, 0.0).astype(BF16)
        qs_ref[1, rows, :] = jnp.where(head0, 0.0, qn).astype(BF16)
        ks_ref[rows, :] = head_rms(k_ref[0, rows, :], gk_ref[...]).astype(BF16)
        vs_ref[rows, :] = v_ref[0, rows, :].astype(BF16)
        return 0

    lax.fori_loop(0, s_len // norm_rows, prep, 0)

    r = lax.broadcasted_iota(jnp.int32, (2 * ATT_BLK, 2 * ATT_BLK), 0)
    c = lax.broadcasted_iota(jnp.int32, (2 * ATT_BLK, 2 * ATT_BLK), 1)
    r = jnp.where(r >= ATT_BLK, r - ATT_BLK, r)
    w2_ref[...] = jnp.where((c >= ATT_BLK) | (r > c), 1.0, 0.0).astype(BF16)

    ti = lax.broadcasted_iota(jnp.int32, (ATT_BLK, ATT_BLK), 0)
    si = lax.broadcasted_iota(jnp.int32, (ATT_BLK, ATT_BLK), 1)
    causal = si < ti

    def tile(h, q_rows, kv_rows, diag):
        qh = qs_ref[h, q_rows, :]
        z = lax.dot_general(qh, ks_ref[kv_rows, :], (((1,), (1,)), ((), ())),
                            preferred_element_type=F32)
        sp = jnp.maximum(z, 0.0) + jnp.log(1.0 + jnp.exp(-jnp.abs(z)))
        log_keep = -sp
        log_beta = z - sp
        if diag:
            log_keep = jnp.where(causal, log_keep, 0.0)
        hi = log_keep.astype(BF16)
        lo = (log_keep - hi.astype(F32)).astype(BF16)
        sums = jnp.dot(jnp.concatenate([hi, lo], axis=1), w2_ref[...],
                       preferred_element_type=F32)
        later = sums[:, :ATT_BLK] + c_ref[h]
        w = jnp.exp(log_beta + later)
        if diag:
            w = jnp.where(causal, w, 0.0)
        acc_ref[h] += jnp.dot(w.astype(BF16), vs_ref[kv_rows, :],
                              preferred_element_type=F32)
        c_ref[h] += sums[:, ATT_BLK:]

    def qblock(qi, _):
        q_rows = pl.ds(pl.multiple_of(qi * ATT_BLK, ATT_BLK), ATT_BLK)
        c_ref[...] = jnp.zeros_like(c_ref)
        acc_ref[...] = jnp.zeros_like(acc_ref)
        tile(0, q_rows, q_rows, True)
        tile(1, q_rows, q_rows, True)

        def kvblock(jj, _):
            j = qi - 1 - jj
            kv_rows = pl.ds(pl.multiple_of(j * ATT_BLK, ATT_BLK), ATT_BLK)
            tile(0, q_rows, kv_rows, False)
            tile(1, q_rows, kv_rows, False)
            return 0

        lax.fori_loop(0, qi, kvblock, 0)
        o_ref[0, q_rows, :] = jnp.where(head0, acc_ref[0], acc_ref[1])
        return 0

    lax.fori_loop(0, n_blk, qblock, 0)


def _attn(qkv3, gq2, gk2):
    bsz, s_len, _ = qkv3.shape
    n_pairs = SB_WIDTH // LANES
    blk = (1, s_len, LANES)
    return pl.pallas_call(
        _attn_kernel,
        grid=(bsz, n_pairs),
        in_specs=[
            pl.BlockSpec(blk, lambda b, p: (b, 0, p)),
            pl.BlockSpec(blk, lambda b, p: (b, 0, n_pairs + p)),
            pl.BlockSpec(blk, lambda b, p: (b, 0, 2 * n_pairs + p)),
            pl.BlockSpec((1, LANES), lambda b, p: (0, 0)),
            pl.BlockSpec((1, LANES), lambda b, p: (0, 0)),
        ],
        out_specs=pl.BlockSpec(blk, lambda b, p: (b, 0, p)),
        out_shape=jax.ShapeDtypeStruct((bsz, s_len, SB_WIDTH), F32),
        scratch_shapes=[
            pltpu.VMEM((2, s_len, LANES), BF16),
            pltpu.VMEM((s_len, LANES), BF16),
            pltpu.VMEM((s_len, LANES), BF16),
            pltpu.VMEM((2 * ATT_BLK, 2 * ATT_BLK), BF16),
            pltpu.VMEM((2, ATT_BLK, ATT_BLK), F32),
            pltpu.VMEM((2, ATT_BLK, LANES), F32),
        ],
        compiler_params=_cparams(2),
        name="attn",
    )(qkv3, qkv3, qkv3, gq2, gk2)


def _s5_kernel(u_ref, a_ref, b_ref, c_ref, d_ref, y_ref, hbuf_ref, state_ref):
    tc, bsz, _ = u_ref.shape
    rows = tc * bsz

    @pl.when(pl.program_id(0) == 0)
    def _():
        state_ref[...] = jnp.zeros_like(state_ref)

    u2 = u_ref[...].reshape(rows, SSM_WIDTH)
    ub = u2.astype(BF16)
    for hf in range(2):
        uh = ub[:, hf * HALF_U:(hf + 1) * HALF_U]
        xh = jnp.dot(uh, b_ref[hf], preferred_element_type=F32)
        hbuf_ref[...] = xh.reshape(tc, bsz, 2 * HALF_STATE)
        for lc in range(HALF_STATE // SCAN_LANES):
            re = pl.ds(lc * SCAN_LANES, SCAN_LANES)
            im = pl.ds(HALF_STATE + lc * SCAN_LANES, SCAN_LANES)
            ar = a_ref[hf, :, re]
            ai = a_ref[hf, :, im]

            def step(t, carry, re=re, im=im, ar=ar, ai=ai):
                hr, hi = carry
                nr = ar * hr - ai * hi + hbuf_ref[t, :, re]
                ni = ar * hi + ai * hr + hbuf_ref[t, :, im]
                hbuf_ref[t, :, re] = nr
                hbuf_ref[t, :, im] = ni
                return nr, ni

            hr, hi = lax.fori_loop(0, tc, step,
                                   (state_ref[hf, :, re], state_ref[hf, :, im]),
                                   unroll=8)
            state_ref[hf, :, re] = hr
            state_ref[hf, :, im] = hi
        hb = hbuf_ref[...].reshape(rows, 2 * HALF_STATE).astype(BF16)
        yh = jnp.dot(hb, c_ref[hf], preferred_element_type=F32)
        cols = slice(hf * HALF_U, (hf + 1) * HALF_U)
        yh = yh + d_ref[:, cols] * u2[:, cols]
        y_ref[:, :, cols] = yh.reshape(tc, bsz, HALF_U)


def _s5(u_tm3, a_bc, b_cat, c_cat, d_skip):
    s_len, bsz, _ = u_tm3.shape
    return pl.pallas_call(
        _s5_kernel,
        grid=(s_len // SCAN_CHUNK,),
        in_specs=[
            pl.BlockSpec((SCAN_CHUNK, bsz, SSM_WIDTH), lambda c: (c, 0, 0)),
            pl.BlockSpec((2, bsz, 2 * HALF_STATE), lambda c: (0, 0, 0)),
            pl.BlockSpec((2, HALF_U, 2 * HALF_STATE), lambda c: (0, 0, 0)),
            pl.BlockSpec((2, 2 * HALF_STATE, HALF_U), lambda c: (0, 0, 0)),
            pl.BlockSpec((1, SSM_WIDTH), lambda c: (0, 0)),
        ],
        out_specs=pl.BlockSpec((SCAN_CHUNK, bsz, SSM_WIDTH), lambda c: (c, 0, 0)),
        out_shape=jax.ShapeDtypeStruct((s_len, bsz, SSM_WIDTH), F32),
        scratch_shapes=[
            pltpu.VMEM((SCAN_CHUNK, bsz, 2 * HALF_STATE), F32),
            pltpu.VMEM((2, bsz, 2 * HALF_STATE), F32),
        ],
        compiler_params=_cparams(1),
        name="s5",
    )(u_tm3, a_bc, b_cat, c_cat, d_skip)


def _s5_params(a_re, a_im, log_dt, b_re, b_im, c_re, c_im, bsz):
    dt = jnp.exp(log_dt)[:, None]
    mag = jnp.exp(a_re * dt)
    abar_r = mag * jnp.cos(a_im * dt)
    abar_i = mag * jnp.sin(a_im * dt)
    den = a_re * a_re + a_im * a_im
    nr = abar_r - 1.0
    ni = abar_i
    fr = (nr * a_re + ni * a_im) / den
    fi = (ni * a_re - nr * a_im) / den
    bbar_r = fr[..., None] * b_re - fi[..., None] * b_im
    bbar_i = fr[..., None] * b_im + fi[..., None] * b_re
    gh = SSM_GROUPS // 2
    eye = jnp.eye(gh, dtype=F32)

    def a_half(hf):
        sl = slice(hf * gh, (hf + 1) * gh)
        row = jnp.concatenate([abar_r[sl].reshape(-1), abar_i[sl].reshape(-1)])
        return jnp.broadcast_to(row[None], (bsz, 2 * HALF_STATE))

    def b_half(bb, hf):
        blk = bb[hf * gh:(hf + 1) * gh]
        return jnp.einsum('gpc,gk->gckp', blk, eye).reshape(HALF_U, HALF_STATE)

    def c_half(cc, hf):
        blk = cc[hf * gh:(hf + 1) * gh]
        return jnp.einsum('gcp,gk->gpkc', blk, eye).reshape(HALF_STATE, HALF_U)

    a_bc = jnp.stack([a_half(0), a_half(1)])
    b_cat = jnp.stack([jnp.concatenate([b_half(bbar_r, hf), b_half(bbar_i, hf)], axis=1)
                       for hf in range(2)]).astype(BF16)
    c_cat = jnp.stack([jnp.concatenate([c_half(c_re, hf), -c_half(c_im, hf)], axis=0)
                       for hf in range(2)]).astype(BF16)
    return a_bc, b_cat, c_cat


def _merge_kernel(x_ref, o_ref, y_ref, gates_ref, wab_ref, wglu_ref, bglu_ref, wout_ref,
                  gffn_ref, wrh_ref, wrl_ref, br_ref,
                  x1_ref, xn_ref, idx_ref, gate_ref, rank_ref, cnt_ref, carry_ref):
    @pl.when(pl.program_id(0) == 0)
    def _():
        carry_ref[...] = jnp.zeros_like(carry_ref)

    attn_branch = jnp.dot(o_ref[...].astype(BF16), wab_ref[...], preferred_element_type=F32)
    zg = jnp.dot(jax.nn.gelu(y_ref[...]).astype(BF16), wglu_ref[...],
                 preferred_element_type=F32) + bglu_ref[...]
    ssm_branch = zg[:, :D_MODEL] * jax.nn.sigmoid(zg[:, D_MODEL:])
    mixed = (jax.nn.sigmoid(gates_ref[:, :D_MODEL]) * attn_branch
             + jax.nn.sigmoid(gates_ref[:, D_MODEL:]) * ssm_branch)
    x1 = x_ref[...] + jnp.dot(mixed.astype(BF16), wout_ref[...], preferred_element_type=F32)
    x1_ref[...] = x1
    xn = _rms(x1, gffn_ref[...])
    xn_ref[...] = xn

    xh = xn.astype(BF16)
    xl = (xn - xh.astype(F32)).astype(BF16)
    logits = (jnp.dot(xh, wrh_ref[...], preferred_element_type=F32)
              + jnp.dot(xl, wrh_ref[...], preferred_element_type=F32)
              + jnp.dot(xh, wrl_ref[...], preferred_element_type=F32)) + br_ref[...]

    tm = logits.shape[0]
    e_iota = lax.broadcasted_iota(jnp.int32, (tm, N_EXPERTS), 1).astype(F32)
    k_iota = lax.broadcasted_iota(jnp.int32, (tm, TOP_K), 1)
    rr = lax.broadcasted_iota(jnp.int32, (tm, tm), 0)
    cc = lax.broadcasted_iota(jnp.int32, (tm, tm), 1)
    tri = jnp.where(cc < rr, 1.0, 0.0).astype(BF16)

    work = logits
    picks, vals = [], []
    sel = jnp.zeros((tm, N_EXPERTS), F32)
    for _ in range(TOP_K):
        m = jnp.max(work, axis=-1, keepdims=True)
        pick = jnp.min(jnp.where(work == m, e_iota, float(N_EXPERTS)), axis=-1, keepdims=True)
        hit = e_iota == pick
        work = jnp.where(hit, -jnp.inf, work)
        sel = sel + jnp.where(hit, 1.0, 0.0)
        picks.append(pick)
        vals.append(m)
    before = jnp.dot(tri, sel.astype(BF16), preferred_element_type=F32) + carry_ref[...]
    exps = [jnp.exp(v - vals[0]) for v in vals]
    denom = exps[0] + exps[1] + exps[2] + exps[3]
    idx = jnp.zeros((tm, TOP_K), jnp.int32)
    gate = jnp.zeros((tm, TOP_K), F32)
    rank = jnp.zeros((tm, TOP_K), jnp.int32)
    for k in range(TOP_K):
        rk = jnp.sum(jnp.where(e_iota == picks[k], before, 0.0), axis=-1, keepdims=True)
        idx = jnp.where(k_iota == k, picks[k].astype(jnp.int32), idx)
        gate = jnp.where(k_iota == k, exps[k] / denom, gate)
        rank = jnp.where(k_iota == k, rk.astype(jnp.int32), rank)
    idx_ref[...] = idx
    gate_ref[...] = gate
    rank_ref[...] = rank
    carry_ref[...] += jnp.sum(sel, axis=0, keepdims=True)
    cnt_ref[...] = carry_ref[...]


def _merge(x2d, o2d, y_tm2, gates, wab, wglu, bglu, wout, gffn, wrh, wrl, br, s_len):
    n = x2d.shape[0]
    tiles_per_seq = s_len // TOK_TILE
    full = lambda shape: pl.BlockSpec(shape, lambda i: (0,) * len(shape))
    row = lambda w: pl.BlockSpec((TOK_TILE, w), lambda i: (i, 0))
    return pl.pallas_call(
        _merge_kernel,
        grid=(n // TOK_TILE,),
        in_specs=[
            row(D_MODEL), row(SB_WIDTH),
            pl.BlockSpec((TOK_TILE, SSM_WIDTH),
                         lambda i: (i % tiles_per_seq, i // tiles_per_seq)),
            row(2 * D_MODEL),
            full((SB_WIDTH, D_MODEL)), full((SSM_WIDTH, 2 * D_MODEL)), full((1, 2 * D_MODEL)),
            full((D_MODEL, D_MODEL)), full((1, D_MODEL)),
            full((D_MODEL, N_EXPERTS)), full((D_MODEL, N_EXPERTS)), full((1, N_EXPERTS)),
        ],
        out_specs=[row(D_MODEL), row(D_MODEL), row(TOP_K), row(TOP_K), row(TOP_K),
                   full((1, N_EXPERTS))],
        out_shape=[
            jax.ShapeDtypeStruct((n, D_MODEL), F32),
            jax.ShapeDtypeStruct((n, D_MODEL), F32),
            jax.ShapeDtypeStruct((n, TOP_K), jnp.int32),
            jax.ShapeDtypeStruct((n, TOP_K), F32),
            jax.ShapeDtypeStruct((n, TOP_K), jnp.int32),
            jax.ShapeDtypeStruct((1, N_EXPERTS), F32),
        ],
        scratch_shapes=[pltpu.VMEM((1, N_EXPERTS), F32)],
        compiler_params=_cparams(1),
        name="merge",
    )(x2d, o2d, y_tm2, gates, wab, wglu, bglu, wout, gffn, wrh, wrl, br)


def _dispatch_kernel(slot_ref, xn_ref, xs_ref, sem):
    def issue(r, _):
        t = r % TOK_TILE
        pltpu.make_async_copy(xn_ref.at[pl.ds(t, 1)],
                              xs_ref.at[pl.ds(slot_ref[r], 1)], sem).start()
        return 0

    lax.fori_loop(0, TOP_K * TOK_TILE, issue, 0, unroll=8)
    for _ in range(TOP_K):
        pltpu.make_async_copy(xn_ref, xs_ref.at[pl.ds(0, TOK_TILE)], sem).wait()


def _dispatch(slots_flat, xn):
    n = xn.shape[0]
    return pl.pallas_call(
        _dispatch_kernel,
        grid=(n // TOK_TILE,),
        in_specs=[
            pl.BlockSpec((TOP_K * TOK_TILE,), lambda i: (i,), memory_space=pltpu.SMEM),
            pl.BlockSpec((TOK_TILE, D_MODEL), lambda i: (i, 0)),
        ],
        out_specs=pl.BlockSpec(memory_space=pl.ANY),
        out_shape=jax.ShapeDtypeStruct((n * TOP_K, D_MODEL), F32),
        scratch_shapes=[pltpu.SemaphoreType.DMA],
        compiler_params=_cparams(1),
        name="dispatch",
    )(slots_flat, xn)


FLAG_VALID, FLAG_FIRST_VISIT, FLAG_NEW_EXPERT = 1, 2, 4


def _experts_kernel(blk_ref, exp_ref, flag_ref, start_ref,
                    xs_ref, wup_ref, bup_ref, wdn_ref, bdn_ref, ys_ref,
                    wup_bf_ref, wdn_bf_ref):
    i = pl.program_id(0)
    flags = flag_ref[i]

    @pl.when((flags & FLAG_NEW_EXPERT) != 0)
    def _():
        wup_bf_ref[...] = wup_ref[0].astype(BF16)
        wdn_bf_ref[...] = wdn_ref[0].astype(BF16)

    @pl.when((flags & FLAG_VALID) != 0)
    def _():
        e = exp_ref[i]
        gu = jnp.dot(xs_ref[...].astype(BF16), wup_bf_ref[...],
                     preferred_element_type=F32) + bup_ref[0]
        g = jnp.minimum(gu[:, :D_FF], SWIGLU_LIMIT)
        up = jnp.clip(gu[:, D_FF:], -SWIGLU_LIMIT, SWIGLU_LIMIT)
        act = (up + 1.0) * g * jax.nn.sigmoid(SWIGLU_ALPHA * g)
        y = jnp.dot(act.astype(BF16), wdn_bf_ref[...],
                    preferred_element_type=F32) + bdn_ref[0]
        row = blk_ref[i] * ROW_BLK + lax.broadcasted_iota(jnp.int32, (ROW_BLK, 1), 0)
        mine = (row >= start_ref[e]) & (row < start_ref[e + 1])

        @pl.when((flags & FLAG_FIRST_VISIT) != 0)
        def _():
            ys_ref[...] = jnp.where(mine, y, 0.0)

        @pl.when((flags & FLAG_FIRST_VISIT) == 0)
        def _():
            ys_ref[...] = jnp.where(mine, y, ys_ref[...])


def _experts(item_blk, item_exp, item_flag, starts, xs, w_up, b_up3, w_down, b_down3):
    rows = xs.shape[0]
    n_items = item_blk.shape[0]
    grid_spec = pltpu.PrefetchScalarGridSpec(
        num_scalar_prefetch=4,
        grid=(n_items,),
        in_specs=[
            pl.BlockSpec((ROW_BLK, D_MODEL), lambda i, b, e, f, s: (b[i], 0)),
            pl.BlockSpec((1, D_MODEL, 2 * D_FF), lambda i, b, e, f, s: (e[i], 0, 0)),
            pl.BlockSpec((1, 1, 2 * D_FF), lambda i, b, e, f, s: (e[i], 0, 0)),
            pl.BlockSpec((1, D_FF, D_MODEL), lambda i, b, e, f, s: (e[i], 0, 0)),
            pl.BlockSpec((1, 1, D_MODEL), lambda i, b, e, f, s: (e[i], 0, 0)),
        ],
        out_specs=pl.BlockSpec((ROW_BLK, D_MODEL), lambda i, b, e, f, s: (b[i], 0)),
        scratch_shapes=[
            pltpu.VMEM((D_MODEL, 2 * D_FF), BF16),
            pltpu.VMEM((D_FF, D_MODEL), BF16),
        ],
    )
    return pl.pallas_call(
        _experts_kernel,
        grid_spec=grid_spec,
        out_shape=jax.ShapeDtypeStruct((rows, D_MODEL), F32),
        compiler_params=_cparams(1),
        name="experts",
    )(item_blk, item_exp, item_flag, starts, xs, w_up, b_up3, w_down, b_down3)


def _build_items(counts, n_rows):
    ends = jnp.cumsum(counts)
    starts = ends - counts
    n_blocks = n_rows // ROW_BLK
    n_items = n_blocks + N_ITEMS_EXTRA
    lo = jnp.arange(n_blocks, dtype=jnp.int32)[:, None] * ROW_BLK
    overlap = jnp.minimum(ends[None, :], lo + ROW_BLK) - jnp.maximum(starts[None, :], lo)
    live = (overlap > 0).reshape(-1)
    n_live = jnp.sum(live.astype(jnp.int32))
    (pos,) = jnp.nonzero(live, size=n_items, fill_value=0)
    pos = pos.astype(jnp.int32)
    k = jnp.arange(n_items, dtype=jnp.int32)
    valid = k < n_live
    pos = jnp.where(valid, pos, pos[jnp.maximum(n_live - 1, 0)])
    blk = pos // N_EXPERTS
    exp = pos % N_EXPERTS
    prev_blk = jnp.concatenate([jnp.full((1,), -1, jnp.int32), blk[:-1]])
    prev_exp = jnp.concatenate([jnp.full((1,), -1, jnp.int32), exp[:-1]])
    flags = (jnp.where(valid, FLAG_VALID, 0)
             | jnp.where(valid & (blk != prev_blk), FLAG_FIRST_VISIT, 0)
             | jnp.where(valid & (exp != prev_exp), FLAG_NEW_EXPERT, 0)).astype(jnp.int32)
    starts33 = jnp.concatenate([starts, ends[-1:]]).astype(jnp.int32)
    return blk, exp, flags, starts33


def _combine_kernel(slot_ref, ys_ref, x1_ref, gate_ref, p_ref, gpg_ref, wpg_ref, wpp_ref,
                    gpp_ref, out_ref, rows_ref, sem):
    def issue(r, _):
        k = r // TOK_TILE
        t = r % TOK_TILE
        pltpu.make_async_copy(ys_ref.at[pl.ds(slot_ref[r], 1)],
                              rows_ref.at[k, pl.ds(t, 1)], sem).start()
        return 0

    lax.fori_loop(0, TOP_K * TOK_TILE, issue, 0, unroll=8)
    ple = _rms(jnp.dot(p_ref[...].astype(BF16), wpp_ref[...], preferred_element_type=F32),
               gpp_ref[...])
    for k in range(TOP_K):
        pltpu.make_async_copy(ys_ref.at[pl.ds(0, TOK_TILE)], rows_ref.at[k], sem).wait()
    x2 = x1_ref[...]
    gate = gate_ref[...]
    for k in range(TOP_K):
        x2 = x2 + gate[:, k:k + 1] * rows_ref[k]
    pg = jax.nn.sigmoid(jnp.dot(_rms(x2, gpg_ref[...]).astype(BF16), wpg_ref[...],
                                preferred_element_type=F32))
    out_ref[...] = x2 + pg * ple


def _combine(slots_flat, ys, x1, gate, p2d, gpg, wpg, wpp, gpp):
    n = x1.shape[0]
    full = lambda shape: pl.BlockSpec(shape, lambda i: (0,) * len(shape))
    row = lambda w: pl.BlockSpec((TOK_TILE, w), lambda i: (i, 0))
    return pl.pallas_call(
        _combine_kernel,
        grid=(n // TOK_TILE,),
        in_specs=[
            pl.BlockSpec((TOP_K * TOK_TILE,), lambda i: (i,), memory_space=pltpu.SMEM),
            pl.BlockSpec(memory_space=pl.ANY),
            row(D_MODEL), row(TOP_K), row(PLE_DIM),
            full((1, D_MODEL)), full((D_MODEL, D_MODEL)), full((PLE_DIM, D_MODEL)),
            full((1, D_MODEL)),
        ],
        out_specs=row(D_MODEL),
        out_shape=jax.ShapeDtypeStruct((n, D_MODEL), F32),
        scratch_shapes=[pltpu.VMEM((TOP_K, TOK_TILE, D_MODEL), F32),
                        pltpu.SemaphoreType.DMA],
        compiler_params=_cparams(1),
        name="combine",
    )(slots_flat, ys, x1, gate, p2d, gpg, wpg, wpp, gpp)


def kernel(x, p, g_mix, w_in, g_q, g_k, w_attn_branch, a_re, a_im, log_dt, b_re, b_im, c_re, c_im, d_skip, w_glu, b_glu, w_out, g_ffn, w_router, b_router, w_up, b_up, w_down, b_down, g_ple_gate, w_ple_gate, w_ple_proj, g_ple_post):
    bsz, s_len, d = x.shape
    depth = w_in.shape[0]
    n = bsz * s_len
    assert d == D_MODEL and s_len % TOK_TILE == 0 and s_len % SCAN_CHUNK == 0
    assert bsz == SUBLANES, "the S5 scan keeps the batch on the sublane axis"

    x2d = x.reshape(n, d)
    for i in range(depth):
        qkv, u_tm, gates = _inproj(x2d, g_mix[i][None], w_in[i].astype(BF16), bsz, s_len)
        o = _attn(qkv.reshape(bsz, s_len, 3 * SB_WIDTH),
                  jnp.tile(g_q[i], 2)[None], jnp.tile(g_k[i], 2)[None])
        a_bc, b_cat, c_cat = _s5_params(a_re[i], a_im[i], log_dt[i], b_re[i], b_im[i],
                                        c_re[i], c_im[i], bsz)
        y_tm = _s5(u_tm.reshape(s_len, bsz, SSM_WIDTH), a_bc, b_cat, c_cat, d_skip[i][None])

        wr = w_router[i]
        wr_hi = wr.astype(BF16)
        wr_lo = (wr - wr_hi.astype(F32)).astype(BF16)
        x1, xn, idx, gate, rank, counts = _merge(
            x2d, o.reshape(n, SB_WIDTH), y_tm.reshape(s_len, bsz * SSM_WIDTH), gates,
            w_attn_branch[i].astype(BF16), w_glu[i].astype(BF16), b_glu[i][None],
            w_out[i].astype(BF16), g_ffn[i][None], wr_hi, wr_lo, b_router[i][None], s_len)

        counts_i = counts[0].astype(jnp.int32)
        blk, exp, flags, starts33 = _build_items(counts_i, n * TOP_K)
        slots = starts33[idx] + rank
        slots_flat = slots.reshape(n // TOK_TILE, TOK_TILE, TOP_K).transpose(0, 2, 1).reshape(-1)

        xs = _dispatch(slots_flat, xn)
        ys = _experts(blk, exp, flags, starts33, xs, w_up[i], b_up[i][:, None, :],
                      w_down[i], b_down[i][:, None, :])
        x2d = _combine(slots_flat, ys, x1, gate, p[i].reshape(n, PLE_DIM),
                       g_ple_gate[i][None], w_ple_gate[i].astype(BF16),
                       w_ple_proj[i].astype(BF16), g_ple_post[i][None])
    return x2d.reshape(bsz, s_len, d)
```

```python
import functools
import math

import jax
import jax.numpy as jnp
from jax import lax
from jax.experimental import pallas as pl
from jax.experimental.pallas import tpu as pltpu

F32 = jnp.float32
BF16 = jnp.bfloat16

D_MODEL = 1024
SB_HEADS = 8
SB_HEAD_DIM = 64
SB_WIDTH = SB_HEADS * SB_HEAD_DIM
SSM_GROUP = 16
SSM_WIDTH = 512
SSM_GROUPS = SSM_WIDTH // SSM_GROUP
SSM_STATE = 64
PLE_DIM = 256
N_EXPERTS = 32
TOP_K = 4
D_FF = D_MODEL
SWIGLU_LIMIT = 7.0
SWIGLU_ALPHA = 1.702
EPS = 1e-6

LANES = 128
SUBLANES = 8
VMEM_LIMIT = 56 * 1024 * 1024

TOK_TILE = 256
TILE_GROUPS = TOK_TILE // SUBLANES
ATT_BLK = 128
DEAD_LOG_WEIGHT = -104.0
SCAN_CHUNK = 64
SCAN_LANES = 512
HALF_U = SSM_WIDTH // 2
HALF_STATE = SSM_GROUPS // 2 * SSM_STATE
ROW_BLK = 256
N_ITEMS_EXTRA = N_EXPERTS - 1


def _cparams(n_axes):
    return pltpu.CompilerParams(
        dimension_semantics=("arbitrary",) * n_axes,
        vmem_limit_bytes=VMEM_LIMIT)


def _rms(x, g):
    ms = jnp.mean(x * x, axis=-1, keepdims=True)
    return x * lax.rsqrt(ms + EPS) * g


def _inproj_kernel(x_ref, g_ref, w_ref, qkv_ref, u_ref, gates_ref):
    h = _rms(x_ref[...], g_ref[...]).astype(BF16)
    n_qkv = 3 * SB_WIDTH
    n_u = n_qkv + SSM_WIDTH
    qkv_ref[...] = jnp.dot(h, w_ref[:, :n_qkv], preferred_element_type=F32)
    u_ref[...] = jnp.dot(h, w_ref[:, n_qkv:n_u], preferred_element_type=F32)
    gates_ref[...] = jnp.dot(h, w_ref[:, n_u:], preferred_element_type=F32)


def _inproj(x2d, g_mix, w_in_bf, bsz, s_len):
    n = x2d.shape[0]
    tiles_per_seq = s_len // TOK_TILE
    in_cols = w_in_bf.shape[1]
    return pl.pallas_call(
        _inproj_kernel,
        grid=(n // TOK_TILE,),
        in_specs=[
            pl.BlockSpec((TOK_TILE, D_MODEL), lambda i: (i, 0)),
            pl.BlockSpec((1, D_MODEL), lambda i: (0, 0)),
            pl.BlockSpec((D_MODEL, in_cols), lambda i: (0, 0)),
        ],
        out_specs=[
            pl.BlockSpec((TOK_TILE, 3 * SB_WIDTH), lambda i: (i, 0)),
            pl.BlockSpec((TOK_TILE, SSM_WIDTH),
                         lambda i: (i % tiles_per_seq, i // tiles_per_seq)),
            pl.BlockSpec((TOK_TILE, 2 * D_MODEL), lambda i: (i, 0)),
        ],
        out_shape=[
            jax.ShapeDtypeStruct((n, 3 * SB_WIDTH), F32),
            jax.ShapeDtypeStruct((s_len, bsz * SSM_WIDTH), F32),
            jax.ShapeDtypeStruct((n, 2 * D_MODEL), F32),
        ],
        compiler_params=_cparams(1),
        name="inproj",
    )(x2d, g_mix, w_in_bf)


def _attn_kernel(q_ref, k_ref, v_ref, gq_ref, gk_ref, o_ref,
                 qs_ref, ks_ref, vs_ref, w2_ref, c_ref, acc_ref):
    s_len = q_ref.shape[1]
    n_blk = s_len // ATT_BLK
    lane = lax.broadcasted_iota(jnp.int32, (1, LANES), 1)
    head0 = lane < SB_HEAD_DIM

    def head_rms(t, g):
        sq = t * t
        s0 = jnp.sum(jnp.where(head0, sq, 0.0), axis=-1, keepdims=True)
        s1 = jnp.sum(jnp.where(head0, 0.0, sq), axis=-1, keepdims=True)
        ms = jnp.where(head0, s0, s1) * (1.0 / SB_HEAD_DIM)
        return t * lax.rsqrt(ms + EPS) * g

    scale = SB_HEAD_DIM ** -0.5
    norm_rows = 256

    def prep(i, _):
        rows = pl.ds(pl.multiple_of(i * norm_rows, norm_rows), norm_rows)
        qn = head_rms(q_ref[0, rows, :], gq_ref[...]) * scale
        qs_ref[0, rows, :] = jnp.where(head0, qn, 0.0).astype(BF16)
        qs_ref[1, rows, :] = jnp.where(head0, 0.0, qn).astype(BF16)
        ks_ref[rows, :] = head_rms(k_ref[0, rows, :], gk_ref[...]).astype(BF16)
        vs_ref[rows, :] = v_ref[0, rows, :].astype(BF16)
        return 0

    lax.fori_loop(0, s_len // norm_rows, prep, 0)

    r = lax.broadcasted_iota(jnp.int32, (2 * ATT_BLK, 2 * ATT_BLK), 0)
    c = lax.broadcasted_iota(jnp.int32, (2 * ATT_BLK, 2 * ATT_BLK), 1)
    r = jnp.where(r >= ATT_BLK, r - ATT_BLK, r)
    w2_ref[...] = jnp.where((c >= ATT_BLK) | (r > c), 1.0, 0.0).astype(BF16)

    ti = lax.broadcasted_iota(jnp.int32, (ATT_BLK, ATT_BLK), 0)
    si = lax.broadcasted_iota(jnp.int32, (ATT_BLK, ATT_BLK), 1)
    causal = si < ti

    def tile(h, q_rows, kv_rows, diag):
        qh = qs_ref[h, q_rows, :]
        z = lax.dot_general(qh, ks_ref[kv_rows, :], (((1,), (1,)), ((), ())),
                            preferred_element_type=F32)
        sp = jnp.maximum(z, 0.0) + jnp.log(1.0 + jnp.exp(-jnp.abs(z)))
        log_keep = -sp
        log_beta = z - sp
        if diag:
            log_keep = jnp.where(causal, log_keep, 0.0)
        hi = log_keep.astype(BF16)
        lo = (log_keep - hi.astype(F32)).astype(BF16)
        sums = jnp.dot(jnp.concatenate([hi, lo], axis=1), w2_ref[...],
                       preferred_element_type=F32)
        later = sums[:, :ATT_BLK] + c_ref[h]
        w = jnp.exp(log_beta + later)
        if diag:
            w = jnp.where(causal, w, 0.0)
        acc_ref[h] += jnp.dot(w.astype(BF16), vs_ref[kv_rows, :],
                              preferred_element_type=F32)
        c_ref[h] += sums[:, ATT_BLK:]

    def qblock(qi, _):
        q_rows = pl.ds(pl.multiple_of(qi * ATT_BLK, ATT_BLK), ATT_BLK)
        c_ref[...] = jnp.zeros_like(c_ref)
        acc_ref[...] = jnp.zeros_like(acc_ref)
        tile(0, q_rows, q_rows, True)
        tile(1, q_rows, q_rows, True)

        def live(carry):
            jj, c_max = carry
            return (jj < qi) & (c_max > DEAD_LOG_WEIGHT)

        def kvblock(carry):
            jj, _ = carry
            j = qi - 1 - jj
            kv_rows = pl.ds(pl.multiple_of(j * ATT_BLK, ATT_BLK), ATT_BLK)
            tile(0, q_rows, kv_rows, False)
            tile(1, q_rows, kv_rows, False)
            return jj + 1, jnp.max(c_ref[...])

        lax.while_loop(live, kvblock, (jnp.int32(0), jnp.max(c_ref[...])))
        o_ref[0, q_rows, :] = jnp.where(head0, acc_ref[0], acc_ref[1])
        return 0

    lax.fori_loop(0, n_blk, qblock, 0)


def _attn(qkv3, gq2, gk2):
    bsz, s_len, _ = qkv3.shape
    n_pairs = SB_WIDTH // LANES
    blk = (1, s_len, LANES)
    return pl.pallas_call(
        _attn_kernel,
        grid=(bsz, n_pairs),
        in_specs=[
            pl.BlockSpec(blk, lambda b, p: (b, 0, p)),
            pl.BlockSpec(blk, lambda b, p: (b, 0, n_pairs + p)),
            pl.BlockSpec(blk, lambda b, p: (b, 0, 2 * n_pairs + p)),
            pl.BlockSpec((1, LANES), lambda b, p: (0, 0)),
            pl.BlockSpec((1, LANES), lambda b, p: (0, 0)),
        ],
        out_specs=pl.BlockSpec(blk, lambda b, p: (b, 0, p)),
        out_shape=jax.ShapeDtypeStruct((bsz, s_len, SB_WIDTH), F32),
        scratch_shapes=[
            pltpu.VMEM((2, s_len, LANES), BF16),
            pltpu.VMEM((s_len, LANES), BF16),
            pltpu.VMEM((s_len, LANES), BF16),
            pltpu.VMEM((2 * ATT_BLK, 2 * ATT_BLK), BF16),
            pltpu.VMEM((2, ATT_BLK, ATT_BLK), F32),
            pltpu.VMEM((2, ATT_BLK, LANES), F32),
        ],
        compiler_params=_cparams(2),
        name="attn",
    )(qkv3, qkv3, qkv3, gq2, gk2)


def _s5_kernel(u_ref, a_ref, b_ref, c_ref, d_ref, y_ref, hbuf_ref, state_ref):
    tc, bsz, _ = u_ref.shape
    rows = tc * bsz

    @pl.when(pl.program_id(0) == 0)
    def _():
        state_ref[...] = jnp.zeros_like(state_ref)

    u2 = u_ref[...].reshape(rows, SSM_WIDTH)
    ub = u2.astype(BF16)
    for hf in range(2):
        uh = ub[:, hf * HALF_U:(hf + 1) * HALF_U]
        xh = jnp.dot(uh, b_ref[hf], preferred_element_type=F32)
        hbuf_ref[...] = xh.reshape(tc, bsz, 2 * HALF_STATE)
        for lc in range(HALF_STATE // SCAN_LANES):
            re = pl.ds(lc * SCAN_LANES, SCAN_LANES)
            im = pl.ds(HALF_STATE + lc * SCAN_LANES, SCAN_LANES)
            ar = a_ref[hf, :, re]
            ai = a_ref[hf, :, im]

            def step(t, carry, re=re, im=im, ar=ar, ai=ai):
                hr, hi = carry
                nr = ar * hr - ai * hi + hbuf_ref[t, :, re]
                ni = ar * hi + ai * hr + hbuf_ref[t, :, im]
                hbuf_ref[t, :, re] = nr
                hbuf_ref[t, :, im] = ni
                return nr, ni

            hr, hi = lax.fori_loop(0, tc, step,
                                   (state_ref[hf, :, re], state_ref[hf, :, im]),
                                   unroll=8)
            state_ref[hf, :, re] = hr
            state_ref[hf, :, im] = hi
        hb = hbuf_ref[...].reshape(rows, 2 * HALF_STATE).astype(BF16)
        yh = jnp.dot(hb, c_ref[hf], preferred_element_type=F32)
        cols = slice(hf * HALF_U, (hf + 1) * HALF_U)
        yh = yh + d_ref[:, cols] * u2[:, cols]
        y_ref[:, :, cols] = yh.reshape(tc, bsz, HALF_U)


def _s5(u_tm3, a_bc, b_cat, c_cat, d_skip):
    s_len, bsz, _ = u_tm3.shape
    return pl.pallas_call(
        _s5_kernel,
        grid=(s_len // SCAN_CHUNK,),
        in_specs=[
            pl.BlockSpec((SCAN_CHUNK, bsz, SSM_WIDTH), lambda c: (c, 0, 0)),
            pl.BlockSpec((2, bsz, 2 * HALF_STATE), lambda c: (0, 0, 0)),
            pl.BlockSpec((2, HALF_U, 2 * HALF_STATE), lambda c: (0, 0, 0)),
            pl.BlockSpec((2, 2 * HALF_STATE, HALF_U), lambda c: (0, 0, 0)),
            pl.BlockSpec((1, SSM_WIDTH), lambda c: (0, 0)),
        ],
        out_specs=pl.BlockSpec((SCAN_CHUNK, bsz, SSM_WIDTH), lambda c: (c, 0, 0)),
        out_shape=jax.ShapeDtypeStruct((s_len, bsz, SSM_WIDTH), F32),
        scratch_shapes=[
            pltpu.VMEM((SCAN_CHUNK, bsz, 2 * HALF_STATE), F32),
            pltpu.VMEM((2, bsz, 2 * HALF_STATE), F32),
        ],
        compiler_params=_cparams(1),
        name="s5",
    )(u_tm3, a_bc, b_cat, c_cat, d_skip)


def _s5_params(a_re, a_im, log_dt, b_re, b_im, c_re, c_im, bsz):
    dt = jnp.exp(log_dt)[:, None]
    mag = jnp.exp(a_re * dt)
    abar_r = mag * jnp.cos(a_im * dt)
    abar_i = mag * jnp.sin(a_im * dt)
    den = a_re * a_re + a_im * a_im
    nr = abar_r - 1.0
    ni = abar_i
    fr = (nr * a_re + ni * a_im) / den
    fi = (ni * a_re - nr * a_im) / den
    bbar_r = fr[..., None] * b_re - fi[..., None] * b_im
    bbar_i = fr[..., None] * b_im + fi[..., None] * b_re
    gh = SSM_GROUPS // 2
    eye = jnp.eye(gh, dtype=F32)

    def a_half(hf):
        sl = slice(hf * gh, (hf + 1) * gh)
        row = jnp.concatenate([abar_r[sl].reshape(-1), abar_i[sl].reshape(-1)])
        return jnp.broadcast_to(row[None], (bsz, 2 * HALF_STATE))

    def b_half(bb, hf):
        blk = bb[hf * gh:(hf + 1) * gh]
        return jnp.einsum('gpc,gk->gckp', blk, eye).reshape(HALF_U, HALF_STATE)

    def c_half(cc, hf):
        blk = cc[hf * gh:(hf + 1) * gh]
        return jnp.einsum('gcp,gk->gpkc', blk, eye).reshape(HALF_STATE, HALF_U)

    a_bc = jnp.stack([a_half(0), a_half(1)])
    b_cat = jnp.stack([jnp.concatenate([b_half(bbar_r, hf), b_half(bbar_i, hf)], axis=1)
                       for hf in range(2)]).astype(BF16)
    c_cat = jnp.stack([jnp.concatenate([c_half(c_re, hf), -c_half(c_im, hf)], axis=0)
                       for hf in range(2)]).astype(BF16)
    return a_bc, b_cat, c_cat


def _merge_kernel(x_ref, o_ref, y_ref, gates_ref, wab_ref, wglu_ref, bglu_ref, wout_ref,
                  gffn_ref, wrh_ref, wrl_ref, br_ref,
                  x1_ref, xn_ref, idx_ref, gate_ref, rank_ref, cnt_ref, carry_ref):
    @pl.when(pl.program_id(0) == 0)
    def _():
        carry_ref[...] = jnp.zeros_like(carry_ref)

    attn_branch = jnp.dot(o_ref[...].astype(BF16), wab_ref[...], preferred_element_type=F32)
    zg = jnp.dot(jax.nn.gelu(y_ref[...]).astype(BF16), wglu_ref[...],
                 preferred_element_type=F32) + bglu_ref[...]
    ssm_branch = zg[:, :D_MODEL] * jax.nn.sigmoid(zg[:, D_MODEL:])
    mixed = (jax.nn.sigmoid(gates_ref[:, :D_MODEL]) * attn_branch
             + jax.nn.sigmoid(gates_ref[:, D_MODEL:]) * ssm_branch)
    x1 = x_ref[...] + jnp.dot(mixed.astype(BF16), wout_ref[...], preferred_element_type=F32)
    x1_ref[...] = x1
    xn = _rms(x1, gffn_ref[...])
    xn_ref[...] = xn

    xh = xn.astype(BF16)
    xl = (xn - xh.astype(F32)).astype(BF16)
    logits = (jnp.dot(xh, wrh_ref[...], preferred_element_type=F32)
              + jnp.dot(xl, wrh_ref[...], preferred_element_type=F32)
              + jnp.dot(xh, wrl_ref[...], preferred_element_type=F32)) + br_ref[...]

    tm = logits.shape[0]
    e_iota = lax.broadcasted_iota(jnp.int32, (tm, N_EXPERTS), 1).astype(F32)
    k_iota = lax.broadcasted_iota(jnp.int32, (tm, TOP_K), 1)
    rr = lax.broadcasted_iota(jnp.int32, (tm, tm), 0)
    cc = lax.broadcasted_iota(jnp.int32, (tm, tm), 1)
    tri = jnp.where(cc < rr, 1.0, 0.0).astype(BF16)

    work = logits
    picks, vals = [], []
    sel = jnp.zeros((tm, N_EXPERTS), F32)
    for _ in range(TOP_K):
        m = jnp.max(work, axis=-1, keepdims=True)
        pick = jnp.min(jnp.where(work == m, e_iota, float(N_EXPERTS)), axis=-1, keepdims=True)
        hit = e_iota == pick
        work = jnp.where(hit, -jnp.inf, work)
        sel = sel + jnp.where(hit, 1.0, 0.0)
        picks.append(pick)
        vals.append(m)
    before = jnp.dot(tri, sel.astype(BF16), preferred_element_type=F32) + carry_ref[...]
    exps = [jnp.exp(v - vals[0]) for v in vals]
    denom = exps[0] + exps[1] + exps[2] + exps[3]
    idx = jnp.zeros((tm, TOP_K), jnp.int32)
    gate = jnp.zeros((tm, TOP_K), F32)
    rank = jnp.zeros((tm, TOP_K), jnp.int32)
    for k in range(TOP_K):
        rk = jnp.sum(jnp.where(e_iota == picks[k], before, 0.0), axis=-1, keepdims=True)
        idx = jnp.where(k_iota == k, picks[k].astype(jnp.int32), idx)
        gate = jnp.where(k_iota == k, exps[k] / denom, gate)
        rank = jnp.where(k_iota == k, rk.astype(jnp.int32), rank)
    idx_ref[...] = idx
    gate_ref[...] = gate
    rank_ref[...] = rank
    carry_ref[...] += jnp.sum(sel, axis=0, keepdims=True)
    cnt_ref[...] = carry_ref[...]


def _merge(x2d, o2d, y_tm2, gates, wab, wglu, bglu, wout, gffn, wrh, wrl, br, s_len):
    n = x2d.shape[0]
    tiles_per_seq = s_len // TOK_TILE
    full = lambda shape: pl.BlockSpec(shape, lambda i: (0,) * len(shape))
    row = lambda w: pl.BlockSpec((TOK_TILE, w), lambda i: (i, 0))
    return pl.pallas_call(
        _merge_kernel,
        grid=(n // TOK_TILE,),
        in_specs=[
            row(D_MODEL), row(SB_WIDTH),
            pl.BlockSpec((TOK_TILE, SSM_WIDTH),
                         lambda i: (i % tiles_per_seq, i // tiles_per_seq)),
            row(2 * D_MODEL),
            full((SB_WIDTH, D_MODEL)), full((SSM_WIDTH, 2 * D_MODEL)), full((1, 2 * D_MODEL)),
            full((D_MODEL, D_MODEL)), full((1, D_MODEL)),
            full((D_MODEL, N_EXPERTS)), full((D_MODEL, N_EXPERTS)), full((1, N_EXPERTS)),
        ],
        out_specs=[row(D_MODEL), row(D_MODEL), row(TOP_K), row(TOP_K), row(TOP_K),
                   full((1, N_EXPERTS))],
        out_shape=[
            jax.ShapeDtypeStruct((n, D_MODEL), F32),
            jax.ShapeDtypeStruct((n, D_MODEL), F32),
            jax.ShapeDtypeStruct((n, TOP_K), jnp.int32),
            jax.ShapeDtypeStruct((n, TOP_K), F32),
            jax.ShapeDtypeStruct((n, TOP_K), jnp.int32),
            jax.ShapeDtypeStruct((1, N_EXPERTS), F32),
        ],
        scratch_shapes=[pltpu.VMEM((1, N_EXPERTS), F32)],
        compiler_params=_cparams(1),
        name="merge",
    )(x2d, o2d, y_tm2, gates, wab, wglu, bglu, wout, gffn, wrh, wrl, br)


def _slot_row(ref, slot):
    return ref.at[slot >> 3, pl.ds(slot & (SUBLANES - 1), 1)]


def _dispatch_kernel(slot_ref, xn_ref, xs_ref, sem):
    def issue(g, _):
        for k in range(TOP_K):
            for j in range(SUBLANES):
                slot = slot_ref[k * TOK_TILE + g * SUBLANES + j]
                pltpu.make_async_copy(xn_ref.at[g, pl.ds(j, 1)], _slot_row(xs_ref, slot),
                                      sem).start(priority=j % 2)
        return 0

    lax.fori_loop(0, TILE_GROUPS, issue, 0)
    for _ in range(TOP_K):
        pltpu.make_async_copy(xn_ref, xs_ref.at[pl.ds(0, TILE_GROUPS)], sem).wait()


def _dispatch(slots_flat, xn3):
    n = xn3.shape[0] * SUBLANES
    return pl.pallas_call(
        _dispatch_kernel,
        grid=(n // TOK_TILE,),
        in_specs=[
            pl.BlockSpec((TOP_K * TOK_TILE,), lambda i: (i,), memory_space=pltpu.SMEM),
            pl.BlockSpec((TILE_GROUPS, SUBLANES, D_MODEL), lambda i: (i, 0, 0)),
        ],
        out_specs=pl.BlockSpec(memory_space=pl.ANY),
        out_shape=jax.ShapeDtypeStruct((n * TOP_K // SUBLANES, SUBLANES, D_MODEL), F32),
        scratch_shapes=[pltpu.SemaphoreType.DMA],
        compiler_params=_cparams(1),
        name="dispatch",
    )(slots_flat, xn3)


FLAG_VALID, FLAG_FIRST_VISIT, FLAG_NEW_EXPERT = 1, 2, 4


def _experts_kernel(blk_ref, exp_ref, flag_ref, start_ref,
                    xs_ref, wup_ref, bup_ref, wdn_ref, bdn_ref, ys_ref,
                    wup_bf_ref, wdn_bf_ref):
    i = pl.program_id(0)
    flags = flag_ref[i]

    @pl.when((flags & FLAG_NEW_EXPERT) != 0)
    def _():
        wup_bf_ref[...] = wup_ref[0].astype(BF16)
        wdn_bf_ref[...] = wdn_ref[0].astype(BF16)

    @pl.when((flags & FLAG_VALID) != 0)
    def _():
        e = exp_ref[i]
        gu = jnp.dot(xs_ref[...].astype(BF16), wup_bf_ref[...],
                     preferred_element_type=F32) + bup_ref[0]
        g = jnp.minimum(gu[:, :D_FF], SWIGLU_LIMIT)
        up = jnp.clip(gu[:, D_FF:], -SWIGLU_LIMIT, SWIGLU_LIMIT)
        act = (up + 1.0) * g * jax.nn.sigmoid(SWIGLU_ALPHA * g)
        y = jnp.dot(act.astype(BF16), wdn_bf_ref[...],
                    preferred_element_type=F32) + bdn_ref[0]
        row = blk_ref[i] * ROW_BLK + lax.broadcasted_iota(jnp.int32, (ROW_BLK, 1), 0)
        mine = (row >= start_ref[e]) & (row < start_ref[e + 1])

        @pl.when((flags & FLAG_FIRST_VISIT) != 0)
        def _():
            ys_ref[...] = jnp.where(mine, y, 0.0)

        @pl.when((flags & FLAG_FIRST_VISIT) == 0)
        def _():
            ys_ref[...] = jnp.where(mine, y, ys_ref[...])


def _experts(item_blk, item_exp, item_flag, starts, xs, w_up, b_up3, w_down, b_down3):
    rows = xs.shape[0]
    n_items = item_blk.shape[0]
    grid_spec = pltpu.PrefetchScalarGridSpec(
        num_scalar_prefetch=4,
        grid=(n_items,),
        in_specs=[
            pl.BlockSpec((ROW_BLK, D_MODEL), lambda i, b, e, f, s: (b[i], 0)),
            pl.BlockSpec((1, D_MODEL, 2 * D_FF), lambda i, b, e, f, s: (e[i], 0, 0)),
            pl.BlockSpec((1, 1, 2 * D_FF), lambda i, b, e, f, s: (e[i], 0, 0)),
            pl.BlockSpec((1, D_FF, D_MODEL), lambda i, b, e, f, s: (e[i], 0, 0)),
            pl.BlockSpec((1, 1, D_MODEL), lambda i, b, e, f, s: (e[i], 0, 0)),
        ],
        out_specs=pl.BlockSpec((ROW_BLK, D_MODEL), lambda i, b, e, f, s: (b[i], 0)),
        scratch_shapes=[
            pltpu.VMEM((D_MODEL, 2 * D_FF), BF16),
            pltpu.VMEM((D_FF, D_MODEL), BF16),
        ],
    )
    return pl.pallas_call(
        _experts_kernel,
        grid_spec=grid_spec,
        out_shape=jax.ShapeDtypeStruct((rows, D_MODEL), F32),
        compiler_params=_cparams(1),
        name="experts",
    )(item_blk, item_exp, item_flag, starts, xs, w_up, b_up3, w_down, b_down3)


def _build_items(counts, n_rows):
    ends = jnp.cumsum(counts)
    starts = ends - counts
    n_blocks = n_rows // ROW_BLK
    n_items = n_blocks + N_ITEMS_EXTRA
    lo = jnp.arange(n_blocks, dtype=jnp.int32)[:, None] * ROW_BLK
    overlap = jnp.minimum(ends[None, :], lo + ROW_BLK) - jnp.maximum(starts[None, :], lo)
    live = (overlap > 0).reshape(-1)
    n_live = jnp.sum(live.astype(jnp.int32))
    (pos,) = jnp.nonzero(live, size=n_items, fill_value=0)
    pos = pos.astype(jnp.int32)
    k = jnp.arange(n_items, dtype=jnp.int32)
    valid = k < n_live
    pos = jnp.where(valid, pos, pos[jnp.maximum(n_live - 1, 0)])
    blk = pos // N_EXPERTS
    exp = pos % N_EXPERTS
    prev_blk = jnp.concatenate([jnp.full((1,), -1, jnp.int32), blk[:-1]])
    prev_exp = jnp.concatenate([jnp.full((1,), -1, jnp.int32), exp[:-1]])
    flags = (jnp.where(valid, FLAG_VALID, 0)
             | jnp.where(valid & (blk != prev_blk), FLAG_FIRST_VISIT, 0)
             | jnp.where(valid & (exp != prev_exp), FLAG_NEW_EXPERT, 0)).astype(jnp.int32)
    starts33 = jnp.concatenate([starts, ends[-1:]]).astype(jnp.int32)
    return blk, exp, flags, starts33


def _combine_kernel(slot_ref, ys_ref, x1_ref, gate_ref, p_ref, gpg_ref, wpg_ref, wpp_ref,
                    gpp_ref, out_ref, rows_ref, sem):
    def issue(g, _):
        for k in range(TOP_K):
            for j in range(SUBLANES):
                slot = slot_ref[k * TOK_TILE + g * SUBLANES + j]
                pltpu.make_async_copy(_slot_row(ys_ref, slot), rows_ref.at[k, g, pl.ds(j, 1)],
                                      sem).start(priority=j % 2)
        return 0

    lax.fori_loop(0, TILE_GROUPS, issue, 0)
    ple = _rms(jnp.dot(p_ref[...].astype(BF16), wpp_ref[...], preferred_element_type=F32),
               gpp_ref[...])
    for k in range(TOP_K):
        pltpu.make_async_copy(ys_ref.at[pl.ds(0, TILE_GROUPS)], rows_ref.at[k], sem).wait()
    x2 = x1_ref[...]
    gate = gate_ref[...]
    for k in range(TOP_K):
        x2 = x2 + gate[:, k:k + 1] * rows_ref[k].reshape(TOK_TILE, D_MODEL)
    pg = jax.nn.sigmoid(jnp.dot(_rms(x2, gpg_ref[...]).astype(BF16), wpg_ref[...],
                                preferred_element_type=F32))
    out_ref[...] = x2 + pg * ple


def _combine(slots_flat, ys, x1, gate, p2d, gpg, wpg, wpp, gpp):
    n = x1.shape[0]
    full = lambda shape: pl.BlockSpec(shape, lambda i: (0,) * len(shape))
    row = lambda w: pl.BlockSpec((TOK_TILE, w), lambda i: (i, 0))
    return pl.pallas_call(
        _combine_kernel,
        grid=(n // TOK_TILE,),
        in_specs=[
            pl.BlockSpec((TOP_K * TOK_TILE,), lambda i: (i,), memory_space=pltpu.SMEM),
            pl.BlockSpec(memory_space=pl.ANY),
            row(D_MODEL), row(TOP_K), row(PLE_DIM),
            full((1, D_MODEL)), full((D_MODEL, D_MODEL)), full((PLE_DIM, D_MODEL)),
            full((1, D_MODEL)),
        ],
        out_specs=row(D_MODEL),
        out_shape=jax.ShapeDtypeStruct((n, D_MODEL), F32),
        scratch_shapes=[pltpu.VMEM((TOP_K, TILE_GROUPS, SUBLANES, D_MODEL), F32),
                        pltpu.SemaphoreType.DMA],
        compiler_params=_cparams(1),
        name="combine",
    )(slots_flat, ys, x1, gate, p2d, gpg, wpg, wpp, gpp)


def kernel(x, p, g_mix, w_in, g_q, g_k, w_attn_branch, a_re, a_im, log_dt, b_re, b_im, c_re, c_im, d_skip, w_glu, b_glu, w_out, g_ffn, w_router, b_router, w_up, b_up, w_down, b_down, g_ple_gate, w_ple_gate, w_ple_proj, g_ple_post):
    bsz, s_len, d = x.shape
    depth = w_in.shape[0]
    n = bsz * s_len
    assert d == D_MODEL and s_len % TOK_TILE == 0 and s_len % SCAN_CHUNK == 0
    assert bsz == SUBLANES, "the S5 scan keeps the batch on the sublane axis"

    x2d = x.reshape(n, d)
    for i in range(depth):
        qkv, u_tm, gates = _inproj(x2d, g_mix[i][None], w_in[i].astype(BF16), bsz, s_len)
        o = _attn(qkv.reshape(bsz, s_len, 3 * SB_WIDTH),
                  jnp.tile(g_q[i], 2)[None], jnp.tile(g_k[i], 2)[None])
        a_bc, b_cat, c_cat = _s5_params(a_re[i], a_im[i], log_dt[i], b_re[i], b_im[i],
                                        c_re[i], c_im[i], bsz)
        y_tm = _s5(u_tm.reshape(s_len, bsz, SSM_WIDTH), a_bc, b_cat, c_cat, d_skip[i][None])

        wr = w_router[i]
        wr_hi = wr.astype(BF16)
        wr_lo = (wr - wr_hi.astype(F32)).astype(BF16)
        x1, xn, idx, gate, rank, counts = _merge(
            x2d, o.reshape(n, SB_WIDTH), y_tm.reshape(s_len, bsz * SSM_WIDTH), gates,
            w_attn_branch[i].astype(BF16), w_glu[i].astype(BF16), b_glu[i][None],
            w_out[i].astype(BF16), g_ffn[i][None], wr_hi, wr_lo, b_router[i][None], s_len)

        counts_i = counts[0].astype(jnp.int32)
        blk, exp, flags, starts33 = _build_items(counts_i, n * TOP_K)
        slots = starts33[idx] + rank
        slots_flat = slots.reshape(n // TOK_TILE, TOK_TILE, TOP_K).transpose(0, 2, 1).reshape(-1)

        xs3 = _dispatch(slots_flat, xn.reshape(n // SUBLANES, SUBLANES, D_MODEL))
        ys = _experts(blk, exp, flags, starts33, xs3.reshape(n * TOP_K, D_MODEL),
                      w_up[i], b_up[i][:, None, :], w_down[i], b_down[i][:, None, :])
        ys3 = ys.reshape(n * TOP_K // SUBLANES, SUBLANES, D_MODEL)
        x2d = _combine(slots_flat, ys3, x1, gate, p[i].reshape(n, PLE_DIM),
                       g_ple_gate[i][None], w_ple_gate[i].astype(BF16),
                       w_ple_proj[i].astype(BF16), g_ple_post[i][None])
    return x2d.reshape(bsz, s_len, d)
```

```python
import functools
import math

import jax
import jax.numpy as jnp
from jax import lax
from jax.experimental import pallas as pl
from jax.experimental.pallas import tpu as pltpu

F32 = jnp.float32
BF16 = jnp.bfloat16

D_MODEL = 1024
SB_HEADS = 8
SB_HEAD_DIM = 64
SB_WIDTH = SB_HEADS * SB_HEAD_DIM
SSM_GROUP = 16
SSM_WIDTH = 512
SSM_GROUPS = SSM_WIDTH // SSM_GROUP
SSM_STATE = 64
PLE_DIM = 256
N_EXPERTS = 32
TOP_K = 4
D_FF = D_MODEL
SWIGLU_LIMIT = 7.0
SWIGLU_ALPHA = 1.702
EPS = 1e-6

LANES = 128
SUBLANES = 8
VMEM_LIMIT = 56 * 1024 * 1024

TOK_TILE = 256
TILE_GROUPS = TOK_TILE // SUBLANES
ATT_BLK = 128
DEAD_LOG_WEIGHT = -104.0
ATT_HEADS_PER_STEP = 8
SCAN_CHUNK = 64
SCAN_LANES = 512
HALF_U = SSM_WIDTH // 2
HALF_STATE = SSM_GROUPS // 2 * SSM_STATE
ROW_BLK = 256
N_ITEMS_EXTRA = N_EXPERTS - 1


def _cparams(n_axes):
    return pltpu.CompilerParams(
        dimension_semantics=("arbitrary",) * n_axes,
        vmem_limit_bytes=VMEM_LIMIT)


def _rms(x, g):
    ms = jnp.mean(x * x, axis=-1, keepdims=True)
    return x * lax.rsqrt(ms + EPS) * g


def _inproj_kernel(x_ref, g_ref, w_ref, qkv_ref, u_ref, gates_ref):
    h = _rms(x_ref[...], g_ref[...]).astype(BF16)
    n_qkv = 3 * SB_WIDTH
    n_u = n_qkv + SSM_WIDTH
    qkv_ref[...] = jnp.dot(h, w_ref[:, :n_qkv], preferred_element_type=F32)
    u_ref[...] = jnp.dot(h, w_ref[:, n_qkv:n_u], preferred_element_type=F32)
    gates_ref[...] = jnp.dot(h, w_ref[:, n_u:], preferred_element_type=F32)


def _inproj(x2d, g_mix, w_in_bf, bsz, s_len):
    n = x2d.shape[0]
    tiles_per_seq = s_len // TOK_TILE
    in_cols = w_in_bf.shape[1]
    return pl.pallas_call(
        _inproj_kernel,
        grid=(n // TOK_TILE,),
        in_specs=[
            pl.BlockSpec((TOK_TILE, D_MODEL), lambda i: (i, 0)),
            pl.BlockSpec((1, D_MODEL), lambda i: (0, 0)),
            pl.BlockSpec((D_MODEL, in_cols), lambda i: (0, 0)),
        ],
        out_specs=[
            pl.BlockSpec((TOK_TILE, 3 * SB_WIDTH), lambda i: (i, 0)),
            pl.BlockSpec((TOK_TILE, SSM_WIDTH),
                         lambda i: (i % tiles_per_seq, i // tiles_per_seq)),
            pl.BlockSpec((TOK_TILE, 2 * D_MODEL), lambda i: (i, 0)),
        ],
        out_shape=[
            jax.ShapeDtypeStruct((n, 3 * SB_WIDTH), F32),
            jax.ShapeDtypeStruct((s_len, bsz * SSM_WIDTH), F32),
            jax.ShapeDtypeStruct((n, 2 * D_MODEL), F32),
        ],
        compiler_params=_cparams(1),
        name="inproj",
    )(x2d, g_mix, w_in_bf)


def _attn_kernel(q_ref, k_ref, v_ref, gq_ref, gk_ref, o_ref,
                 qs_ref, ks_ref, vs_ref, w2_ref, c_ref, acc_ref):
    s_len = q_ref.shape[1]
    n_blk = s_len // ATT_BLK
    n_pairs = q_ref.shape[2] // LANES
    lane = lax.broadcasted_iota(jnp.int32, (1, LANES), 1)
    head0 = lane < SB_HEAD_DIM

    def head_rms(t, g):
        sq = t * t
        s0 = jnp.sum(jnp.where(head0, sq, 0.0), axis=-1, keepdims=True)
        s1 = jnp.sum(jnp.where(head0, 0.0, sq), axis=-1, keepdims=True)
        ms = jnp.where(head0, s0, s1) * (1.0 / SB_HEAD_DIM)
        return t * lax.rsqrt(ms + EPS) * g

    scale = SB_HEAD_DIM ** -0.5

    def prep(qi, _):
        rows = pl.ds(pl.multiple_of(qi * ATT_BLK, ATT_BLK), ATT_BLK)
        for p in range(n_pairs):
            cols = slice(p * LANES, (p + 1) * LANES)
            qn = head_rms(q_ref[0, rows, cols], gq_ref[...]) * scale
            qs_ref[p, qi, :ATT_BLK, :] = jnp.where(head0, qn, 0.0).astype(BF16)
            qs_ref[p, qi, ATT_BLK:, :] = jnp.where(head0, 0.0, qn).astype(BF16)
            ks_ref[p, rows, :] = head_rms(k_ref[0, rows, cols], gk_ref[...]).astype(BF16)
            vs_ref[p, rows, :] = v_ref[0, rows, cols].astype(BF16)
        return 0

    lax.fori_loop(0, n_blk, prep, 0)

    r = lax.broadcasted_iota(jnp.int32, (2 * ATT_BLK, 2 * ATT_BLK), 0)
    c = lax.broadcasted_iota(jnp.int32, (2 * ATT_BLK, 2 * ATT_BLK), 1)
    r = jnp.where(r >= ATT_BLK, r - ATT_BLK, r)
    w2_ref[...] = jnp.where((c >= ATT_BLK) | (r > c), 1.0, 0.0).astype(BF16)

    ti = lax.broadcasted_iota(jnp.int32, (2 * ATT_BLK, ATT_BLK), 0)
    si = lax.broadcasted_iota(jnp.int32, (2 * ATT_BLK, ATT_BLK), 1)
    causal = si < jnp.where(ti >= ATT_BLK, ti - ATT_BLK, ti)

    def tiles(qi, kv_rows, diag):
        pairs = range(n_pairs)
        z = [lax.dot_general(qs_ref[p, qi], ks_ref[p, kv_rows, :], (((1,), (1,)), ((), ())),
                             preferred_element_type=F32) for p in pairs]
        log_beta, stacked = [], []
        for p in pairs:
            sp = jnp.maximum(z[p], 0.0) + jnp.log(1.0 + jnp.exp(-jnp.abs(z[p])))
            log_keep = -sp
            log_beta.append(z[p] - sp)
            if diag:
                log_keep = jnp.where(causal, log_keep, 0.0)
            hi = log_keep.astype(BF16)
            lo = (log_keep - hi.astype(F32)).astype(BF16)
            stacked.append(jnp.concatenate([hi, lo], axis=1))
        sums = [jnp.dot(stacked[p], w2_ref[...], preferred_element_type=F32) for p in pairs]
        w = []
        for p in pairs:
            wp = jnp.exp(log_beta[p] + sums[p][:, :ATT_BLK] + c_ref[p])
            if diag:
                wp = jnp.where(causal, wp, 0.0)
            w.append(wp.astype(BF16))
            c_ref[p] += sums[p][:, ATT_BLK:]
        pv = [jnp.dot(w[p], vs_ref[p, kv_rows, :], preferred_element_type=F32) for p in pairs]
        for p in pairs:
            acc_ref[p] += pv[p]

    def qblock(qi, _):
        q_rows = pl.ds(pl.multiple_of(qi * ATT_BLK, ATT_BLK), ATT_BLK)
        c_ref[...] = jnp.zeros_like(c_ref)
        acc_ref[...] = jnp.zeros_like(acc_ref)
        tiles(qi, q_rows, True)

        def live(carry):
            jj, c_max = carry
            return (jj < qi) & (c_max > DEAD_LOG_WEIGHT)

        def kvblock(carry):
            jj, _ = carry
            j = qi - 1 - jj
            kv_rows = pl.ds(pl.multiple_of(j * ATT_BLK, ATT_BLK), ATT_BLK)
            tiles(qi, kv_rows, False)
            return jj + 1, jnp.max(c_ref[...])

        lax.while_loop(live, kvblock, (jnp.int32(0), jnp.max(c_ref[...])))
        for p in range(n_pairs):
            o_ref[0, q_rows, p * LANES:(p + 1) * LANES] = jnp.where(
                head0, acc_ref[p, :ATT_BLK, :], acc_ref[p, ATT_BLK:, :])
        return 0

    lax.fori_loop(0, n_blk, qblock, 0)


def _attn(qkv3, gq2, gk2):
    bsz, s_len, _ = qkv3.shape
    n_pairs = ATT_HEADS_PER_STEP // 2
    n_steps = SB_HEADS // ATT_HEADS_PER_STEP
    blk = (1, s_len, n_pairs * LANES)
    return pl.pallas_call(
        _attn_kernel,
        grid=(bsz, n_steps),
        in_specs=[
            pl.BlockSpec(blk, lambda b, p: (b, 0, p)),
            pl.BlockSpec(blk, lambda b, p: (b, 0, n_steps + p)),
            pl.BlockSpec(blk, lambda b, p: (b, 0, 2 * n_steps + p)),
            pl.BlockSpec((1, LANES), lambda b, p: (0, 0)),
            pl.BlockSpec((1, LANES), lambda b, p: (0, 0)),
        ],
        out_specs=pl.BlockSpec(blk, lambda b, p: (b, 0, p)),
        out_shape=jax.ShapeDtypeStruct((bsz, s_len, SB_WIDTH), F32),
        scratch_shapes=[
            pltpu.VMEM((n_pairs, s_len // ATT_BLK, 2 * ATT_BLK, LANES), BF16),
            pltpu.VMEM((n_pairs, s_len, LANES), BF16),
            pltpu.VMEM((n_pairs, s_len, LANES), BF16),
            pltpu.VMEM((2 * ATT_BLK, 2 * ATT_BLK), BF16),
            pltpu.VMEM((n_pairs, 2 * ATT_BLK, ATT_BLK), F32),
            pltpu.VMEM((n_pairs, 2 * ATT_BLK, LANES), F32),
        ],
        compiler_params=_cparams(2),
        name="attn",
    )(qkv3, qkv3, qkv3, gq2, gk2)


def _s5_kernel(u_ref, a_ref, b_ref, c_ref, d_ref, y_ref, hbuf_ref, state_ref):
    tc, bsz, _ = u_ref.shape
    rows = tc * bsz

    @pl.when(pl.program_id(0) == 0)
    def _():
        state_ref[...] = jnp.zeros_like(state_ref)

    u2 = u_ref[...].reshape(rows, SSM_WIDTH)
    ub = u2.astype(BF16)
    for hf in range(2):
        uh = ub[:, hf * HALF_U:(hf + 1) * HALF_U]
        xh = jnp.dot(uh, b_ref[hf], preferred_element_type=F32)
        hbuf_ref[...] = xh.reshape(tc, bsz, 2 * HALF_STATE)
        for lc in range(HALF_STATE // SCAN_LANES):
            re = pl.ds(lc * SCAN_LANES, SCAN_LANES)
            im = pl.ds(HALF_STATE + lc * SCAN_LANES, SCAN_LANES)
            ar = a_ref[hf, :, re]
            ai = a_ref[hf, :, im]

            def step(t, carry, re=re, im=im, ar=ar, ai=ai):
                hr, hi = carry
                nr = ar * hr - ai * hi + hbuf_ref[t, :, re]
                ni = ar * hi + ai * hr + hbuf_ref[t, :, im]
                hbuf_ref[t, :, re] = nr
                hbuf_ref[t, :, im] = ni
                return nr, ni

            hr, hi = lax.fori_loop(0, tc, step,
                                   (state_ref[hf, :, re], state_ref[hf, :, im]),
                                   unroll=8)
            state_ref[hf, :, re] = hr
            state_ref[hf, :, im] = hi
        hb = hbuf_ref[...].reshape(rows, 2 * HALF_STATE).astype(BF16)
        yh = jnp.dot(hb, c_ref[hf], preferred_element_type=F32)
        cols = slice(hf * HALF_U, (hf + 1) * HALF_U)
        yh = yh + d_ref[:, cols] * u2[:, cols]
        y_ref[:, :, cols] = yh.reshape(tc, bsz, HALF_U)


def _s5(u_tm3, a_bc, b_cat, c_cat, d_skip):
    s_len, bsz, _ = u_tm3.shape
    return pl.pallas_call(
        _s5_kernel,
        grid=(s_len // SCAN_CHUNK,),
        in_specs=[
            pl.BlockSpec((SCAN_CHUNK, bsz, SSM_WIDTH), lambda c: (c, 0, 0)),
            pl.BlockSpec((2, bsz, 2 * HALF_STATE), lambda c: (0, 0, 0)),
            pl.BlockSpec((2, HALF_U, 2 * HALF_STATE), lambda c: (0, 0, 0)),
            pl.BlockSpec((2, 2 * HALF_STATE, HALF_U), lambda c: (0, 0, 0)),
            pl.BlockSpec((1, SSM_WIDTH), lambda c: (0, 0)),
        ],
        out_specs=pl.BlockSpec((SCAN_CHUNK, bsz, SSM_WIDTH), lambda c: (c, 0, 0)),
        out_shape=jax.ShapeDtypeStruct((s_len, bsz, SSM_WIDTH), F32),
        scratch_shapes=[
            pltpu.VMEM((SCAN_CHUNK, bsz, 2 * HALF_STATE), F32),
            pltpu.VMEM((2, bsz, 2 * HALF_STATE), F32),
        ],
        compiler_params=_cparams(1),
        name="s5",
    )(u_tm3, a_bc, b_cat, c_cat, d_skip)


def _s5_params(a_re, a_im, log_dt, b_re, b_im, c_re, c_im, bsz):
    dt = jnp.exp(log_dt)[:, None]
    mag = jnp.exp(a_re * dt)
    abar_r = mag * jnp.cos(a_im * dt)
    abar_i = mag * jnp.sin(a_im * dt)
    den = a_re * a_re + a_im * a_im
    nr = abar_r - 1.0
    ni = abar_i
    fr = (nr * a_re + ni * a_im) / den
    fi = (ni * a_re - nr * a_im) / den
    bbar_r = fr[..., None] * b_re - fi[..., None] * b_im
    bbar_i = fr[..., None] * b_im + fi[..., None] * b_re
    gh = SSM_GROUPS // 2
    eye = jnp.eye(gh, dtype=F32)

    def a_half(hf):
        sl = slice(hf * gh, (hf + 1) * gh)
        row = jnp.concatenate([abar_r[sl].reshape(-1), abar_i[sl].reshape(-1)])
        return jnp.broadcast_to(row[None], (bsz, 2 * HALF_STATE))

    def b_half(bb, hf):
        blk = bb[hf * gh:(hf + 1) * gh]
        return jnp.einsum('gpc,gk->gckp', blk, eye).reshape(HALF_U, HALF_STATE)

    def c_half(cc, hf):
        blk = cc[hf * gh:(hf + 1) * gh]
        return jnp.einsum('gcp,gk->gpkc', blk, eye).reshape(HALF_STATE, HALF_U)

    a_bc = jnp.stack([a_half(0), a_half(1)])
    b_cat = jnp.stack([jnp.concatenate([b_half(bbar_r, hf), b_half(bbar_i, hf)], axis=1)
                       for hf in range(2)]).astype(BF16)
    c_cat = jnp.stack([jnp.concatenate([c_half(c_re, hf), -c_half(c_im, hf)], axis=0)
                       for hf in range(2)]).astype(BF16)
    return a_bc, b_cat, c_cat


def _merge_kernel(x_ref, o_ref, y_ref, gates_ref, wab_ref, wglu_ref, bglu_ref, wout_ref,
                  gffn_ref, wrh_ref, wrl_ref, br_ref,
                  x1_ref, xn_ref, idx_ref, gate_ref, rank_ref, cnt_ref, carry_ref):
    @pl.when(pl.program_id(0) == 0)
    def _():
        carry_ref[...] = jnp.zeros_like(carry_ref)

    attn_branch = jnp.dot(o_ref[...].astype(BF16), wab_ref[...], preferred_element_type=F32)
    zg = jnp.dot(jax.nn.gelu(y_ref[...]).astype(BF16), wglu_ref[...],
                 preferred_element_type=F32) + bglu_ref[...]
    ssm_branch = zg[:, :D_MODEL] * jax.nn.sigmoid(zg[:, D_MODEL:])
    mixed = (jax.nn.sigmoid(gates_ref[:, :D_MODEL]) * attn_branch
             + jax.nn.sigmoid(gates_ref[:, D_MODEL:]) * ssm_branch)
    x1 = x_ref[...] + jnp.dot(mixed.astype(BF16), wout_ref[...], preferred_element_type=F32)
    x1_ref[...] = x1
    xn = _rms(x1, gffn_ref[...])
    xn_ref[...] = xn

    xh = xn.astype(BF16)
    xl = (xn - xh.astype(F32)).astype(BF16)
    logits = (jnp.dot(xh, wrh_ref[...], preferred_element_type=F32)
              + jnp.dot(xl, wrh_ref[...], preferred_element_type=F32)
              + jnp.dot(xh, wrl_ref[...], preferred_element_type=F32)) + br_ref[...]

    tm = logits.shape[0]
    e_iota = lax.broadcasted_iota(jnp.int32, (tm, N_EXPERTS), 1).astype(F32)
    k_iota = lax.broadcasted_iota(jnp.int32, (tm, TOP_K), 1)
    rr = lax.broadcasted_iota(jnp.int32, (tm, tm), 0)
    cc = lax.broadcasted_iota(jnp.int32, (tm, tm), 1)
    tri = jnp.where(cc < rr, 1.0, 0.0).astype(BF16)

    work = logits
    picks, vals = [], []
    sel = jnp.zeros((tm, N_EXPERTS), F32)
    for _ in range(TOP_K):
        m = jnp.max(work, axis=-1, keepdims=True)
        pick = jnp.min(jnp.where(work == m, e_iota, float(N_EXPERTS)), axis=-1, keepdims=True)
        hit = e_iota == pick
        work = jnp.where(hit, -jnp.inf, work)
        sel = sel + jnp.where(hit, 1.0, 0.0)
        picks.append(pick)
        vals.append(m)
    before = jnp.dot(tri, sel.astype(BF16), preferred_element_type=F32) + carry_ref[...]
    exps = [jnp.exp(v - vals[0]) for v in vals]
    denom = exps[0] + exps[1] + exps[2] + exps[3]
    idx = jnp.zeros((tm, TOP_K), jnp.int32)
    gate = jnp.zeros((tm, TOP_K), F32)
    rank = jnp.zeros((tm, TOP_K), jnp.int32)
    for k in range(TOP_K):
        rk = jnp.sum(jnp.where(e_iota == picks[k], before, 0.0), axis=-1, keepdims=True)
        idx = jnp.where(k_iota == k, picks[k].astype(jnp.int32), idx)
        gate = jnp.where(k_iota == k, exps[k] / denom, gate)
        rank = jnp.where(k_iota == k, rk.astype(jnp.int32), rank)
    idx_ref[...] = idx
    gate_ref[...] = gate
    rank_ref[...] = rank
    carry_ref[...] += jnp.sum(sel, axis=0, keepdims=True)
    cnt_ref[...] = carry_ref[...]


def _merge(x2d, o2d, y_tm2, gates, wab, wglu, bglu, wout, gffn, wrh, wrl, br, s_len):
    n = x2d.shape[0]
    tiles_per_seq = s_len // TOK_TILE
    full = lambda shape: pl.BlockSpec(shape, lambda i: (0,) * len(shape))
    row = lambda w: pl.BlockSpec((TOK_TILE, w), lambda i: (i, 0))
    return pl.pallas_call(
        _merge_kernel,
        grid=(n // TOK_TILE,),
        in_specs=[
            row(D_MODEL), row(SB_WIDTH),
            pl.BlockSpec((TOK_TILE, SSM_WIDTH),
                         lambda i: (i % tiles_per_seq, i // tiles_per_seq)),
            row(2 * D_MODEL),
            full((SB_WIDTH, D_MODEL)), full((SSM_WIDTH, 2 * D_MODEL)), full((1, 2 * D_MODEL)),
            full((D_MODEL, D_MODEL)), full((1, D_MODEL)),
            full((D_MODEL, N_EXPERTS)), full((D_MODEL, N_EXPERTS)), full((1, N_EXPERTS)),
        ],
        out_specs=[row(D_MODEL), row(D_MODEL), row(TOP_K), row(TOP_K), row(TOP_K),
                   full((1, N_EXPERTS))],
        out_shape=[
            jax.ShapeDtypeStruct((n, D_MODEL), F32),
            jax.ShapeDtypeStruct((n, D_MODEL), F32),
            jax.ShapeDtypeStruct((n, TOP_K), jnp.int32),
            jax.ShapeDtypeStruct((n, TOP_K), F32),
            jax.ShapeDtypeStruct((n, TOP_K), jnp.int32),
            jax.ShapeDtypeStruct((1, N_EXPERTS), F32),
        ],
        scratch_shapes=[pltpu.VMEM((1, N_EXPERTS), F32)],
        compiler_params=_cparams(1),
        name="merge",
    )(x2d, o2d, y_tm2, gates, wab, wglu, bglu, wout, gffn, wrh, wrl, br)


def _slot_row(ref, slot):
    return ref.at[slot >> (SUBLANES.bit_length() - 1), pl.ds(slot & (SUBLANES - 1), 1)]


def _dispatch_kernel(slot_ref, xn_ref, xs_ref, sem):
    def issue(g, _):
        for k in range(TOP_K):
            for j in range(SUBLANES):
                slot = slot_ref[k * TOK_TILE + g * SUBLANES + j]
                pltpu.make_async_copy(xn_ref.at[g, pl.ds(j, 1)], _slot_row(xs_ref, slot),
                                      sem).start(priority=j % 2)
        return 0

    lax.fori_loop(0, TILE_GROUPS, issue, 0)
    for _ in range(TOP_K):
        pltpu.make_async_copy(xn_ref, xs_ref.at[pl.ds(0, TILE_GROUPS)], sem).wait()


def _dispatch(slots_flat, xn3):
    n = xn3.shape[0] * SUBLANES
    return pl.pallas_call(
        _dispatch_kernel,
        grid=(n // TOK_TILE,),
        in_specs=[
            pl.BlockSpec((TOP_K * TOK_TILE,), lambda i: (i,), memory_space=pltpu.SMEM),
            pl.BlockSpec((TILE_GROUPS, SUBLANES, D_MODEL), lambda i: (i, 0, 0)),
        ],
        out_specs=pl.BlockSpec(memory_space=pl.ANY),
        out_shape=jax.ShapeDtypeStruct((n * TOP_K // SUBLANES, SUBLANES, D_MODEL), F32),
        scratch_shapes=[pltpu.SemaphoreType.DMA],
        compiler_params=_cparams(1),
        name="dispatch",
    )(slots_flat, xn3)


FLAG_VALID, FLAG_FIRST_VISIT, FLAG_NEW_EXPERT = 1, 2, 4


def _experts_kernel(blk_ref, exp_ref, flag_ref, start_ref,
                    xs_ref, wup_ref, bup_ref, wdn_ref, bdn_ref, ys_ref,
                    wup_bf_ref, wdn_bf_ref):
    i = pl.program_id(0)
    flags = flag_ref[i]

    @pl.when((flags & FLAG_NEW_EXPERT) != 0)
    def _():
        wup_bf_ref[...] = wup_ref[0].astype(BF16)
        wdn_bf_ref[...] = wdn_ref[0].astype(BF16)

    @pl.when((flags & FLAG_VALID) != 0)
    def _():
        e = exp_ref[i]
        gu = jnp.dot(xs_ref[...].astype(BF16), wup_bf_ref[...],
                     preferred_element_type=F32) + bup_ref[0]
        g = jnp.minimum(gu[:, :D_FF], SWIGLU_LIMIT)
        up = jnp.clip(gu[:, D_FF:], -SWIGLU_LIMIT, SWIGLU_LIMIT)
        act = (up + 1.0) * g * jax.nn.sigmoid(SWIGLU_ALPHA * g)
        y = jnp.dot(act.astype(BF16), wdn_bf_ref[...],
                    preferred_element_type=F32) + bdn_ref[0]
        row = blk_ref[i] * ROW_BLK + lax.broadcasted_iota(jnp.int32, (ROW_BLK, 1), 0)
        mine = (row >= start_ref[e]) & (row < start_ref[e + 1])

        @pl.when((flags & FLAG_FIRST_VISIT) != 0)
        def _():
            ys_ref[...] = jnp.where(mine, y, 0.0)

        @pl.when((flags & FLAG_FIRST_VISIT) == 0)
        def _():
            ys_ref[...] = jnp.where(mine, y, ys_ref[...])


def _experts(item_blk, item_exp, item_flag, starts, xs, w_up, b_up3, w_down, b_down3):
    rows = xs.shape[0]
    n_items = item_blk.shape[0]
    grid_spec = pltpu.PrefetchScalarGridSpec(
        num_scalar_prefetch=4,
        grid=(n_items,),
        in_specs=[
            pl.BlockSpec((ROW_BLK, D_MODEL), lambda i, b, e, f, s: (b[i], 0)),
            pl.BlockSpec((1, D_MODEL, 2 * D_FF), lambda i, b, e, f, s: (e[i], 0, 0)),
            pl.BlockSpec((1, 1, 2 * D_FF), lambda i, b, e, f, s: (e[i], 0, 0)),
            pl.BlockSpec((1, D_FF, D_MODEL), lambda i, b, e, f, s: (e[i], 0, 0)),
            pl.BlockSpec((1, 1, D_MODEL), lambda i, b, e, f, s: (e[i], 0, 0)),
        ],
        out_specs=pl.BlockSpec((ROW_BLK, D_MODEL), lambda i, b, e, f, s: (b[i], 0)),
        scratch_shapes=[
            pltpu.VMEM((D_MODEL, 2 * D_FF), BF16),
            pltpu.VMEM((D_FF, D_MODEL), BF16),
        ],
    )
    return pl.pallas_call(
        _experts_kernel,
        grid_spec=grid_spec,
        out_shape=jax.ShapeDtypeStruct((rows, D_MODEL), F32),
        compiler_params=_cparams(1),
        name="experts",
    )(item_blk, item_exp, item_flag, starts, xs, w_up, b_up3, w_down, b_down3)


def _build_items(counts, n_rows):
    ends = jnp.cumsum(counts)
    starts = ends - counts
    n_blocks = n_rows // ROW_BLK
    n_items = n_blocks + N_ITEMS_EXTRA
    lo = jnp.arange(n_blocks, dtype=jnp.int32)[:, None] * ROW_BLK
    overlap = jnp.minimum(ends[None, :], lo + ROW_BLK) - jnp.maximum(starts[None, :], lo)
    live = (overlap > 0).reshape(-1)
    n_live = jnp.sum(live.astype(jnp.int32))
    (pos,) = jnp.nonzero(live, size=n_items, fill_value=0)
    pos = pos.astype(jnp.int32)
    k = jnp.arange(n_items, dtype=jnp.int32)
    valid = k < n_live
    pos = jnp.where(valid, pos, pos[jnp.maximum(n_live - 1, 0)])
    blk = pos // N_EXPERTS
    exp = pos % N_EXPERTS
    prev_blk = jnp.concatenate([jnp.full((1,), -1, jnp.int32), blk[:-1]])
    prev_exp = jnp.concatenate([jnp.full((1,), -1, jnp.int32), exp[:-1]])
    flags = (jnp.where(valid, FLAG_VALID, 0)
             | jnp.where(valid & (blk != prev_blk), FLAG_FIRST_VISIT, 0)
             | jnp.where(valid & (exp != prev_exp), FLAG_NEW_EXPERT, 0)).astype(jnp.int32)
    starts33 = jnp.concatenate([starts, ends[-1:]]).astype(jnp.int32)
    return blk, exp, flags, starts33


def _combine_kernel(slot_ref, ys_ref, x1_ref, gate_ref, p_ref, gpg_ref, wpg_ref, wpp_ref,
                    gpp_ref, out_ref, rows_ref, sem):
    def issue(g, _):
        for k in range(TOP_K):
            for j in range(SUBLANES):
                slot = slot_ref[k * TOK_TILE + g * SUBLANES + j]
                pltpu.make_async_copy(_slot_row(ys_ref, slot), rows_ref.at[k, g, pl.ds(j, 1)],
                                      sem).start(priority=j % 2)
        return 0

    lax.fori_loop(0, TILE_GROUPS, issue, 0)
    ple = _rms(jnp.dot(p_ref[...].astype(BF16), wpp_ref[...], preferred_element_type=F32),
               gpp_ref[...])
    for k in range(TOP_K):
        pltpu.make_async_copy(ys_ref.at[pl.ds(0, TILE_GROUPS)], rows_ref.at[k], sem).wait()
    x2 = x1_ref[...]
    gate = gate_ref[...]
    for k in range(TOP_K):
        x2 = x2 + gate[:, k:k + 1] * rows_ref[k].reshape(TOK_TILE, D_MODEL)
    pg = jax.nn.sigmoid(jnp.dot(_rms(x2, gpg_ref[...]).astype(BF16), wpg_ref[...],
                                preferred_element_type=F32))
    out_ref[...] = x2 + pg * ple


def _combine(slots_flat, ys, x1, gate, p2d, gpg, wpg, wpp, gpp):
    n = x1.shape[0]
    full = lambda shape: pl.BlockSpec(shape, lambda i: (0,) * len(shape))
    row = lambda w: pl.BlockSpec((TOK_TILE, w), lambda i: (i, 0))
    return pl.pallas_call(
        _combine_kernel,
        grid=(n // TOK_TILE,),
        in_specs=[
            pl.BlockSpec((TOP_K * TOK_TILE,), lambda i: (i,), memory_space=pltpu.SMEM),
            pl.BlockSpec(memory_space=pl.ANY),
            row(D_MODEL), row(TOP_K), row(PLE_DIM),
            full((1, D_MODEL)), full((D_MODEL, D_MODEL)), full((PLE_DIM, D_MODEL)),
            full((1, D_MODEL)),
        ],
        out_specs=row(D_MODEL),
        out_shape=jax.ShapeDtypeStruct((n, D_MODEL), F32),
        scratch_shapes=[pltpu.VMEM((TOP_K, TILE_GROUPS, SUBLANES, D_MODEL), F32),
                        pltpu.SemaphoreType.DMA],
        compiler_params=_cparams(1),
        name="combine",
    )(slots_flat, ys, x1, gate, p2d, gpg, wpg, wpp, gpp)


def kernel(x, p, g_mix, w_in, g_q, g_k, w_attn_branch, a_re, a_im, log_dt, b_re, b_im, c_re, c_im, d_skip, w_glu, b_glu, w_out, g_ffn, w_router, b_router, w_up, b_up, w_down, b_down, g_ple_gate, w_ple_gate, w_ple_proj, g_ple_post):
    bsz, s_len, d = x.shape
    depth = w_in.shape[0]
    n = bsz * s_len
    assert d == D_MODEL and s_len % TOK_TILE == 0 and s_len % SCAN_CHUNK == 0
    assert bsz == SUBLANES, "the S5 scan keeps the batch on the sublane axis"

    x2d = x.reshape(n, d)
    for i in range(depth):
        qkv, u_tm, gates = _inproj(x2d, g_mix[i][None], w_in[i].astype(BF16), bsz, s_len)
        o = _attn(qkv.reshape(bsz, s_len, 3 * SB_WIDTH),
                  jnp.tile(g_q[i], 2)[None], jnp.tile(g_k[i], 2)[None])
        a_bc, b_cat, c_cat = _s5_params(a_re[i], a_im[i], log_dt[i], b_re[i], b_im[i],
                                        c_re[i], c_im[i], bsz)
        y_tm = _s5(u_tm.reshape(s_len, bsz, SSM_WIDTH), a_bc, b_cat, c_cat, d_skip[i][None])

        wr = w_router[i]
        wr_hi = wr.astype(BF16)
        wr_lo = (wr - wr_hi.astype(F32)).astype(BF16)
        x1, xn, idx, gate, rank, counts = _merge(
            x2d, o.reshape(n, SB_WIDTH), y_tm.reshape(s_len, bsz * SSM_WIDTH), gates,
            w_attn_branch[i].astype(BF16), w_glu[i].astype(BF16), b_glu[i][None],
            w_out[i].astype(BF16), g_ffn[i][None], wr_hi, wr_lo, b_router[i][None], s_len)

        counts_i = counts[0].astype(jnp.int32)
        blk, exp, flags, starts33 = _build_items(counts_i, n * TOP_K)
        slots = starts33[idx] + rank
        slots_flat = slots.reshape(n // TOK_TILE, TOK_TILE, TOP_K).transpose(0, 2, 1).reshape(-1)

        xs3 = _dispatch(slots_flat, xn.reshape(n // SUBLANES, SUBLANES, D_MODEL))
        ys = _experts(blk, exp, flags, starts33, xs3.reshape(n * TOP_K, D_MODEL),
                      w_up[i], b_up[i][:, None, :], w_down[i], b_down[i][:, None, :])
        ys3 = ys.reshape(n * TOP_K // SUBLANES, SUBLANES, D_MODEL)
        x2d = _combine(slots_flat, ys3, x1, gate, p[i].reshape(n, PLE_DIM),
                       g_ple_gate[i][None], w_ple_gate[i].astype(BF16),
                       w_ple_proj[i].astype(BF16), g_ple_post[i][None])
    return x2d.reshape(bsz, s_len, d)
```

```python
import jax
import jax.numpy as jnp
from jax import lax
from jax.experimental import pallas as pl
from jax.experimental.pallas import tpu as pltpu

F32 = jnp.float32
BF16 = jnp.bfloat16

D_MODEL = 1024
SB_HEADS = 8
SB_HEAD_DIM = 64
SB_WIDTH = SB_HEADS * SB_HEAD_DIM
SSM_GROUP = 16
SSM_WIDTH = 512
SSM_GROUPS = SSM_WIDTH // SSM_GROUP
SSM_STATE = 64
PLE_DIM = 256
N_EXPERTS = 32
TOP_K = 4
D_FF = D_MODEL
SWIGLU_LIMIT = 7.0
SWIGLU_ALPHA = 1.702
EPS = 1e-6

LANES = 128
SUBLANES = 8
VMEM_LIMIT = 56 * 1024 * 1024

TOK_TILE = 256
TILE_GROUPS = TOK_TILE // SUBLANES
MERGE_SUBTILES = 2
ATT_BLK = 128
DEAD_LOG_WEIGHT = -104.0
ATT_HEADS_PER_STEP = 8
SCAN_CHUNK = 64
SCAN_LANES = 512
HALF_U = SSM_WIDTH // 2
HALF_STATE = SSM_GROUPS // 2 * SSM_STATE
ROW_BLK = 256
N_ITEMS_EXTRA = N_EXPERTS - 1


def _cparams(n_axes):
    return pltpu.CompilerParams(
        dimension_semantics=("arbitrary",) * n_axes,
        vmem_limit_bytes=VMEM_LIMIT)


def _rms(x, g):
    ms = jnp.mean(x * x, axis=-1, keepdims=True)
    return x * lax.rsqrt(ms + EPS) * g


def _inproj_kernel(x_ref, g_ref, w_ref, qkv_ref, u_ref, gates_ref):
    h = _rms(x_ref[...], g_ref[...]).astype(BF16)
    n_qkv = 3 * SB_WIDTH
    n_u = n_qkv + SSM_WIDTH
    qkv_ref[...] = jnp.dot(h, w_ref[:, :n_qkv], preferred_element_type=F32)
    u_ref[...] = jnp.dot(h, w_ref[:, n_qkv:n_u], preferred_element_type=F32)
    gates_ref[...] = jnp.dot(h, w_ref[:, n_u:], preferred_element_type=F32)


def _inproj(x2d, g_mix, w_in_bf, bsz, s_len):
    n = x2d.shape[0]
    tiles_per_seq = s_len // TOK_TILE
    in_cols = w_in_bf.shape[1]
    return pl.pallas_call(
        _inproj_kernel,
        grid=(n // TOK_TILE,),
        in_specs=[
            pl.BlockSpec((TOK_TILE, D_MODEL), lambda i: (i, 0)),
            pl.BlockSpec((1, D_MODEL), lambda i: (0, 0)),
            pl.BlockSpec((D_MODEL, in_cols), lambda i: (0, 0)),
        ],
        out_specs=[
            pl.BlockSpec((TOK_TILE, 3 * SB_WIDTH), lambda i: (i, 0)),
            pl.BlockSpec((TOK_TILE, SSM_WIDTH),
                         lambda i: (i % tiles_per_seq, i // tiles_per_seq)),
            pl.BlockSpec((TOK_TILE, 2 * D_MODEL), lambda i: (i, 0)),
        ],
        out_shape=[
            jax.ShapeDtypeStruct((n, 3 * SB_WIDTH), F32),
            jax.ShapeDtypeStruct((s_len, bsz * SSM_WIDTH), F32),
            jax.ShapeDtypeStruct((n, 2 * D_MODEL), F32),
        ],
        compiler_params=_cparams(1),
        name="inproj",
    )(x2d, g_mix, w_in_bf)


def _attn_kernel(q_ref, k_ref, v_ref, gq_ref, gk_ref, o_ref,
                 qs_ref, ks_ref, vs_ref, w2_ref, c_ref, acc_ref):
    s_len = q_ref.shape[1]
    n_blk = s_len // ATT_BLK
    n_pairs = q_ref.shape[2] // LANES
    lane = lax.broadcasted_iota(jnp.int32, (1, LANES), 1)
    head0 = lane < SB_HEAD_DIM

    def head_rms(t, g):
        sq = t * t
        s0 = jnp.sum(jnp.where(head0, sq, 0.0), axis=-1, keepdims=True)
        s1 = jnp.sum(jnp.where(head0, 0.0, sq), axis=-1, keepdims=True)
        ms = jnp.where(head0, s0, s1) * (1.0 / SB_HEAD_DIM)
        return t * lax.rsqrt(ms + EPS) * g

    scale = SB_HEAD_DIM ** -0.5

    def prep(qi, _):
        rows = pl.ds(pl.multiple_of(qi * ATT_BLK, ATT_BLK), ATT_BLK)
        for p in range(n_pairs):
            cols = slice(p * LANES, (p + 1) * LANES)
            qn = head_rms(q_ref[0, rows, cols], gq_ref[...]) * scale
            qs_ref[p, qi, :ATT_BLK, :] = jnp.where(head0, qn, 0.0).astype(BF16)
            qs_ref[p, qi, ATT_BLK:, :] = jnp.where(head0, 0.0, qn).astype(BF16)
            ks_ref[p, rows, :] = head_rms(k_ref[0, rows, cols], gk_ref[...]).astype(BF16)
            vs_ref[p, rows, :] = v_ref[0, rows, cols].astype(BF16)
        return 0

    lax.fori_loop(0, n_blk, prep, 0)

    r = lax.broadcasted_iota(jnp.int32, (2 * ATT_BLK, 2 * ATT_BLK), 0)
    c = lax.broadcasted_iota(jnp.int32, (2 * ATT_BLK, 2 * ATT_BLK), 1)
    r = jnp.where(r >= ATT_BLK, r - ATT_BLK, r)
    w2_ref[...] = jnp.where((c >= ATT_BLK) | (r > c), 1.0, 0.0).astype(BF16)

    ti = lax.broadcasted_iota(jnp.int32, (2 * ATT_BLK, ATT_BLK), 0)
    si = lax.broadcasted_iota(jnp.int32, (2 * ATT_BLK, ATT_BLK), 1)
    causal = si < jnp.where(ti >= ATT_BLK, ti - ATT_BLK, ti)

    def tiles(qi, kv_rows, diag):
        pairs = range(n_pairs)
        z = [lax.dot_general(qs_ref[p, qi], ks_ref[p, kv_rows, :], (((1,), (1,)), ((), ())),
                             preferred_element_type=F32) for p in pairs]
        log_beta, stacked = [], []
        for p in pairs:
            sp = jnp.maximum(z[p], 0.0) + jnp.log(1.0 + jnp.exp(-jnp.abs(z[p])))
            log_keep = -sp
            log_beta.append(z[p] - sp)
            if diag:
                log_keep = jnp.where(causal, log_keep, 0.0)
            hi = log_keep.astype(BF16)
            lo = (log_keep - hi.astype(F32)).astype(BF16)
            stacked.append(jnp.concatenate([hi, lo], axis=1))
        sums = [jnp.dot(stacked[p], w2_ref[...], preferred_element_type=F32) for p in pairs]
        w = []
        for p in pairs:
            wp = jnp.exp(log_beta[p] + sums[p][:, :ATT_BLK] + c_ref[p])
            if diag:
                wp = jnp.where(causal, wp, 0.0)
            w.append(wp.astype(BF16))
            c_ref[p] += sums[p][:, ATT_BLK:]
        pv = [jnp.dot(w[p], vs_ref[p, kv_rows, :], preferred_element_type=F32) for p in pairs]
        for p in pairs:
            acc_ref[p] += pv[p]

    def qblock(qi, _):
        q_rows = pl.ds(pl.multiple_of(qi * ATT_BLK, ATT_BLK), ATT_BLK)
        c_ref[...] = jnp.zeros_like(c_ref)
        acc_ref[...] = jnp.zeros_like(acc_ref)
        tiles(qi, q_rows, True)

        def live(carry):
            jj, c_max = carry
            return (jj < qi) & (c_max > DEAD_LOG_WEIGHT)

        def kvblock(carry):
            jj, _ = carry
            j = qi - 1 - jj
            kv_rows = pl.ds(pl.multiple_of(j * ATT_BLK, ATT_BLK), ATT_BLK)
            tiles(qi, kv_rows, False)
            return jj + 1, jnp.max(c_ref[...])

        lax.while_loop(live, kvblock, (jnp.int32(0), jnp.max(c_ref[...])))
        for p in range(n_pairs):
            o_ref[0, q_rows, p * LANES:(p + 1) * LANES] = jnp.where(
                head0, acc_ref[p, :ATT_BLK, :], acc_ref[p, ATT_BLK:, :])
        return 0

    lax.fori_loop(0, n_blk, qblock, 0)


def _attn(qkv3, gq2, gk2):
    bsz, s_len, _ = qkv3.shape
    n_pairs = ATT_HEADS_PER_STEP // 2
    n_steps = SB_HEADS // ATT_HEADS_PER_STEP
    blk = (1, s_len, n_pairs * LANES)
    return pl.pallas_call(
        _attn_kernel,
        grid=(bsz, n_steps),
        in_specs=[
            pl.BlockSpec(blk, lambda b, p: (b, 0, p)),
            pl.BlockSpec(blk, lambda b, p: (b, 0, n_steps + p)),
            pl.BlockSpec(blk, lambda b, p: (b, 0, 2 * n_steps + p)),
            pl.BlockSpec((1, LANES), lambda b, p: (0, 0)),
            pl.BlockSpec((1, LANES), lambda b, p: (0, 0)),
        ],
        out_specs=pl.BlockSpec(blk, lambda b, p: (b, 0, p)),
        out_shape=jax.ShapeDtypeStruct((bsz, s_len, SB_WIDTH), F32),
        scratch_shapes=[
            pltpu.VMEM((n_pairs, s_len // ATT_BLK, 2 * ATT_BLK, LANES), BF16),
            pltpu.VMEM((n_pairs, s_len, LANES), BF16),
            pltpu.VMEM((n_pairs, s_len, LANES), BF16),
            pltpu.VMEM((2 * ATT_BLK, 2 * ATT_BLK), BF16),
            pltpu.VMEM((n_pairs, 2 * ATT_BLK, ATT_BLK), F32),
            pltpu.VMEM((n_pairs, 2 * ATT_BLK, LANES), F32),
        ],
        compiler_params=_cparams(2),
        name="attn",
    )(qkv3, qkv3, qkv3, gq2, gk2)


def _s5_kernel(u_ref, a_ref, b_ref, c_ref, d_ref, y_ref, hbuf_ref, state_ref):
    tc, bsz, _ = u_ref.shape
    rows = tc * bsz

    @pl.when(pl.program_id(0) == 0)
    def _():
        state_ref[...] = jnp.zeros_like(state_ref)

    u2 = u_ref[...].reshape(rows, SSM_WIDTH)
    ub = u2.astype(BF16)
    for hf in range(2):
        uh = ub[:, hf * HALF_U:(hf + 1) * HALF_U]
        xh = jnp.dot(uh, b_ref[hf], preferred_element_type=F32)
        hbuf_ref[...] = xh.reshape(tc, bsz, 2 * HALF_STATE)
        for lc in range(HALF_STATE // SCAN_LANES):
            re = pl.ds(lc * SCAN_LANES, SCAN_LANES)
            im = pl.ds(HALF_STATE + lc * SCAN_LANES, SCAN_LANES)
            ar = a_ref[hf, :, re]
            ai = a_ref[hf, :, im]

            def step(t, carry, re=re, im=im, ar=ar, ai=ai):
                hr, hi = carry
                nr = ar * hr - ai * hi + hbuf_ref[t, :, re]
                ni = ar * hi + ai * hr + hbuf_ref[t, :, im]
                hbuf_ref[t, :, re] = nr
                hbuf_ref[t, :, im] = ni
                return nr, ni

            hr, hi = lax.fori_loop(0, tc, step,
                                   (state_ref[hf, :, re], state_ref[hf, :, im]),
                                   unroll=8)
            state_ref[hf, :, re] = hr
            state_ref[hf, :, im] = hi
        hb = hbuf_ref[...].reshape(rows, 2 * HALF_STATE).astype(BF16)
        yh = jnp.dot(hb, c_ref[hf], preferred_element_type=F32)
        cols = slice(hf * HALF_U, (hf + 1) * HALF_U)
        yh = yh + d_ref[:, cols] * u2[:, cols]
        y_ref[:, :, cols] = yh.reshape(tc, bsz, HALF_U)


def _s5(u_tm3, a_bc, b_cat, c_cat, d_skip):
    s_len, bsz, _ = u_tm3.shape
    return pl.pallas_call(
        _s5_kernel,
        grid=(s_len // SCAN_CHUNK,),
        in_specs=[
            pl.BlockSpec((SCAN_CHUNK, bsz, SSM_WIDTH), lambda c: (c, 0, 0)),
            pl.BlockSpec((2, bsz, 2 * HALF_STATE), lambda c: (0, 0, 0)),
            pl.BlockSpec((2, HALF_U, 2 * HALF_STATE), lambda c: (0, 0, 0)),
            pl.BlockSpec((2, 2 * HALF_STATE, HALF_U), lambda c: (0, 0, 0)),
            pl.BlockSpec((1, SSM_WIDTH), lambda c: (0, 0)),
        ],
        out_specs=pl.BlockSpec((SCAN_CHUNK, bsz, SSM_WIDTH), lambda c: (c, 0, 0)),
        out_shape=jax.ShapeDtypeStruct((s_len, bsz, SSM_WIDTH), F32),
        scratch_shapes=[
            pltpu.VMEM((SCAN_CHUNK, bsz, 2 * HALF_STATE), F32),
            pltpu.VMEM((2, bsz, 2 * HALF_STATE), F32),
        ],
        compiler_params=_cparams(1),
        name="s5",
    )(u_tm3, a_bc, b_cat, c_cat, d_skip)


def _s5_params(a_re, a_im, log_dt, b_re, b_im, c_re, c_im, bsz):
    dt = jnp.exp(log_dt)[:, None]
    mag = jnp.exp(a_re * dt)
    abar_r = mag * jnp.cos(a_im * dt)
    abar_i = mag * jnp.sin(a_im * dt)
    den = a_re * a_re + a_im * a_im
    nr = abar_r - 1.0
    ni = abar_i
    fr = (nr * a_re + ni * a_im) / den
    fi = (ni * a_re - nr * a_im) / den
    bbar_r = fr[..., None] * b_re - fi[..., None] * b_im
    bbar_i = fr[..., None] * b_im + fi[..., None] * b_re
    gh = SSM_GROUPS // 2
    eye = jnp.eye(gh, dtype=F32)

    def a_half(hf):
        sl = slice(hf * gh, (hf + 1) * gh)
        row = jnp.concatenate([abar_r[sl].reshape(-1), abar_i[sl].reshape(-1)])
        return jnp.broadcast_to(row[None], (bsz, 2 * HALF_STATE))

    def b_half(bb, hf):
        blk = bb[hf * gh:(hf + 1) * gh]
        return jnp.einsum('gpc,gk->gckp', blk, eye).reshape(HALF_U, HALF_STATE)

    def c_half(cc, hf):
        blk = cc[hf * gh:(hf + 1) * gh]
        return jnp.einsum('gcp,gk->gpkc', blk, eye).reshape(HALF_STATE, HALF_U)

    a_bc = jnp.stack([a_half(0), a_half(1)])
    b_cat = jnp.stack([jnp.concatenate([b_half(bbar_r, hf), b_half(bbar_i, hf)], axis=1)
                       for hf in range(2)]).astype(BF16)
    c_cat = jnp.stack([jnp.concatenate([c_half(c_re, hf), -c_half(c_im, hf)], axis=0)
                       for hf in range(2)]).astype(BF16)
    return a_bc, b_cat, c_cat


def _merge_kernel(x_ref, o_ref, y_ref, gates_ref, wab_ref, wglu_ref, bglu_ref, wout_ref,
                  gffn_ref, wrh_ref, wrl_ref, br_ref,
                  x1_ref, xn_ref, idx_ref, gate_ref, rank_ref, cnt_ref, carry_ref):
    @pl.when(pl.program_id(0) == 0)
    def _():
        carry_ref[...] = jnp.zeros_like(carry_ref)

    subs = [pl.ds(s * TOK_TILE, TOK_TILE) for s in range(MERGE_SUBTILES)]
    nt = (((1,), (1,)), ((), ()))

    attn_branch = [jnp.dot(o_ref[s, :].astype(BF16), wab_ref[...], preferred_element_type=F32)
                   for s in subs]
    zg = [jnp.dot(jax.nn.gelu(y_ref[s, :]).astype(BF16), wglu_ref[...],
                  preferred_element_type=F32) + bglu_ref[...] for s in subs]
    mixed = []
    for i, s in enumerate(subs):
        ssm_branch = zg[i][:, :D_MODEL] * jax.nn.sigmoid(zg[i][:, D_MODEL:])
        mixed.append((jax.nn.sigmoid(gates_ref[s, :D_MODEL]) * attn_branch[i]
                      + jax.nn.sigmoid(gates_ref[s, D_MODEL:]) * ssm_branch).astype(BF16))
    x1 = [x_ref[s, :] + jnp.dot(mixed[i], wout_ref[...], preferred_element_type=F32)
          for i, s in enumerate(subs)]
    xh, xl = [], []
    for i, s in enumerate(subs):
        x1_ref[s, :] = x1[i]
        xn = _rms(x1[i], gffn_ref[...])
        xn_ref[s, :] = xn
        xh.append(xn.astype(BF16))
        xl.append((xn - xh[i].astype(F32)).astype(BF16))
    logits = [(lax.dot_general(wrh_ref[...], xh[i], nt, preferred_element_type=F32)
               + lax.dot_general(wrh_ref[...], xl[i], nt, preferred_element_type=F32)
               + lax.dot_general(wrl_ref[...], xh[i], nt, preferred_element_type=F32))
              + br_ref[...] for i in range(MERGE_SUBTILES)]

    e_iota = lax.broadcasted_iota(jnp.int32, (N_EXPERTS, TOK_TILE), 0).astype(F32)
    k_iota = lax.broadcasted_iota(jnp.int32, (SUBLANES, TOK_TILE), 0)
    rr = lax.broadcasted_iota(jnp.int32, (TOK_TILE, TOK_TILE), 0)
    cc = lax.broadcasted_iota(jnp.int32, (TOK_TILE, TOK_TILE), 1)
    earlier = jnp.where(rr < cc, 1.0, 0.0).astype(BF16)

    for i, s in enumerate(subs):
        work = logits[i]
        hits, vals = [], []
        sel = jnp.zeros((N_EXPERTS, TOK_TILE), F32)
        for _ in range(TOP_K):
            m = jnp.max(work, axis=0, keepdims=True)
            pick = jnp.min(jnp.where(work == m, e_iota, float(N_EXPERTS)), axis=0, keepdims=True)
            hit = e_iota == pick
            work = jnp.where(hit, -jnp.inf, work)
            sel = sel + jnp.where(hit, 1.0, 0.0)
            hits.append((hit, pick))
            vals.append(m)
        before = jnp.dot(sel.astype(BF16), earlier, preferred_element_type=F32) + carry_ref[...]
        exps = [jnp.exp(v - vals[0]) for v in vals]
        denom = exps[0] + exps[1] + exps[2] + exps[3]
        idx = jnp.zeros((SUBLANES, TOK_TILE), jnp.int32)
        gate = jnp.zeros((SUBLANES, TOK_TILE), F32)
        rank = jnp.zeros((SUBLANES, TOK_TILE), jnp.int32)
        for k in range(TOP_K):
            hit, pick = hits[k]
            rk = jnp.sum(jnp.where(hit, before, 0.0), axis=0, keepdims=True)
            idx = jnp.where(k_iota == k, pick.astype(jnp.int32), idx)
            gate = jnp.where(k_iota == k, exps[k] / denom, gate)
            rank = jnp.where(k_iota == k, rk.astype(jnp.int32), rank)
        idx_ref[:, s] = idx
        gate_ref[:, s] = gate
        rank_ref[:, s] = rank
        carry_ref[...] += jnp.sum(sel, axis=1, keepdims=True)
    cnt_ref[...] = jnp.broadcast_to(carry_ref[...], cnt_ref.shape)


def _merge(x2d, o2d, y_tm2, gates, wab, wglu, bglu, wout, gffn, wrh_t, wrl_t, br_col, s_len):
    n = x2d.shape[0]
    tile = MERGE_SUBTILES * TOK_TILE
    tiles_per_seq = s_len // tile
    full = lambda shape: pl.BlockSpec(shape, lambda i: (0,) * len(shape))
    row = lambda w: pl.BlockSpec((tile, w), lambda i: (i, 0))
    col = pl.BlockSpec((SUBLANES, tile), lambda i: (0, i))
    return pl.pallas_call(
        _merge_kernel,
        grid=(n // tile,),
        in_specs=[
            row(D_MODEL), row(SB_WIDTH),
            pl.BlockSpec((tile, SSM_WIDTH),
                         lambda i: (i % tiles_per_seq, i // tiles_per_seq)),
            row(2 * D_MODEL),
            full((SB_WIDTH, D_MODEL)), full((SSM_WIDTH, 2 * D_MODEL)), full((1, 2 * D_MODEL)),
            full((D_MODEL, D_MODEL)), full((1, D_MODEL)),
            full((N_EXPERTS, D_MODEL)), full((N_EXPERTS, D_MODEL)), full((N_EXPERTS, 1)),
        ],
        out_specs=[row(D_MODEL), row(D_MODEL), col, col, col, full((N_EXPERTS, LANES))],
        out_shape=[
            jax.ShapeDtypeStruct((n, D_MODEL), F32),
            jax.ShapeDtypeStruct((n, D_MODEL), F32),
            jax.ShapeDtypeStruct((SUBLANES, n), jnp.int32),
            jax.ShapeDtypeStruct((SUBLANES, n), F32),
            jax.ShapeDtypeStruct((SUBLANES, n), jnp.int32),
            jax.ShapeDtypeStruct((N_EXPERTS, LANES), F32),
        ],
        scratch_shapes=[pltpu.VMEM((N_EXPERTS, 1), F32)],
        compiler_params=_cparams(1),
        name="merge",
    )(x2d, o2d, y_tm2, gates, wab, wglu, bglu, wout, gffn, wrh_t, wrl_t, br_col)


def _slot_row(ref, slot):
    return ref.at[slot >> (SUBLANES.bit_length() - 1), pl.ds(slot & (SUBLANES - 1), 1)]


def _start_row_gather(idx_ref, idx_base, src_ref, dst_ref, sem):
    for g in range(TILE_GROUPS):
        for j in range(SUBLANES):
            row = idx_ref[idx_base + g * SUBLANES + j]
            pltpu.make_async_copy(_slot_row(src_ref, row), dst_ref.at[g, pl.ds(j, 1)],
                                  sem).start(priority=j % 2)


def _wait_row_gather(src_ref, dst_ref, sem):
    pltpu.make_async_copy(src_ref.at[pl.ds(0, TILE_GROUPS)], dst_ref, sem).wait()


def _dispatch_kernel(slot_ref, xn_ref, xs_ref, sem):
    def issue(g, _):
        for k in range(TOP_K):
            for j in range(SUBLANES):
                slot = slot_ref[k * TOK_TILE + g * SUBLANES + j]
                pltpu.make_async_copy(xn_ref.at[g, pl.ds(j, 1)], _slot_row(xs_ref, slot),
                                      sem).start(priority=j % 2)
        return 0

    lax.fori_loop(0, TILE_GROUPS, issue, 0)
    for _ in range(TOP_K):
        pltpu.make_async_copy(xn_ref, xs_ref.at[pl.ds(0, TILE_GROUPS)], sem).wait()


def _dispatch(slots_flat, xn3):
    n = xn3.shape[0] * SUBLANES
    return pl.pallas_call(
        _dispatch_kernel,
        grid=(n // TOK_TILE,),
        in_specs=[
            pl.BlockSpec((TOP_K * TOK_TILE,), lambda i: (i,), memory_space=pltpu.SMEM),
            pl.BlockSpec((TILE_GROUPS, SUBLANES, D_MODEL), lambda i: (i, 0, 0)),
        ],
        out_specs=pl.BlockSpec(memory_space=pl.ANY),
        out_shape=jax.ShapeDtypeStruct((n * TOP_K // SUBLANES, SUBLANES, D_MODEL), F32),
        scratch_shapes=[pltpu.SemaphoreType.DMA],
        compiler_params=_cparams(1),
        name="dispatch",
    )(slots_flat, xn3)


FLAG_VALID, FLAG_FIRST_VISIT, FLAG_NEW_EXPERT = 1, 2, 4


def _experts_kernel(blk_ref, exp_ref, flag_ref, start_ref,
                    xs_ref, wup_ref, bup_ref, wdn_ref, bdn_ref, ys_ref,
                    wup_bf_ref, wdn_bf_ref):
    i = pl.program_id(0)
    flags = flag_ref[i]

    @pl.when((flags & FLAG_NEW_EXPERT) != 0)
    def _():
        wup_bf_ref[...] = wup_ref[0].astype(BF16)
        wdn_bf_ref[...] = wdn_ref[0].astype(BF16)

    @pl.when((flags & FLAG_VALID) != 0)
    def _():
        e = exp_ref[i]
        gu = jnp.dot(xs_ref[...].astype(BF16), wup_bf_ref[...],
                     preferred_element_type=F32) + bup_ref[0]
        g = jnp.minimum(gu[:, :D_FF], SWIGLU_LIMIT)
        up = jnp.clip(gu[:, D_FF:], -SWIGLU_LIMIT, SWIGLU_LIMIT)
        act = (up + 1.0) * g * jax.nn.sigmoid(SWIGLU_ALPHA * g)
        y = jnp.dot(act.astype(BF16), wdn_bf_ref[...],
                    preferred_element_type=F32) + bdn_ref[0]
        row = blk_ref[i] * ROW_BLK + lax.broadcasted_iota(jnp.int32, (ROW_BLK, 1), 0)
        mine = (row >= start_ref[e]) & (row < start_ref[e + 1])

        @pl.when((flags & FLAG_FIRST_VISIT) != 0)
        def _():
            ys_ref[...] = jnp.where(mine, y, 0.0)

        @pl.when((flags & FLAG_FIRST_VISIT) == 0)
        def _():
            ys_ref[...] = jnp.where(mine, y, ys_ref[...])


def _experts(item_blk, item_exp, item_flag, starts, xs, w_up, b_up3, w_down, b_down3):
    rows = xs.shape[0]
    n_items = item_blk.shape[0]
    grid_spec = pltpu.PrefetchScalarGridSpec(
        num_scalar_prefetch=4,
        grid=(n_items,),
        in_specs=[
            pl.BlockSpec((ROW_BLK, D_MODEL), lambda i, b, e, f, s: (b[i], 0)),
            pl.BlockSpec((1, D_MODEL, 2 * D_FF), lambda i, b, e, f, s: (e[i], 0, 0)),
            pl.BlockSpec((1, 1, 2 * D_FF), lambda i, b, e, f, s: (e[i], 0, 0)),
            pl.BlockSpec((1, D_FF, D_MODEL), lambda i, b, e, f, s: (e[i], 0, 0)),
            pl.BlockSpec((1, 1, D_MODEL), lambda i, b, e, f, s: (e[i], 0, 0)),
        ],
        out_specs=pl.BlockSpec((ROW_BLK, D_MODEL), lambda i, b, e, f, s: (b[i], 0)),
        scratch_shapes=[
            pltpu.VMEM((D_MODEL, 2 * D_FF), BF16),
            pltpu.VMEM((D_FF, D_MODEL), BF16),
        ],
    )
    return pl.pallas_call(
        _experts_kernel,
        grid_spec=grid_spec,
        out_shape=jax.ShapeDtypeStruct((rows, D_MODEL), F32),
        compiler_params=_cparams(1),
        name="experts",
    )(item_blk, item_exp, item_flag, starts, xs, w_up, b_up3, w_down, b_down3)


def _build_items(counts, n_rows):
    ends = jnp.cumsum(counts)
    starts = ends - counts
    n_blocks = n_rows // ROW_BLK
    n_items = n_blocks + N_ITEMS_EXTRA
    lo = jnp.arange(n_blocks, dtype=jnp.int32)[:, None] * ROW_BLK
    overlap = jnp.minimum(ends[None, :], lo + ROW_BLK) - jnp.maximum(starts[None, :], lo)
    live = (overlap > 0).reshape(-1)
    n_live = jnp.sum(live.astype(jnp.int32))
    (pos,) = jnp.nonzero(live, size=n_items, fill_value=0)
    pos = pos.astype(jnp.int32)
    k = jnp.arange(n_items, dtype=jnp.int32)
    valid = k < n_live
    pos = jnp.where(valid, pos, pos[jnp.maximum(n_live - 1, 0)])
    blk = pos // N_EXPERTS
    exp = pos % N_EXPERTS
    prev_blk = jnp.concatenate([jnp.full((1,), -1, jnp.int32), blk[:-1]])
    prev_exp = jnp.concatenate([jnp.full((1,), -1, jnp.int32), exp[:-1]])
    flags = (jnp.where(valid, FLAG_VALID, 0)
             | jnp.where(valid & (blk != prev_blk), FLAG_FIRST_VISIT, 0)
             | jnp.where(valid & (exp != prev_exp), FLAG_NEW_EXPERT, 0)).astype(jnp.int32)
    starts33 = jnp.concatenate([starts, ends[-1:]]).astype(jnp.int32)
    return blk, exp, flags, starts33


def _combine_kernel(slot_cur_ref, slot_next_ref, ys_ref, x1_ref, gate_ref, p_ref, gpg_ref,
                    wpg_ref, wpp_ref, gpp_ref, out_ref, rows_ref, sem):
    i = pl.program_id(0)
    last = pl.num_programs(0) - 1
    cur = i % 2
    nxt = 1 - cur

    def start(slot_ref, buf):
        for k in range(TOP_K):
            _start_row_gather(slot_ref, k * TOK_TILE, ys_ref, rows_ref.at[buf, k], sem.at[buf])

    def wait(buf):
        for k in range(TOP_K):
            _wait_row_gather(ys_ref, rows_ref.at[buf, k], sem.at[buf])

    @pl.when(i == 0)
    def _():
        start(slot_cur_ref, 0)

    wait(cur)
    x2 = x1_ref[...]
    gate = jnp.concatenate([gate_ref[...], jnp.zeros((LANES - SUBLANES, TOK_TILE), F32)],
                           axis=0).T
    for k in range(TOP_K):
        x2 = x2 + gate[:, k:k + 1] * rows_ref[cur, k].reshape(TOK_TILE, D_MODEL)
    start(slot_next_ref, nxt)
    ple = _rms(jnp.dot(p_ref[...].astype(BF16), wpp_ref[...], preferred_element_type=F32),
               gpp_ref[...])
    pg = jax.nn.sigmoid(jnp.dot(_rms(x2, gpg_ref[...]).astype(BF16), wpg_ref[...],
                                preferred_element_type=F32))
    out_ref[...] = x2 + pg * ple

    @pl.when(i == last)
    def _():
        wait(nxt)


def _combine(slots_flat, ys3, x1, gate, p2d, gpg, wpg, wpp, gpp):
    n = x1.shape[0]
    n_tiles = n // TOK_TILE
    full = lambda shape: pl.BlockSpec(shape, lambda i: (0,) * len(shape))
    row = lambda w: pl.BlockSpec((TOK_TILE, w), lambda i: (i, 0))
    return pl.pallas_call(
        _combine_kernel,
        grid=(n_tiles,),
        in_specs=[
            pl.BlockSpec((TOP_K * TOK_TILE,), lambda i: (i,), memory_space=pltpu.SMEM),
            pl.BlockSpec((TOP_K * TOK_TILE,), lambda i: (jnp.minimum(i + 1, n_tiles - 1),),
                         memory_space=pltpu.SMEM),
            pl.BlockSpec(memory_space=pl.ANY),
            row(D_MODEL), pl.BlockSpec((SUBLANES, TOK_TILE), lambda i: (0, i)), row(PLE_DIM),
            full((1, D_MODEL)), full((D_MODEL, D_MODEL)), full((PLE_DIM, D_MODEL)),
            full((1, D_MODEL)),
        ],
        out_specs=row(D_MODEL),
        out_shape=jax.ShapeDtypeStruct((n, D_MODEL), F32),
        scratch_shapes=[pltpu.VMEM((2, TOP_K, TILE_GROUPS, SUBLANES, D_MODEL), F32),
                        pltpu.SemaphoreType.DMA((2,))],
        compiler_params=_cparams(1),
        name="combine",
    )(slots_flat, slots_flat, ys3, x1, gate, p2d, gpg, wpg, wpp, gpp)


def kernel(x, p, g_mix, w_in, g_q, g_k, w_attn_branch, a_re, a_im, log_dt, b_re, b_im, c_re, c_im, d_skip, w_glu, b_glu, w_out, g_ffn, w_router, b_router, w_up, b_up, w_down, b_down, g_ple_gate, w_ple_gate, w_ple_proj, g_ple_post):
    bsz, s_len, d = x.shape
    depth = w_in.shape[0]
    n = bsz * s_len
    assert d == D_MODEL and s_len % (MERGE_SUBTILES * TOK_TILE) == 0 and s_len % SCAN_CHUNK == 0
    assert bsz == SUBLANES, "the S5 scan keeps the batch on the sublane axis"

    x2d = x.reshape(n, d)
    for i in range(depth):
        qkv, u_tm, gates = _inproj(x2d, g_mix[i][None], w_in[i].astype(BF16), bsz, s_len)
        o = _attn(qkv.reshape(bsz, s_len, 3 * SB_WIDTH),
                  jnp.tile(g_q[i], 2)[None], jnp.tile(g_k[i], 2)[None])
        a_bc, b_cat, c_cat = _s5_params(a_re[i], a_im[i], log_dt[i], b_re[i], b_im[i],
                                        c_re[i], c_im[i], bsz)
        y_tm = _s5(u_tm.reshape(s_len, bsz, SSM_WIDTH), a_bc, b_cat, c_cat, d_skip[i][None])

        wr_t = w_router[i].T
        wr_hi = wr_t.astype(BF16)
        wr_lo = (wr_t - wr_hi.astype(F32)).astype(BF16)
        x1, xn, idx, gate, rank, counts = _merge(
            x2d, o.reshape(n, SB_WIDTH), y_tm.reshape(s_len, bsz * SSM_WIDTH), gates,
            w_attn_branch[i].astype(BF16), w_glu[i].astype(BF16), b_glu[i][None],
            w_out[i].astype(BF16), g_ffn[i][None], wr_hi, wr_lo, b_router[i][:, None], s_len)

        counts_i = counts[:, 0].astype(jnp.int32)
        blk, exp, flags, starts33 = _build_items(counts_i, n * TOP_K)
        slots = starts33[idx[:TOP_K]] + rank[:TOP_K]
        slots_flat = slots.reshape(TOP_K, n // TOK_TILE, TOK_TILE).transpose(1, 0, 2).reshape(-1)

        xs3 = _dispatch(slots_flat, xn.reshape(n // SUBLANES, SUBLANES, D_MODEL))
        ys = _experts(blk, exp, flags, starts33, xs3.reshape(n * TOP_K, D_MODEL),
                      w_up[i], b_up[i][:, None, :], w_down[i], b_down[i][:, None, :])
        ys3 = ys.reshape(n * TOP_K // SUBLANES, SUBLANES, D_MODEL)
        x2d = _combine(slots_flat, ys3, x1, gate, p[i].reshape(n, PLE_DIM),
                       g_ple_gate[i][None], w_ple_gate[i].astype(BF16),
                       w_ple_proj[i].astype(BF16), g_ple_post[i][None])
    return x2d.reshape(bsz, s_len, d)
```

```python
import jax
import jax.numpy as jnp
from jax import lax
from jax.experimental import pallas as pl
from jax.experimental.pallas import tpu as pltpu

F32 = jnp.float32
BF16 = jnp.bfloat16

D_MODEL = 1024
SB_HEADS = 8
SB_HEAD_DIM = 64
SB_WIDTH = SB_HEADS * SB_HEAD_DIM
SSM_GROUP = 16
SSM_WIDTH = 512
SSM_GROUPS = SSM_WIDTH // SSM_GROUP
SSM_STATE = 64
PLE_DIM = 256
N_EXPERTS = 32
TOP_K = 4
D_FF = D_MODEL
SWIGLU_LIMIT = 7.0
SWIGLU_ALPHA = 1.702
EPS = 1e-6

LANES = 128
SUBLANES = 8
VMEM_LIMIT = 56 * 1024 * 1024

TOK_TILE = 256
TILE_GROUPS = TOK_TILE // SUBLANES
MERGE_SUBTILES = 2
ATT_BLK = 128
DEAD_LOG_WEIGHT = -104.0
ATT_HEADS_PER_STEP = 8
SCAN_CHUNK = 64
SCAN_LANES = 512
HALF_U = SSM_WIDTH // 2
HALF_STATE = SSM_GROUPS // 2 * SSM_STATE
ROW_BLK = 256
FF_CHUNK = 256
N_ITEMS_EXTRA = N_EXPERTS - 1


def _cparams(n_axes):
    return pltpu.CompilerParams(
        dimension_semantics=("arbitrary",) * n_axes,
        vmem_limit_bytes=VMEM_LIMIT)


def _rms(x, g):
    ms = jnp.mean(x * x, axis=-1, keepdims=True)
    return x * lax.rsqrt(ms + EPS) * g


def _inproj_kernel(x_ref, g_ref, w_ref, qkv_ref, u_ref, gates_ref):
    h = _rms(x_ref[...], g_ref[...]).astype(BF16)
    n_qkv = 3 * SB_WIDTH
    n_u = n_qkv + SSM_WIDTH
    qkv_ref[...] = jnp.dot(h, w_ref[:, :n_qkv], preferred_element_type=F32)
    u_ref[...] = jnp.dot(h, w_ref[:, n_qkv:n_u], preferred_element_type=F32)
    gates_ref[...] = jnp.dot(h, w_ref[:, n_u:], preferred_element_type=F32)


def _inproj(x2d, g_mix, w_in_bf, bsz, s_len):
    n = x2d.shape[0]
    tiles_per_seq = s_len // TOK_TILE
    in_cols = w_in_bf.shape[1]
    return pl.pallas_call(
        _inproj_kernel,
        grid=(n // TOK_TILE,),
        in_specs=[
            pl.BlockSpec((TOK_TILE, D_MODEL), lambda i: (i, 0)),
            pl.BlockSpec((1, D_MODEL), lambda i: (0, 0)),
            pl.BlockSpec((D_MODEL, in_cols), lambda i: (0, 0)),
        ],
        out_specs=[
            pl.BlockSpec((TOK_TILE, 3 * SB_WIDTH), lambda i: (i, 0)),
            pl.BlockSpec((TOK_TILE, SSM_WIDTH),
                         lambda i: (i % tiles_per_seq, i // tiles_per_seq)),
            pl.BlockSpec((TOK_TILE, 2 * D_MODEL), lambda i: (i, 0)),
        ],
        out_shape=[
            jax.ShapeDtypeStruct((n, 3 * SB_WIDTH), F32),
            jax.ShapeDtypeStruct((s_len, bsz * SSM_WIDTH), F32),
            jax.ShapeDtypeStruct((n, 2 * D_MODEL), F32),
        ],
        compiler_params=_cparams(1),
        name="inproj",
    )(x2d, g_mix, w_in_bf)


def _attn_kernel(q_ref, k_ref, v_ref, gq_ref, gk_ref, o_ref,
                 qs_ref, ks_ref, vs_ref, w2_ref, c_ref, acc_ref):
    s_len = q_ref.shape[1]
    n_blk = s_len // ATT_BLK
    n_pairs = q_ref.shape[2] // LANES
    lane = lax.broadcasted_iota(jnp.int32, (1, LANES), 1)
    head0 = lane < SB_HEAD_DIM

    li = lax.broadcasted_iota(jnp.int32, (LANES, LANES), 0) // SB_HEAD_DIM
    lj = lax.broadcasted_iota(jnp.int32, (LANES, LANES), 1) // SB_HEAD_DIM
    head_mean = jnp.where(li == lj, 1.0 / SB_HEAD_DIM, 0.0).astype(BF16)

    def head_rms(t, g):
        sq = t * t
        hi = sq.astype(BF16)
        lo = (sq - hi.astype(F32)).astype(BF16)
        ms = (jnp.dot(hi, head_mean, preferred_element_type=F32)
              + jnp.dot(lo, head_mean, preferred_element_type=F32))
        return t * lax.rsqrt(ms + EPS) * g

    scale = SB_HEAD_DIM ** -0.5

    def prep(qi, _):
        rows = pl.ds(pl.multiple_of(qi * ATT_BLK, ATT_BLK), ATT_BLK)
        for p in range(n_pairs):
            cols = slice(p * LANES, (p + 1) * LANES)
            qn = head_rms(q_ref[0, rows, cols], gq_ref[...]) * scale
            qs_ref[p, qi, :ATT_BLK, :] = jnp.where(head0, qn, 0.0).astype(BF16)
            qs_ref[p, qi, ATT_BLK:, :] = jnp.where(head0, 0.0, qn).astype(BF16)
            ks_ref[p, rows, :] = head_rms(k_ref[0, rows, cols], gk_ref[...]).astype(BF16)
            vs_ref[p, rows, :] = v_ref[0, rows, cols].astype(BF16)
        return 0

    lax.fori_loop(0, n_blk, prep, 0)

    r = lax.broadcasted_iota(jnp.int32, (2 * ATT_BLK, 2 * ATT_BLK), 0)
    c = lax.broadcasted_iota(jnp.int32, (2 * ATT_BLK, 2 * ATT_BLK), 1)
    r = jnp.where(r >= ATT_BLK, r - ATT_BLK, r)
    w2_ref[...] = jnp.where((c >= ATT_BLK) | (r > c), 1.0, 0.0).astype(BF16)

    ti = lax.broadcasted_iota(jnp.int32, (2 * ATT_BLK, ATT_BLK), 0)
    si = lax.broadcasted_iota(jnp.int32, (2 * ATT_BLK, ATT_BLK), 1)
    causal = si < jnp.where(ti >= ATT_BLK, ti - ATT_BLK, ti)

    def tiles(qi, kv_rows, diag):
        pairs = range(n_pairs)
        z = [lax.dot_general(qs_ref[p, qi], ks_ref[p, kv_rows, :], (((1,), (1,)), ((), ())),
                             preferred_element_type=F32) for p in pairs]
        log_beta, stacked = [], []
        for p in pairs:
            sp = jnp.maximum(z[p], 0.0) + jnp.log(1.0 + jnp.exp(-jnp.abs(z[p])))
            log_keep = -sp
            log_beta.append(z[p] - sp)
            if diag:
                log_keep = jnp.where(causal, log_keep, 0.0)
            hi = log_keep.astype(BF16)
            lo = (log_keep - hi.astype(F32)).astype(BF16)
            stacked.append(jnp.concatenate([hi, lo], axis=1))
        sums = [jnp.dot(stacked[p], w2_ref[...], preferred_element_type=F32) for p in pairs]
        w = []
        for p in pairs:
            wp = jnp.exp(log_beta[p] + sums[p][:, :ATT_BLK] + c_ref[p])
            if diag:
                wp = jnp.where(causal, wp, 0.0)
            w.append(wp.astype(BF16))
            c_ref[p] += sums[p][:, ATT_BLK:]
        pv = [jnp.dot(w[p], vs_ref[p, kv_rows, :], preferred_element_type=F32) for p in pairs]
        for p in pairs:
            acc_ref[p] += pv[p]

    def qblock(qi, _):
        q_rows = pl.ds(pl.multiple_of(qi * ATT_BLK, ATT_BLK), ATT_BLK)
        c_ref[...] = jnp.zeros_like(c_ref)
        acc_ref[...] = jnp.zeros_like(acc_ref)
        tiles(qi, q_rows, True)

        def live(carry):
            jj, c_max = carry
            return (jj < qi) & (c_max > DEAD_LOG_WEIGHT)

        def kvblock(carry):
            jj, _ = carry
            j = qi - 1 - jj
            kv_rows = pl.ds(pl.multiple_of(j * ATT_BLK, ATT_BLK), ATT_BLK)
            tiles(qi, kv_rows, False)
            return jj + 1, jnp.max(c_ref[...])

        lax.while_loop(live, kvblock, (jnp.int32(0), jnp.max(c_ref[...])))
        for p in range(n_pairs):
            o_ref[0, q_rows, p * LANES:(p + 1) * LANES] = jnp.where(
                head0, acc_ref[p, :ATT_BLK, :], acc_ref[p, ATT_BLK:, :])
        return 0

    lax.fori_loop(0, n_blk, qblock, 0)


def _attn(qkv3, gq2, gk2):
    bsz, s_len, _ = qkv3.shape
    n_pairs = ATT_HEADS_PER_STEP // 2
    n_steps = SB_HEADS // ATT_HEADS_PER_STEP
    blk = (1, s_len, n_pairs * LANES)
    return pl.pallas_call(
        _attn_kernel,
        grid=(bsz, n_steps),
        in_specs=[
            pl.BlockSpec(blk, lambda b, p: (b, 0, p)),
            pl.BlockSpec(blk, lambda b, p: (b, 0, n_steps + p)),
            pl.BlockSpec(blk, lambda b, p: (b, 0, 2 * n_steps + p)),
            pl.BlockSpec((1, LANES), lambda b, p: (0, 0)),
            pl.BlockSpec((1, LANES), lambda b, p: (0, 0)),
        ],
        out_specs=pl.BlockSpec(blk, lambda b, p: (b, 0, p)),
        out_shape=jax.ShapeDtypeStruct((bsz, s_len, SB_WIDTH), F32),
        scratch_shapes=[
            pltpu.VMEM((n_pairs, s_len // ATT_BLK, 2 * ATT_BLK, LANES), BF16),
            pltpu.VMEM((n_pairs, s_len, LANES), BF16),
            pltpu.VMEM((n_pairs, s_len, LANES), BF16),
            pltpu.VMEM((2 * ATT_BLK, 2 * ATT_BLK), BF16),
            pltpu.VMEM((n_pairs, 2 * ATT_BLK, ATT_BLK), F32),
            pltpu.VMEM((n_pairs, 2 * ATT_BLK, LANES), F32),
        ],
        compiler_params=_cparams(2),
        name="attn",
    )(qkv3, qkv3, qkv3, gq2, gk2)


def _s5_kernel(u_ref, a_ref, b_ref, c_ref, d_ref, y_ref, hbuf_ref, state_ref):
    tc, bsz, _ = u_ref.shape
    rows = tc * bsz

    @pl.when(pl.program_id(0) == 0)
    def _():
        state_ref[...] = jnp.zeros_like(state_ref)

    u2 = u_ref[...].reshape(rows, SSM_WIDTH)
    ub = u2.astype(BF16)
    for hf in range(2):
        uh = ub[:, hf * HALF_U:(hf + 1) * HALF_U]
        xh = jnp.dot(uh, b_ref[hf], preferred_element_type=F32)
        hbuf_ref[...] = xh.reshape(tc, bsz, 2 * HALF_STATE)
        for lc in range(HALF_STATE // SCAN_LANES):
            re = pl.ds(lc * SCAN_LANES, SCAN_LANES)
            im = pl.ds(HALF_STATE + lc * SCAN_LANES, SCAN_LANES)
            ar = a_ref[hf, :, re]
            ai = a_ref[hf, :, im]

            def step(t, carry, re=re, im=im, ar=ar, ai=ai):
                hr, hi = carry
                nr = ar * hr - ai * hi + hbuf_ref[t, :, re]
                ni = ar * hi + ai * hr + hbuf_ref[t, :, im]
                hbuf_ref[t, :, re] = nr
                hbuf_ref[t, :, im] = ni
                return nr, ni

            hr, hi = lax.fori_loop(0, tc, step,
                                   (state_ref[hf, :, re], state_ref[hf, :, im]),
                                   unroll=8)
            state_ref[hf, :, re] = hr
            state_ref[hf, :, im] = hi
        hb = hbuf_ref[...].reshape(rows, 2 * HALF_STATE).astype(BF16)
        yh = jnp.dot(hb, c_ref[hf], preferred_element_type=F32)
        cols = slice(hf * HALF_U, (hf + 1) * HALF_U)
        yh = yh + d_ref[:, cols] * u2[:, cols]
        y_ref[:, :, cols] = yh.reshape(tc, bsz, HALF_U)


def _s5(u_tm3, a_bc, b_cat, c_cat, d_skip):
    s_len, bsz, _ = u_tm3.shape
    return pl.pallas_call(
        _s5_kernel,
        grid=(s_len // SCAN_CHUNK,),
        in_specs=[
            pl.BlockSpec((SCAN_CHUNK, bsz, SSM_WIDTH), lambda c: (c, 0, 0)),
            pl.BlockSpec((2, bsz, 2 * HALF_STATE), lambda c: (0, 0, 0)),
            pl.BlockSpec((2, HALF_U, 2 * HALF_STATE), lambda c: (0, 0, 0)),
            pl.BlockSpec((2, 2 * HALF_STATE, HALF_U), lambda c: (0, 0, 0)),
            pl.BlockSpec((1, SSM_WIDTH), lambda c: (0, 0)),
        ],
        out_specs=pl.BlockSpec((SCAN_CHUNK, bsz, SSM_WIDTH), lambda c: (c, 0, 0)),
        out_shape=jax.ShapeDtypeStruct((s_len, bsz, SSM_WIDTH), F32),
        scratch_shapes=[
            pltpu.VMEM((SCAN_CHUNK, bsz, 2 * HALF_STATE), F32),
            pltpu.VMEM((2, bsz, 2 * HALF_STATE), F32),
        ],
        compiler_params=_cparams(1),
        name="s5",
    )(u_tm3, a_bc, b_cat, c_cat, d_skip)


def _s5_params(a_re, a_im, log_dt, b_re, b_im, c_re, c_im, bsz):
    dt = jnp.exp(log_dt)[:, None]
    mag = jnp.exp(a_re * dt)
    abar_r = mag * jnp.cos(a_im * dt)
    abar_i = mag * jnp.sin(a_im * dt)
    den = a_re * a_re + a_im * a_im
    nr = abar_r - 1.0
    ni = abar_i
    fr = (nr * a_re + ni * a_im) / den
    fi = (ni * a_re - nr * a_im) / den
    bbar_r = fr[..., None] * b_re - fi[..., None] * b_im
    bbar_i = fr[..., None] * b_im + fi[..., None] * b_re
    gh = SSM_GROUPS // 2
    eye = jnp.eye(gh, dtype=F32)

    def a_half(hf):
        sl = slice(hf * gh, (hf + 1) * gh)
        row = jnp.concatenate([abar_r[sl].reshape(-1), abar_i[sl].reshape(-1)])
        return jnp.broadcast_to(row[None], (bsz, 2 * HALF_STATE))

    def b_half(bb, hf):
        blk = bb[hf * gh:(hf + 1) * gh]
        return jnp.einsum('gpc,gk->gckp', blk, eye).reshape(HALF_U, HALF_STATE)

    def c_half(cc, hf):
        blk = cc[hf * gh:(hf + 1) * gh]
        return jnp.einsum('gcp,gk->gpkc', blk, eye).reshape(HALF_STATE, HALF_U)

    a_bc = jnp.stack([a_half(0), a_half(1)])
    b_cat = jnp.stack([jnp.concatenate([b_half(bbar_r, hf), b_half(bbar_i, hf)], axis=1)
                       for hf in range(2)]).astype(BF16)
    c_cat = jnp.stack([jnp.concatenate([c_half(c_re, hf), -c_half(c_im, hf)], axis=0)
                       for hf in range(2)]).astype(BF16)
    return a_bc, b_cat, c_cat


def _merge_kernel(x_ref, o_ref, y_ref, gates_ref, wab_ref, wglu_ref, bglu_ref, wout_ref,
                  gffn_ref, wrh_ref, wrl_ref, br_ref,
                  x1_ref, xn_ref, idx_ref, gate_ref, rank_ref, cnt_ref, carry_ref):
    @pl.when(pl.program_id(0) == 0)
    def _():
        carry_ref[...] = jnp.zeros_like(carry_ref)

    subs = [pl.ds(s * TOK_TILE, TOK_TILE) for s in range(MERGE_SUBTILES)]
    nt = (((1,), (1,)), ((), ()))

    attn_branch = [jnp.dot(o_ref[s, :].astype(BF16), wab_ref[...], preferred_element_type=F32)
                   for s in subs]
    zg = [jnp.dot(jax.nn.gelu(y_ref[s, :]).astype(BF16), wglu_ref[...],
                  preferred_element_type=F32) + bglu_ref[...] for s in subs]
    mixed = []
    for i, s in enumerate(subs):
        ssm_branch = zg[i][:, :D_MODEL] * jax.nn.sigmoid(zg[i][:, D_MODEL:])
        mixed.append((jax.nn.sigmoid(gates_ref[s, :D_MODEL]) * attn_branch[i]
                      + jax.nn.sigmoid(gates_ref[s, D_MODEL:]) * ssm_branch).astype(BF16))
    x1 = [x_ref[s, :] + jnp.dot(mixed[i], wout_ref[...], preferred_element_type=F32)
          for i, s in enumerate(subs)]
    xh, xl = [], []
    for i, s in enumerate(subs):
        x1_ref[s, :] = x1[i]
        xn = _rms(x1[i], gffn_ref[...])
        xn_ref[s, :] = xn
        xh.append(xn.astype(BF16))
        xl.append((xn - xh[i].astype(F32)).astype(BF16))
    logits = [(lax.dot_general(wrh_ref[...], xh[i], nt, preferred_element_type=F32)
               + lax.dot_general(wrh_ref[...], xl[i], nt, preferred_element_type=F32)
               + lax.dot_general(wrl_ref[...], xh[i], nt, preferred_element_type=F32))
              + br_ref[...] for i in range(MERGE_SUBTILES)]

    e_iota = lax.broadcasted_iota(jnp.int32, (N_EXPERTS, TOK_TILE), 0).astype(F32)
    k_iota = lax.broadcasted_iota(jnp.int32, (SUBLANES, TOK_TILE), 0)
    rr = lax.broadcasted_iota(jnp.int32, (TOK_TILE, TOK_TILE), 0)
    cc = lax.broadcasted_iota(jnp.int32, (TOK_TILE, TOK_TILE), 1)
    earlier = jnp.where(rr < cc, 1.0, 0.0).astype(BF16)

    for i, s in enumerate(subs):
        work = logits[i]
        hits, vals = [], []
        sel = jnp.zeros((N_EXPERTS, TOK_TILE), F32)
        for _ in range(TOP_K):
            m = jnp.max(work, axis=0, keepdims=True)
            pick = jnp.min(jnp.where(work == m, e_iota, float(N_EXPERTS)), axis=0, keepdims=True)
            hit = e_iota == pick
            work = jnp.where(hit, -jnp.inf, work)
            sel = sel + jnp.where(hit, 1.0, 0.0)
            hits.append((hit, pick))
            vals.append(m)
        before = jnp.dot(sel.astype(BF16), earlier, preferred_element_type=F32) + carry_ref[...]
        exps = [jnp.exp(v - vals[0]) for v in vals]
        denom = exps[0] + exps[1] + exps[2] + exps[3]
        idx = jnp.zeros((SUBLANES, TOK_TILE), jnp.int32)
        gate = jnp.zeros((SUBLANES, TOK_TILE), F32)
        rank = jnp.zeros((SUBLANES, TOK_TILE), jnp.int32)
        for k in range(TOP_K):
            hit, pick = hits[k]
            rk = jnp.sum(jnp.where(hit, before, 0.0), axis=0, keepdims=True)
            idx = jnp.where(k_iota == k, pick.astype(jnp.int32), idx)
            gate = jnp.where(k_iota == k, exps[k] / denom, gate)
            rank = jnp.where(k_iota == k, rk.astype(jnp.int32), rank)
        idx_ref[:, s] = idx
        gate_ref[:, s] = gate
        rank_ref[:, s] = rank
        carry_ref[...] += jnp.sum(sel, axis=1, keepdims=True)
    cnt_ref[...] = jnp.broadcast_to(carry_ref[...], cnt_ref.shape)


def _merge(x2d, o2d, y_tm2, gates, wab, wglu, bglu, wout, gffn, wrh_t, wrl_t, br_col, s_len):
    n = x2d.shape[0]
    tile = MERGE_SUBTILES * TOK_TILE
    tiles_per_seq = s_len // tile
    full = lambda shape: pl.BlockSpec(shape, lambda i: (0,) * len(shape))
    row = lambda w: pl.BlockSpec((tile, w), lambda i: (i, 0))
    col = pl.BlockSpec((SUBLANES, tile), lambda i: (0, i))
    return pl.pallas_call(
        _merge_kernel,
        grid=(n // tile,),
        in_specs=[
            row(D_MODEL), row(SB_WIDTH),
            pl.BlockSpec((tile, SSM_WIDTH),
                         lambda i: (i % tiles_per_seq, i // tiles_per_seq)),
            row(2 * D_MODEL),
            full((SB_WIDTH, D_MODEL)), full((SSM_WIDTH, 2 * D_MODEL)), full((1, 2 * D_MODEL)),
            full((D_MODEL, D_MODEL)), full((1, D_MODEL)),
            full((N_EXPERTS, D_MODEL)), full((N_EXPERTS, D_MODEL)), full((N_EXPERTS, 1)),
        ],
        out_specs=[row(D_MODEL), row(D_MODEL), col, col, col, full((N_EXPERTS, LANES))],
        out_shape=[
            jax.ShapeDtypeStruct((n, D_MODEL), F32),
            jax.ShapeDtypeStruct((n, D_MODEL), F32),
            jax.ShapeDtypeStruct((SUBLANES, n), jnp.int32),
            jax.ShapeDtypeStruct((SUBLANES, n), F32),
            jax.ShapeDtypeStruct((SUBLANES, n), jnp.int32),
            jax.ShapeDtypeStruct((N_EXPERTS, LANES), F32),
        ],
        scratch_shapes=[pltpu.VMEM((N_EXPERTS, 1), F32)],
        compiler_params=_cparams(1),
        name="merge",
    )(x2d, o2d, y_tm2, gates, wab, wglu, bglu, wout, gffn, wrh_t, wrl_t, br_col)


def _slot_row(ref, slot):
    return ref.at[slot >> (SUBLANES.bit_length() - 1), pl.ds(slot & (SUBLANES - 1), 1)]


def _start_row_gather(idx_ref, idx_base, src_ref, dst_ref, sem):
    for g in range(TILE_GROUPS):
        for j in range(SUBLANES):
            row = idx_ref[idx_base + g * SUBLANES + j]
            pltpu.make_async_copy(_slot_row(src_ref, row), dst_ref.at[g, pl.ds(j, 1)],
                                  sem).start(priority=j % 2)


def _wait_row_gather(src_ref, dst_ref, sem):
    pltpu.make_async_copy(src_ref.at[pl.ds(0, TILE_GROUPS)], dst_ref, sem).wait()


def _dispatch_kernel(slot_ref, xn_ref, xs_ref, sem):
    def issue(g, _):
        for k in range(TOP_K):
            for j in range(SUBLANES):
                slot = slot_ref[k * TOK_TILE + g * SUBLANES + j]
                pltpu.make_async_copy(xn_ref.at[g, pl.ds(j, 1)], _slot_row(xs_ref, slot),
                                      sem).start(priority=j % 2)
        return 0

    lax.fori_loop(0, TILE_GROUPS, issue, 0)
    for _ in range(TOP_K):
        pltpu.make_async_copy(xn_ref, xs_ref.at[pl.ds(0, TILE_GROUPS)], sem).wait()


def _dispatch(slots_flat, xn3):
    n = xn3.shape[0] * SUBLANES
    return pl.pallas_call(
        _dispatch_kernel,
        grid=(n // TOK_TILE,),
        in_specs=[
            pl.BlockSpec((TOP_K * TOK_TILE,), lambda i: (i,), memory_space=pltpu.SMEM),
            pl.BlockSpec((TILE_GROUPS, SUBLANES, D_MODEL), lambda i: (i, 0, 0)),
        ],
        out_specs=pl.BlockSpec(memory_space=pl.ANY),
        out_shape=jax.ShapeDtypeStruct((n * TOP_K // SUBLANES, SUBLANES, D_MODEL), F32),
        scratch_shapes=[pltpu.SemaphoreType.DMA],
        compiler_params=_cparams(1),
        name="dispatch",
    )(slots_flat, xn3)


FLAG_VALID, FLAG_NEW_EXPERT = 1, 2


def _experts_kernel(blk_ref, exp_ref, flag_ref, start_ref,
                    xs_ref, wup_ref, bup_ref, wdn_ref, bdn_ref, ys_ref,
                    wup_bf_ref, wdn_bf_ref):
    i = pl.program_id(0)
    flags = flag_ref[i]

    @pl.when((flags & FLAG_NEW_EXPERT) != 0)
    def _():
        wup_bf_ref[...] = wup_ref[0].astype(BF16)
        wdn_bf_ref[...] = wdn_ref[0].astype(BF16)

    @pl.when((flags & FLAG_VALID) != 0)
    def _():
        e = exp_ref[i]
        x = xs_ref[...].astype(BF16)
        acts = []
        for c in range(D_FF // FF_CHUNK):
            gcols = pl.ds(c * FF_CHUNK, FF_CHUNK)
            ucols = pl.ds(D_FF + c * FF_CHUNK, FF_CHUNK)
            g = jnp.dot(x, wup_bf_ref[:, gcols], preferred_element_type=F32) + bup_ref[0, :, gcols]
            up = jnp.dot(x, wup_bf_ref[:, ucols], preferred_element_type=F32) + bup_ref[0, :, ucols]
            g = jnp.minimum(g, SWIGLU_LIMIT)
            up = jnp.clip(up, -SWIGLU_LIMIT, SWIGLU_LIMIT)
            acts.append(((up + 1.0) * g * jax.nn.sigmoid(SWIGLU_ALPHA * g)).astype(BF16))
        y = jnp.dot(jnp.concatenate(acts, axis=1), wdn_bf_ref[...],
                    preferred_element_type=F32) + bdn_ref[0]
        row = blk_ref[i] * ROW_BLK + lax.broadcasted_iota(jnp.int32, (ROW_BLK, 1), 0)
        mine = (row >= start_ref[e]) & (row < start_ref[e + 1])
        pltpu.store(ys_ref, y, mask=jnp.broadcast_to(mine, y.shape))


def _experts(item_blk, item_exp, item_flag, starts, xs, w_up, b_up3, w_down, b_down3):
    rows = xs.shape[0]
    n_items = item_blk.shape[0]
    grid_spec = pltpu.PrefetchScalarGridSpec(
        num_scalar_prefetch=4,
        grid=(n_items,),
        in_specs=[
            pl.BlockSpec((ROW_BLK, D_MODEL), lambda i, b, e, f, s: (b[i], 0)),
            pl.BlockSpec((1, D_MODEL, 2 * D_FF), lambda i, b, e, f, s: (e[i], 0, 0)),
            pl.BlockSpec((1, 1, 2 * D_FF), lambda i, b, e, f, s: (e[i], 0, 0)),
            pl.BlockSpec((1, D_FF, D_MODEL), lambda i, b, e, f, s: (e[i], 0, 0)),
            pl.BlockSpec((1, 1, D_MODEL), lambda i, b, e, f, s: (e[i], 0, 0)),
        ],
        out_specs=pl.BlockSpec((ROW_BLK, D_MODEL), lambda i, b, e, f, s: (b[i], 0)),
        scratch_shapes=[
            pltpu.VMEM((D_MODEL, 2 * D_FF), BF16),
            pltpu.VMEM((D_FF, D_MODEL), BF16),
        ],
    )
    return pl.pallas_call(
        _experts_kernel,
        grid_spec=grid_spec,
        out_shape=jax.ShapeDtypeStruct((rows, D_MODEL), F32),
        compiler_params=_cparams(1),
        name="experts",
    )(item_blk, item_exp, item_flag, starts, xs, w_up, b_up3, w_down, b_down3)


def _build_items(counts, n_rows):
    ends = jnp.cumsum(counts)
    starts = ends - counts
    n_blocks = n_rows // ROW_BLK
    n_items = n_blocks + N_ITEMS_EXTRA
    lo = jnp.arange(n_blocks, dtype=jnp.int32)[:, None] * ROW_BLK
    overlap = jnp.minimum(ends[None, :], lo + ROW_BLK) - jnp.maximum(starts[None, :], lo)
    live = (overlap > 0).reshape(-1)
    n_live = jnp.sum(live.astype(jnp.int32))
    (pos,) = jnp.nonzero(live, size=n_items, fill_value=0)
    pos = pos.astype(jnp.int32)
    k = jnp.arange(n_items, dtype=jnp.int32)
    valid = k < n_live
    pos = jnp.where(valid, pos, pos[jnp.maximum(n_live - 1, 0)])
    blk = pos // N_EXPERTS
    exp = pos % N_EXPERTS
    prev_exp = jnp.concatenate([jnp.full((1,), -1, jnp.int32), exp[:-1]])
    flags = (jnp.where(valid, FLAG_VALID, 0)
             | jnp.where(valid & (exp != prev_exp), FLAG_NEW_EXPERT, 0)).astype(jnp.int32)
    starts33 = jnp.concatenate([starts, ends[-1:]]).astype(jnp.int32)
    return blk, exp, flags, starts33


def _combine_kernel(slot_cur_ref, slot_next_ref, ys_ref, x1_ref, gate_ref, p_ref, gpg_ref,
                    wpg_ref, wpp_ref, gpp_ref, out_ref, rows_ref, sem):
    i = pl.program_id(0)
    last = pl.num_programs(0) - 1
    cur = i % 2
    nxt = 1 - cur

    def start(slot_ref, buf):
        for k in range(TOP_K):
            _start_row_gather(slot_ref, k * TOK_TILE, ys_ref, rows_ref.at[buf, k], sem.at[buf])

    def wait(buf):
        for k in range(TOP_K):
            _wait_row_gather(ys_ref, rows_ref.at[buf, k], sem.at[buf])

    @pl.when(i == 0)
    def _():
        start(slot_cur_ref, 0)

    wait(cur)
    x2 = x1_ref[...]
    gate = jnp.concatenate([gate_ref[...], jnp.zeros((LANES - SUBLANES, TOK_TILE), F32)],
                           axis=0).T
    for k in range(TOP_K):
        x2 = x2 + gate[:, k:k + 1] * rows_ref[cur, k].reshape(TOK_TILE, D_MODEL)
    start(slot_next_ref, nxt)
    ple = _rms(jnp.dot(p_ref[...].astype(BF16), wpp_ref[...], preferred_element_type=F32),
               gpp_ref[...])
    pg = jax.nn.sigmoid(jnp.dot(_rms(x2, gpg_ref[...]).astype(BF16), wpg_ref[...],
                                preferred_element_type=F32))
    out_ref[...] = x2 + pg * ple

    @pl.when(i == last)
    def _():
        wait(nxt)


def _combine(slots_flat, ys3, x1, gate, p2d, gpg, wpg, wpp, gpp):
    n = x1.shape[0]
    n_tiles = n // TOK_TILE
    full = lambda shape: pl.BlockSpec(shape, lambda i: (0,) * len(shape))
    row = lambda w: pl.BlockSpec((TOK_TILE, w), lambda i: (i, 0))
    return pl.pallas_call(
        _combine_kernel,
        grid=(n_tiles,),
        in_specs=[
            pl.BlockSpec((TOP_K * TOK_TILE,), lambda i: (i,), memory_space=pltpu.SMEM),
            pl.BlockSpec((TOP_K * TOK_TILE,), lambda i: (jnp.minimum(i + 1, n_tiles - 1),),
                         memory_space=pltpu.SMEM),
            pl.BlockSpec(memory_space=pl.ANY),
            row(D_MODEL), pl.BlockSpec((SUBLANES, TOK_TILE), lambda i: (0, i)), row(PLE_DIM),
            full((1, D_MODEL)), full((D_MODEL, D_MODEL)), full((PLE_DIM, D_MODEL)),
            full((1, D_MODEL)),
        ],
        out_specs=row(D_MODEL),
        out_shape=jax.ShapeDtypeStruct((n, D_MODEL), F32),
        scratch_shapes=[pltpu.VMEM((2, TOP_K, TILE_GROUPS, SUBLANES, D_MODEL), F32),
                        pltpu.SemaphoreType.DMA((2,))],
        compiler_params=_cparams(1),
        name="combine",
    )(slots_flat, slots_flat, ys3, x1, gate, p2d, gpg, wpg, wpp, gpp)


def kernel(x, p, g_mix, w_in, g_q, g_k, w_attn_branch, a_re, a_im, log_dt, b_re, b_im, c_re, c_im, d_skip, w_glu, b_glu, w_out, g_ffn, w_router, b_router, w_up, b_up, w_down, b_down, g_ple_gate, w_ple_gate, w_ple_proj, g_ple_post):
    bsz, s_len, d = x.shape
    depth = w_in.shape[0]
    n = bsz * s_len
    assert d == D_MODEL and s_len % (MERGE_SUBTILES * TOK_TILE) == 0 and s_len % SCAN_CHUNK == 0
    assert bsz == SUBLANES, "the S5 scan keeps the batch on the sublane axis"

    x2d = x.reshape(n, d)
    for i in range(depth):
        qkv, u_tm, gates = _inproj(x2d, g_mix[i][None], w_in[i].astype(BF16), bsz, s_len)
        o = _attn(qkv.reshape(bsz, s_len, 3 * SB_WIDTH),
                  jnp.tile(g_q[i], 2)[None], jnp.tile(g_k[i], 2)[None])
        a_bc, b_cat, c_cat = _s5_params(a_re[i], a_im[i], log_dt[i], b_re[i], b_im[i],
                                        c_re[i], c_im[i], bsz)
        y_tm = _s5(u_tm.reshape(s_len, bsz, SSM_WIDTH), a_bc, b_cat, c_cat, d_skip[i][None])

        wr_t = w_router[i].T
        wr_hi = wr_t.astype(BF16)
        wr_lo = (wr_t - wr_hi.astype(F32)).astype(BF16)
        x1, xn, idx, gate, rank, counts = _merge(
            x2d, o.reshape(n, SB_WIDTH), y_tm.reshape(s_len, bsz * SSM_WIDTH), gates,
            w_attn_branch[i].astype(BF16), w_glu[i].astype(BF16), b_glu[i][None],
            w_out[i].astype(BF16), g_ffn[i][None], wr_hi, wr_lo, b_router[i][:, None], s_len)

        counts_i = counts[:, 0].astype(jnp.int32)
        blk, exp, flags, starts33 = _build_items(counts_i, n * TOP_K)
        hot = idx[:TOP_K, :, None] == jnp.arange(N_EXPERTS, dtype=jnp.int32)
        slots = jnp.sum(jnp.where(hot, starts33[:N_EXPERTS], 0), axis=-1) + rank[:TOP_K]
        slots_flat = slots.reshape(TOP_K, n // TOK_TILE, TOK_TILE).transpose(1, 0, 2).reshape(-1)

        xs3 = _dispatch(slots_flat, xn.reshape(n // SUBLANES, SUBLANES, D_MODEL))
        ys = _experts(blk, exp, flags, starts33, xs3.reshape(n * TOP_K, D_MODEL),
                      w_up[i], b_up[i][:, None, :], w_down[i], b_down[i][:, None, :])
        ys3 = ys.reshape(n * TOP_K // SUBLANES, SUBLANES, D_MODEL)
        x2d = _combine(slots_flat, ys3, x1, gate, p[i].reshape(n, PLE_DIM),
                       g_ple_gate[i][None], w_ple_gate[i].astype(BF16),
                       w_ple_proj[i].astype(BF16), g_ple_post[i][None])
    return x2d.reshape(bsz, s_len, d)
```

```python
import jax
import jax.numpy as jnp
from jax import lax
from jax.experimental import pallas as pl
from jax.experimental.pallas import tpu as pltpu

F32 = jnp.float32
BF16 = jnp.bfloat16

D_MODEL = 1024
SB_HEADS = 8
SB_HEAD_DIM = 64
SB_WIDTH = SB_HEADS * SB_HEAD_DIM
SSM_GROUP = 16
SSM_WIDTH = 512
SSM_GROUPS = SSM_WIDTH // SSM_GROUP
SSM_STATE = 64
PLE_DIM = 256
N_EXPERTS = 32
TOP_K = 4
D_FF = D_MODEL
SWIGLU_LIMIT = 7.0
SWIGLU_ALPHA = 1.702
EPS = 1e-6

LANES = 128
SUBLANES = 8
VMEM_LIMIT = 56 * 1024 * 1024

TOK_TILE = 256
TILE_GROUPS = TOK_TILE // SUBLANES
ROW_TILES = D_MODEL // LANES
assert ROW_TILES == SUBLANES
MERGE_SUBTILES = 2
ATT_BLK = 128
DEAD_LOG_WEIGHT = -104.0
ATT_HEADS_PER_STEP = 8
SCAN_CHUNK = 64
SCAN_LANES = 512
HALF_U = SSM_WIDTH // 2
HALF_STATE = SSM_GROUPS // 2 * SSM_STATE
ROW_BLK = 256
FF_CHUNK = 256
N_ITEMS_EXTRA = N_EXPERTS - 1


def _cparams(n_axes):
    return pltpu.CompilerParams(
        dimension_semantics=("arbitrary",) * n_axes,
        vmem_limit_bytes=VMEM_LIMIT)


def _rms(x, g):
    ms = jnp.mean(x * x, axis=-1, keepdims=True)
    return x * lax.rsqrt(ms + EPS) * g


def _inproj_kernel(x_ref, g_ref, w_ref, qkv_ref, u_ref, gates_ref):
    h = _rms(x_ref[...], g_ref[...]).astype(BF16)
    n_qkv = 3 * SB_WIDTH
    n_u = n_qkv + SSM_WIDTH
    qkv_ref[...] = jnp.dot(h, w_ref[:, :n_qkv], preferred_element_type=F32)
    u_ref[...] = jnp.dot(h, w_ref[:, n_qkv:n_u], preferred_element_type=F32)
    gates_ref[...] = jnp.dot(h, w_ref[:, n_u:], preferred_element_type=F32)


def _inproj(x2d, g_mix, w_in_bf, bsz, s_len):
    n = x2d.shape[0]
    tiles_per_seq = s_len // TOK_TILE
    in_cols = w_in_bf.shape[1]
    return pl.pallas_call(
        _inproj_kernel,
        grid=(n // TOK_TILE,),
        in_specs=[
            pl.BlockSpec((TOK_TILE, D_MODEL), lambda i: (i, 0)),
            pl.BlockSpec((1, D_MODEL), lambda i: (0, 0)),
            pl.BlockSpec((D_MODEL, in_cols), lambda i: (0, 0)),
        ],
        out_specs=[
            pl.BlockSpec((TOK_TILE, 3 * SB_WIDTH), lambda i: (i, 0)),
            pl.BlockSpec((TOK_TILE, SSM_WIDTH),
                         lambda i: (i % tiles_per_seq, i // tiles_per_seq)),
            pl.BlockSpec((TOK_TILE, 2 * D_MODEL), lambda i: (i, 0)),
        ],
        out_shape=[
            jax.ShapeDtypeStruct((n, 3 * SB_WIDTH), F32),
            jax.ShapeDtypeStruct((s_len, bsz * SSM_WIDTH), F32),
            jax.ShapeDtypeStruct((n, 2 * D_MODEL), F32),
        ],
        compiler_params=_cparams(1),
        name="inproj",
    )(x2d, g_mix, w_in_bf)


def _attn_kernel(q_ref, k_ref, v_ref, gq_ref, gk_ref, o_ref,
                 qs_ref, ks_ref, vs_ref, w2_ref, c_ref, acc_ref):
    s_len = q_ref.shape[1]
    n_blk = s_len // ATT_BLK
    n_pairs = q_ref.shape[2] // LANES
    lane = lax.broadcasted_iota(jnp.int32, (1, LANES), 1)
    head0 = lane < SB_HEAD_DIM

    li = lax.broadcasted_iota(jnp.int32, (LANES, LANES), 0) // SB_HEAD_DIM
    lj = lax.broadcasted_iota(jnp.int32, (LANES, LANES), 1) // SB_HEAD_DIM
    head_mean = jnp.where(li == lj, 1.0 / SB_HEAD_DIM, 0.0).astype(BF16)

    def head_rms(t, g):
        sq = t * t
        hi = sq.astype(BF16)
        lo = (sq - hi.astype(F32)).astype(BF16)
        ms = (jnp.dot(hi, head_mean, preferred_element_type=F32)
              + jnp.dot(lo, head_mean, preferred_element_type=F32))
        return t * lax.rsqrt(ms + EPS) * g

    scale = SB_HEAD_DIM ** -0.5

    def prep(qi, _):
        rows = pl.ds(pl.multiple_of(qi * ATT_BLK, ATT_BLK), ATT_BLK)
        for p in range(n_pairs):
            cols = slice(p * LANES, (p + 1) * LANES)
            qn = head_rms(q_ref[0, rows, cols], gq_ref[...]) * scale
            qs_ref[p, qi, :ATT_BLK, :] = jnp.where(head0, qn, 0.0).astype(BF16)
            qs_ref[p, qi, ATT_BLK:, :] = jnp.where(head0, 0.0, qn).astype(BF16)
            ks_ref[p, rows, :] = head_rms(k_ref[0, rows, cols], gk_ref[...]).astype(BF16)
            vs_ref[p, rows, :] = v_ref[0, rows, cols].astype(BF16)
        return 0

    lax.fori_loop(0, n_blk, prep, 0)

    r = lax.broadcasted_iota(jnp.int32, (2 * ATT_BLK, 2 * ATT_BLK), 0)
    c = lax.broadcasted_iota(jnp.int32, (2 * ATT_BLK, 2 * ATT_BLK), 1)
    r = jnp.where(r >= ATT_BLK, r - ATT_BLK, r)
    w2_ref[...] = jnp.where((c >= ATT_BLK) | (r > c), 1.0, 0.0).astype(BF16)

    ti = lax.broadcasted_iota(jnp.int32, (2 * ATT_BLK, ATT_BLK), 0)
    si = lax.broadcasted_iota(jnp.int32, (2 * ATT_BLK, ATT_BLK), 1)
    causal = si < jnp.where(ti >= ATT_BLK, ti - ATT_BLK, ti)

    def tiles(qi, kv_rows, diag):
        pairs = range(n_pairs)
        z = [lax.dot_general(qs_ref[p, qi], ks_ref[p, kv_rows, :], (((1,), (1,)), ((), ())),
                             preferred_element_type=F32) for p in pairs]
        log_beta, stacked = [], []
        for p in pairs:
            sp = jnp.maximum(z[p], 0.0) + jnp.log(1.0 + jnp.exp(-jnp.abs(z[p])))
            log_keep = -sp
            log_beta.append(z[p] - sp)
            if diag:
                log_keep = jnp.where(causal, log_keep, 0.0)
            hi = log_keep.astype(BF16)
            lo = (log_keep - hi.astype(F32)).astype(BF16)
            stacked.append(jnp.concatenate([hi, lo], axis=1))
        sums = [jnp.dot(stacked[p], w2_ref[...], preferred_element_type=F32) for p in pairs]
        w = []
        for p in pairs:
            wp = jnp.exp(log_beta[p] + sums[p][:, :ATT_BLK] + c_ref[p])
            if diag:
                wp = jnp.where(causal, wp, 0.0)
            w.append(wp.astype(BF16))
            c_ref[p] += sums[p][:, ATT_BLK:]
        pv = [jnp.dot(w[p], vs_ref[p, kv_rows, :], preferred_element_type=F32) for p in pairs]
        for p in pairs:
            acc_ref[p] += pv[p]

    def qblock(qi, _):
        q_rows = pl.ds(pl.multiple_of(qi * ATT_BLK, ATT_BLK), ATT_BLK)
        c_ref[...] = jnp.zeros_like(c_ref)
        acc_ref[...] = jnp.zeros_like(acc_ref)
        tiles(qi, q_rows, True)

        def live(carry):
            jj, c_max = carry
            return (jj < qi) & (c_max > DEAD_LOG_WEIGHT)

        def kvblock(carry):
            jj, _ = carry
            j = qi - 1 - jj
            kv_rows = pl.ds(pl.multiple_of(j * ATT_BLK, ATT_BLK), ATT_BLK)
            tiles(qi, kv_rows, False)
            return jj + 1, jnp.max(c_ref[...])

        lax.while_loop(live, kvblock, (jnp.int32(0), jnp.max(c_ref[...])))
        for p in range(n_pairs):
            o_ref[0, q_rows, p * LANES:(p + 1) * LANES] = jnp.where(
                head0, acc_ref[p, :ATT_BLK, :], acc_ref[p, ATT_BLK:, :])
        return 0

    lax.fori_loop(0, n_blk, qblock, 0)


def _attn(qkv3, gq2, gk2):
    bsz, s_len, _ = qkv3.shape
    n_pairs = ATT_HEADS_PER_STEP // 2
    n_steps = SB_HEADS // ATT_HEADS_PER_STEP
    blk = (1, s_len, n_pairs * LANES)
    return pl.pallas_call(
        _attn_kernel,
        grid=(bsz, n_steps),
        in_specs=[
            pl.BlockSpec(blk, lambda b, p: (b, 0, p)),
            pl.BlockSpec(blk, lambda b, p: (b, 0, n_steps + p)),
            pl.BlockSpec(blk, lambda b, p: (b, 0, 2 * n_steps + p)),
            pl.BlockSpec((1, LANES), lambda b, p: (0, 0)),
            pl.BlockSpec((1, LANES), lambda b, p: (0, 0)),
        ],
        out_specs=pl.BlockSpec(blk, lambda b, p: (b, 0, p)),
        out_shape=jax.ShapeDtypeStruct((bsz, s_len, SB_WIDTH), F32),
        scratch_shapes=[
            pltpu.VMEM((n_pairs, s_len // ATT_BLK, 2 * ATT_BLK, LANES), BF16),
            pltpu.VMEM((n_pairs, s_len, LANES), BF16),
            pltpu.VMEM((n_pairs, s_len, LANES), BF16),
            pltpu.VMEM((2 * ATT_BLK, 2 * ATT_BLK), BF16),
            pltpu.VMEM((n_pairs, 2 * ATT_BLK, ATT_BLK), F32),
            pltpu.VMEM((n_pairs, 2 * ATT_BLK, LANES), F32),
        ],
        compiler_params=_cparams(2),
        name="attn",
    )(qkv3, qkv3, qkv3, gq2, gk2)


def _s5_kernel(u_ref, a_ref, b_ref, c_ref, d_ref, y_ref, hbuf_ref, state_ref):
    tc, bsz, _ = u_ref.shape
    rows = tc * bsz

    @pl.when(pl.program_id(0) == 0)
    def _():
        state_ref[...] = jnp.zeros_like(state_ref)

    u2 = u_ref[...].reshape(rows, SSM_WIDTH)
    ub = u2.astype(BF16)
    for hf in range(2):
        uh = ub[:, hf * HALF_U:(hf + 1) * HALF_U]
        xh = jnp.dot(uh, b_ref[hf], preferred_element_type=F32)
        hbuf_ref[...] = xh.reshape(tc, bsz, 2 * HALF_STATE)
        for lc in range(HALF_STATE // SCAN_LANES):
            re = pl.ds(lc * SCAN_LANES, SCAN_LANES)
            im = pl.ds(HALF_STATE + lc * SCAN_LANES, SCAN_LANES)
            ar = a_ref[hf, :, re]
            ai = a_ref[hf, :, im]

            def step(t, carry, re=re, im=im, ar=ar, ai=ai):
                hr, hi = carry
                nr = ar * hr - ai * hi + hbuf_ref[t, :, re]
                ni = ar * hi + ai * hr + hbuf_ref[t, :, im]
                hbuf_ref[t, :, re] = nr
                hbuf_ref[t, :, im] = ni
                return nr, ni

            hr, hi = lax.fori_loop(0, tc, step,
                                   (state_ref[hf, :, re], state_ref[hf, :, im]),
                                   unroll=8)
            state_ref[hf, :, re] = hr
            state_ref[hf, :, im] = hi
        hb = hbuf_ref[...].reshape(rows, 2 * HALF_STATE).astype(BF16)
        yh = jnp.dot(hb, c_ref[hf], preferred_element_type=F32)
        cols = slice(hf * HALF_U, (hf + 1) * HALF_U)
        yh = yh + d_ref[:, cols] * u2[:, cols]
        y_ref[:, :, cols] = yh.reshape(tc, bsz, HALF_U)


def _s5(u_tm3, a_bc, b_cat, c_cat, d_skip):
    s_len, bsz, _ = u_tm3.shape
    return pl.pallas_call(
        _s5_kernel,
        grid=(s_len // SCAN_CHUNK,),
        in_specs=[
            pl.BlockSpec((SCAN_CHUNK, bsz, SSM_WIDTH), lambda c: (c, 0, 0)),
            pl.BlockSpec((2, bsz, 2 * HALF_STATE), lambda c: (0, 0, 0)),
            pl.BlockSpec((2, HALF_U, 2 * HALF_STATE), lambda c: (0, 0, 0)),
            pl.BlockSpec((2, 2 * HALF_STATE, HALF_U), lambda c: (0, 0, 0)),
            pl.BlockSpec((1, SSM_WIDTH), lambda c: (0, 0)),
        ],
        out_specs=pl.BlockSpec((SCAN_CHUNK, bsz, SSM_WIDTH), lambda c: (c, 0, 0)),
        out_shape=jax.ShapeDtypeStruct((s_len, bsz, SSM_WIDTH), F32),
        scratch_shapes=[
            pltpu.VMEM((SCAN_CHUNK, bsz, 2 * HALF_STATE), F32),
            pltpu.VMEM((2, bsz, 2 * HALF_STATE), F32),
        ],
        compiler_params=_cparams(1),
        name="s5",
    )(u_tm3, a_bc, b_cat, c_cat, d_skip)


def _s5_params(a_re, a_im, log_dt, b_re, b_im, c_re, c_im, bsz):
    dt = jnp.exp(log_dt)[:, None]
    mag = jnp.exp(a_re * dt)
    abar_r = mag * jnp.cos(a_im * dt)
    abar_i = mag * jnp.sin(a_im * dt)
    den = a_re * a_re + a_im * a_im
    nr = abar_r - 1.0
    ni = abar_i
    fr = (nr * a_re + ni * a_im) / den
    fi = (ni * a_re - nr * a_im) / den
    bbar_r = fr[..., None] * b_re - fi[..., None] * b_im
    bbar_i = fr[..., None] * b_im + fi[..., None] * b_re
    gh = SSM_GROUPS // 2
    eye = jnp.eye(gh, dtype=F32)

    def a_half(hf):
        sl = slice(hf * gh, (hf + 1) * gh)
        row = jnp.concatenate([abar_r[sl].reshape(-1), abar_i[sl].reshape(-1)])
        return jnp.broadcast_to(row[None], (bsz, 2 * HALF_STATE))

    def b_half(bb, hf):
        blk = bb[hf * gh:(hf + 1) * gh]
        return jnp.einsum('gpc,gk->gckp', blk, eye).reshape(HALF_U, HALF_STATE)

    def c_half(cc, hf):
        blk = cc[hf * gh:(hf + 1) * gh]
        return jnp.einsum('gcp,gk->gpkc', blk, eye).reshape(HALF_STATE, HALF_U)

    a_bc = jnp.stack([a_half(0), a_half(1)])
    b_cat = jnp.stack([jnp.concatenate([b_half(bbar_r, hf), b_half(bbar_i, hf)], axis=1)
                       for hf in range(2)]).astype(BF16)
    c_cat = jnp.stack([jnp.concatenate([c_half(c_re, hf), -c_half(c_im, hf)], axis=0)
                       for hf in range(2)]).astype(BF16)
    return a_bc, b_cat, c_cat


def _merge_kernel(x_ref, o_ref, y_ref, gates_ref, wab_ref, wglu_ref, bglu_ref, wout_ref,
                  gffn_ref, wrh_ref, wrl_ref, br_ref,
                  x1_ref, xn_ref, idx_ref, gate_ref, rank_ref, cnt_ref, carry_ref):
    @pl.when(pl.program_id(0) == 0)
    def _():
        carry_ref[...] = jnp.zeros_like(carry_ref)

    subs = [pl.ds(s * TOK_TILE, TOK_TILE) for s in range(MERGE_SUBTILES)]
    nt = (((1,), (1,)), ((), ()))

    attn_branch = [jnp.dot(o_ref[s, :].astype(BF16), wab_ref[...], preferred_element_type=F32)
                   for s in subs]
    zg = [jnp.dot(jax.nn.gelu(y_ref[s, :]).astype(BF16), wglu_ref[...],
                  preferred_element_type=F32) + bglu_ref[...] for s in subs]
    mixed = []
    for i, s in enumerate(subs):
        ssm_branch = zg[i][:, :D_MODEL] * jax.nn.sigmoid(zg[i][:, D_MODEL:])
        mixed.append((jax.nn.sigmoid(gates_ref[s, :D_MODEL]) * attn_branch[i]
                      + jax.nn.sigmoid(gates_ref[s, D_MODEL:]) * ssm_branch).astype(BF16))
    x1 = [x_ref[s, :] + jnp.dot(mixed[i], wout_ref[...], preferred_element_type=F32)
          for i, s in enumerate(subs)]
    xh, xl = [], []
    for i, s in enumerate(subs):
        x1_ref[s, :] = x1[i]
        xn = _rms(x1[i], gffn_ref[...])
        for c in range(ROW_TILES):
            xn_ref[pl.ds(i * TOK_TILE * ROW_TILES + c, TOK_TILE, stride=ROW_TILES), :] = (
                xn[:, c * LANES:(c + 1) * LANES])
        xh.append(xn.astype(BF16))
        xl.append((xn - xh[i].astype(F32)).astype(BF16))
    logits = [(lax.dot_general(wrh_ref[...], xh[i], nt, preferred_element_type=F32)
               + lax.dot_general(wrh_ref[...], xl[i], nt, preferred_element_type=F32)
               + lax.dot_general(wrl_ref[...], xh[i], nt, preferred_element_type=F32))
              + br_ref[...] for i in range(MERGE_SUBTILES)]

    e_iota = lax.broadcasted_iota(jnp.int32, (N_EXPERTS, TOK_TILE), 0).astype(F32)
    k_iota = lax.broadcasted_iota(jnp.int32, (SUBLANES, TOK_TILE), 0)
    rr = lax.broadcasted_iota(jnp.int32, (TOK_TILE, TOK_TILE), 0)
    cc = lax.broadcasted_iota(jnp.int32, (TOK_TILE, TOK_TILE), 1)
    earlier = jnp.where(rr < cc, 1.0, 0.0).astype(BF16)

    for i, s in enumerate(subs):
        work = logits[i]
        hits, vals = [], []
        sel = jnp.zeros((N_EXPERTS, TOK_TILE), F32)
        for _ in range(TOP_K):
            m = jnp.max(work, axis=0, keepdims=True)
            pick = jnp.min(jnp.where(work == m, e_iota, float(N_EXPERTS)), axis=0, keepdims=True)
            hit = e_iota == pick
            work = jnp.where(hit, -jnp.inf, work)
            sel = sel + jnp.where(hit, 1.0, 0.0)
            hits.append((hit, pick))
            vals.append(m)
        before = jnp.dot(sel.astype(BF16), earlier, preferred_element_type=F32) + carry_ref[...]
        exps = [jnp.exp(v - vals[0]) for v in vals]
        denom = exps[0] + exps[1] + exps[2] + exps[3]
        idx = jnp.zeros((SUBLANES, TOK_TILE), jnp.int32)
        gate = jnp.zeros((SUBLANES, TOK_TILE), F32)
        rank = jnp.zeros((SUBLANES, TOK_TILE), jnp.int32)
        for k in range(TOP_K):
            hit, pick = hits[k]
            rk = jnp.sum(jnp.where(hit, before, 0.0), axis=0, keepdims=True)
            idx = jnp.where(k_iota == k, pick.astype(jnp.int32), idx)
            gate = jnp.where(k_iota == k, exps[k] / denom, gate)
            rank = jnp.where(k_iota == k, rk.astype(jnp.int32), rank)
        idx_ref[:, s] = idx
        gate_ref[:, s] = gate
        rank_ref[:, s] = rank
        carry_ref[...] += jnp.sum(sel, axis=1, keepdims=True)
    cnt_ref[...] = jnp.broadcast_to(carry_ref[...], cnt_ref.shape)


def _merge(x2d, o2d, y_tm2, gates, wab, wglu, bglu, wout, gffn, wrh_t, wrl_t, br_col, s_len):
    n = x2d.shape[0]
    tile = MERGE_SUBTILES * TOK_TILE
    tiles_per_seq = s_len // tile
    full = lambda shape: pl.BlockSpec(shape, lambda i: (0,) * len(shape))
    row = lambda w: pl.BlockSpec((tile, w), lambda i: (i, 0))
    col = pl.BlockSpec((SUBLANES, tile), lambda i: (0, i))
    return pl.pallas_call(
        _merge_kernel,
        grid=(n // tile,),
        in_specs=[
            row(D_MODEL), row(SB_WIDTH),
            pl.BlockSpec((tile, SSM_WIDTH),
                         lambda i: (i % tiles_per_seq, i // tiles_per_seq)),
            row(2 * D_MODEL),
            full((SB_WIDTH, D_MODEL)), full((SSM_WIDTH, 2 * D_MODEL)), full((1, 2 * D_MODEL)),
            full((D_MODEL, D_MODEL)), full((1, D_MODEL)),
            full((N_EXPERTS, D_MODEL)), full((N_EXPERTS, D_MODEL)), full((N_EXPERTS, 1)),
        ],
        out_specs=[row(D_MODEL), pl.BlockSpec((tile * ROW_TILES, LANES), lambda i: (i, 0)),
                   col, col, col, full((N_EXPERTS, LANES))],
        out_shape=[
            jax.ShapeDtypeStruct((n, D_MODEL), F32),
            jax.ShapeDtypeStruct((n * ROW_TILES, LANES), F32),
            jax.ShapeDtypeStruct((SUBLANES, n), jnp.int32),
            jax.ShapeDtypeStruct((SUBLANES, n), F32),
            jax.ShapeDtypeStruct((SUBLANES, n), jnp.int32),
            jax.ShapeDtypeStruct((N_EXPERTS, LANES), F32),
        ],
        scratch_shapes=[pltpu.VMEM((N_EXPERTS, 1), F32)],
        compiler_params=_cparams(1),
        name="merge",
    )(x2d, o2d, y_tm2, gates, wab, wglu, bglu, wout, gffn, wrh_t, wrl_t, br_col)


def _lane_tile(ref2d, c, n_rows):
    return ref2d.at[pl.ds(c, n_rows, stride=ROW_TILES), :]


def _load_row_tiled(ref2d, n_rows):
    return jnp.concatenate([_lane_tile(ref2d, c, n_rows)[...] for c in range(ROW_TILES)], axis=1)


def _dispatch_kernel(slot_ref, xn_ref, xs_ref, sem):
    def issue(g, _):
        for k in range(TOP_K):
            for j in range(SUBLANES):
                t = g * SUBLANES + j
                pltpu.make_async_copy(xn_ref.at[t], xs_ref.at[slot_ref[k * TOK_TILE + t]],
                                      sem).start(priority=j % 2)
        return 0

    lax.fori_loop(0, TILE_GROUPS, issue, 0)
    for _ in range(TOP_K):
        pltpu.make_async_copy(xn_ref, xs_ref.at[pl.ds(0, TOK_TILE)], sem).wait()


def _dispatch(slots_flat, xn3):
    n = xn3.shape[0]
    return pl.pallas_call(
        _dispatch_kernel,
        grid=(n // TOK_TILE,),
        in_specs=[
            pl.BlockSpec((TOP_K * TOK_TILE,), lambda i: (i,), memory_space=pltpu.SMEM),
            pl.BlockSpec((TOK_TILE, ROW_TILES, LANES), lambda i: (i, 0, 0)),
        ],
        out_specs=pl.BlockSpec(memory_space=pl.ANY),
        out_shape=jax.ShapeDtypeStruct((n * TOP_K, ROW_TILES, LANES), F32),
        scratch_shapes=[pltpu.SemaphoreType.DMA],
        compiler_params=_cparams(1),
        name="dispatch",
    )(slots_flat, xn3)


FLAG_VALID, FLAG_FIRST_VISIT, FLAG_NEW_EXPERT = 1, 2, 4


def _experts_kernel(blk_ref, exp_ref, wexp_ref, flag_ref, start_ref,
                    xs_ref, wup_ref, bup_ref, wdn_ref, bdn_ref, ys_ref,
                    wup_bf_ref, wdn_bf_ref):
    del wexp_ref
    i = pl.program_id(0)
    flags = flag_ref[i]

    @pl.when((flags & FLAG_NEW_EXPERT) != 0)
    def _():
        wup_bf_ref[...] = wup_ref[0].astype(BF16)
        wdn_bf_ref[...] = wdn_ref[0].astype(BF16)

    @pl.when((flags & FLAG_VALID) != 0)
    def _():
        e = exp_ref[i]
        x = _load_row_tiled(xs_ref, ROW_BLK).astype(BF16)
        acts = []
        for c in range(D_FF // FF_CHUNK):
            gcols = pl.ds(c * FF_CHUNK, FF_CHUNK)
            ucols = pl.ds(D_FF + c * FF_CHUNK, FF_CHUNK)
            g = jnp.dot(x, wup_bf_ref[:, gcols], preferred_element_type=F32) + bup_ref[0, :, gcols]
            up = jnp.dot(x, wup_bf_ref[:, ucols], preferred_element_type=F32) + bup_ref[0, :, ucols]
            g = jnp.minimum(g, SWIGLU_LIMIT)
            up = jnp.clip(up, -SWIGLU_LIMIT, SWIGLU_LIMIT)
            acts.append(((up + 1.0) * g * jax.nn.sigmoid(SWIGLU_ALPHA * g)).astype(BF16))
        y = jnp.dot(jnp.concatenate(acts, axis=1), wdn_bf_ref[...],
                    preferred_element_type=F32) + bdn_ref[0]
        row = blk_ref[i] * ROW_BLK + lax.broadcasted_iota(jnp.int32, (ROW_BLK, 1), 0)
        mine_or_later = jnp.broadcast_to(row >= start_ref[e], (ROW_BLK, LANES))

        @pl.when((flags & FLAG_FIRST_VISIT) != 0)
        def _():
            for c in range(ROW_TILES):
                _lane_tile(ys_ref, c, ROW_BLK)[...] = y[:, c * LANES:(c + 1) * LANES]

        @pl.when((flags & FLAG_FIRST_VISIT) == 0)
        def _():
            for c in range(ROW_TILES):
                tile = _lane_tile(ys_ref, c, ROW_BLK)
                tile[...] = jnp.where(mine_or_later, y[:, c * LANES:(c + 1) * LANES], tile[...])


def _experts(item_blk, item_exp, item_wexp, item_flag, starts, xs2d, w_up, b_up3, w_down, b_down3):
    n_items = item_blk.shape[0]
    rows_blk = pl.BlockSpec((ROW_BLK * ROW_TILES, LANES), lambda i, b, e, w, f, s: (b[i], 0))
    grid_spec = pltpu.PrefetchScalarGridSpec(
        num_scalar_prefetch=5,
        grid=(n_items,),
        in_specs=[
            rows_blk,
            pl.BlockSpec((1, D_MODEL, 2 * D_FF), lambda i, b, e, w, f, s: (w[i], 0, 0)),
            pl.BlockSpec((1, 1, 2 * D_FF), lambda i, b, e, w, f, s: (e[i], 0, 0)),
            pl.BlockSpec((1, D_FF, D_MODEL), lambda i, b, e, w, f, s: (w[i], 0, 0)),
            pl.BlockSpec((1, 1, D_MODEL), lambda i, b, e, w, f, s: (e[i], 0, 0)),
        ],
        out_specs=rows_blk,
        scratch_shapes=[
            pltpu.VMEM((D_MODEL, 2 * D_FF), BF16),
            pltpu.VMEM((D_FF, D_MODEL), BF16),
        ],
    )
    return pl.pallas_call(
        _experts_kernel,
        grid_spec=grid_spec,
        out_shape=jax.ShapeDtypeStruct(xs2d.shape, F32),
        compiler_params=_cparams(1),
        name="experts",
    )(item_blk, item_exp, item_wexp, item_flag, starts, xs2d, w_up, b_up3, w_down, b_down3)


def _build_items(counts, n_rows):
    ends = jnp.cumsum(counts)
    starts = ends - counts
    n_blocks = n_rows // ROW_BLK
    n_items = n_blocks + N_ITEMS_EXTRA
    lo = jnp.arange(n_blocks, dtype=jnp.int32)[:, None] * ROW_BLK
    overlap = jnp.minimum(ends[None, :], lo + ROW_BLK) - jnp.maximum(starts[None, :], lo)
    live = (overlap > 0).reshape(-1)
    n_live = jnp.sum(live.astype(jnp.int32))
    (pos,) = jnp.nonzero(live, size=n_items, fill_value=0)
    pos = pos.astype(jnp.int32)
    k = jnp.arange(n_items, dtype=jnp.int32)
    valid = k < n_live
    pos = jnp.where(valid, pos, pos[jnp.maximum(n_live - 1, 0)])
    blk = pos // N_EXPERTS
    exp = pos % N_EXPERTS
    prev_blk = jnp.concatenate([jnp.full((1,), -1, jnp.int32), blk[:-1]])
    prev_exp = jnp.concatenate([jnp.full((1,), -1, jnp.int32), exp[:-1]])
    new_exp = valid & (exp != prev_exp)
    flags = (jnp.where(valid, FLAG_VALID, 0)
             | jnp.where(valid & (blk != prev_blk), FLAG_FIRST_VISIT, 0)
             | jnp.where(new_exp, FLAG_NEW_EXPERT, 0)).astype(jnp.int32)
    first_at = jnp.where(new_exp, k, n_items)
    next_first = jnp.flip(lax.cummin(jnp.flip(first_at)))
    next_first = jnp.concatenate([next_first[1:], jnp.full((1,), n_items, jnp.int32)])
    wexp = jnp.where(new_exp | (next_first >= n_items), exp,
                     exp[jnp.minimum(next_first, n_items - 1)])
    starts33 = jnp.concatenate([starts, ends[-1:]]).astype(jnp.int32)
    return blk, exp, wexp, flags, starts33


def _combine_kernel(slot_cur_ref, slot_next_ref, ys3_ref, ys2d_ref, x1_ref, gate_ref, p_ref,
                    gpg_ref, wpg_ref, wpp_ref, gpp_ref, out_ref, rows_ref, sem):
    i = pl.program_id(0)
    last = pl.num_programs(0) - 1
    cur = i % 2
    nxt = 1 - cur

    def start(slot_ref, buf):
        for k in range(TOP_K):
            for t in range(TOK_TILE):
                pltpu.make_async_copy(ys3_ref.at[slot_ref[k * TOK_TILE + t]],
                                      rows_ref.at[buf, k, pl.ds(t * ROW_TILES, ROW_TILES)],
                                      sem.at[buf]).start(priority=t % 2)

    def wait(buf):
        for k in range(TOP_K):
            pltpu.make_async_copy(ys2d_ref.at[pl.ds(0, TOK_TILE * ROW_TILES)],
                                  rows_ref.at[buf, k], sem.at[buf]).wait()

    @pl.when(i == 0)
    def _():
        start(slot_cur_ref, 0)

    wait(cur)
    gate = jnp.concatenate([gate_ref[...], jnp.zeros((LANES - SUBLANES, TOK_TILE), F32)],
                           axis=0).T
    moe = []
    for c in range(ROW_TILES):
        acc = gate[:, 0:1] * _lane_tile(rows_ref.at[cur, 0], c, TOK_TILE)[...]
        for k in range(1, TOP_K):
            acc = acc + gate[:, k:k + 1] * _lane_tile(rows_ref.at[cur, k], c, TOK_TILE)[...]
        moe.append(acc)
    x2 = x1_ref[...] + jnp.concatenate(moe, axis=1)
    start(slot_next_ref, nxt)
    ple = _rms(jnp.dot(p_ref[...].astype(BF16), wpp_ref[...], preferred_element_type=F32),
               gpp_ref[...])
    pg = jax.nn.sigmoid(jnp.dot(_rms(x2, gpg_ref[...]).astype(BF16), wpg_ref[...],
                                preferred_element_type=F32))
    out_ref[...] = x2 + pg * ple

    @pl.when(i == last)
    def _():
        wait(nxt)


def _combine(slots_flat, ys2d, x1, gate, p2d, gpg, wpg, wpp, gpp):
    n = x1.shape[0]
    n_tiles = n // TOK_TILE
    full = lambda shape: pl.BlockSpec(shape, lambda i: (0,) * len(shape))
    row = lambda w: pl.BlockSpec((TOK_TILE, w), lambda i: (i, 0))
    return pl.pallas_call(
        _combine_kernel,
        grid=(n_tiles,),
        in_specs=[
            pl.BlockSpec((TOP_K * TOK_TILE,), lambda i: (i,), memory_space=pltpu.SMEM),
            pl.BlockSpec((TOP_K * TOK_TILE,), lambda i: (jnp.minimum(i + 1, n_tiles - 1),),
                         memory_space=pltpu.SMEM),
            pl.BlockSpec(memory_space=pl.ANY), pl.BlockSpec(memory_space=pl.ANY),
            row(D_MODEL), pl.BlockSpec((SUBLANES, TOK_TILE), lambda i: (0, i)), row(PLE_DIM),
            full((1, D_MODEL)), full((D_MODEL, D_MODEL)), full((PLE_DIM, D_MODEL)),
            full((1, D_MODEL)),
        ],
        out_specs=row(D_MODEL),
        out_shape=jax.ShapeDtypeStruct((n, D_MODEL), F32),
        scratch_shapes=[pltpu.VMEM((2, TOP_K, TOK_TILE * ROW_TILES, LANES), F32),
                        pltpu.SemaphoreType.DMA((2,))],
        compiler_params=_cparams(1),
        name="combine",
    )(slots_flat, slots_flat, ys2d.reshape(-1, ROW_TILES, LANES), ys2d, x1, gate, p2d,
      gpg, wpg, wpp, gpp)


def kernel(x, p, g_mix, w_in, g_q, g_k, w_attn_branch, a_re, a_im, log_dt, b_re, b_im, c_re, c_im, d_skip, w_glu, b_glu, w_out, g_ffn, w_router, b_router, w_up, b_up, w_down, b_down, g_ple_gate, w_ple_gate, w_ple_proj, g_ple_post):
    bsz, s_len, d = x.shape
    depth = w_in.shape[0]
    n = bsz * s_len
    assert d == D_MODEL and s_len % (MERGE_SUBTILES * TOK_TILE) == 0 and s_len % SCAN_CHUNK == 0
    assert bsz == SUBLANES, "the S5 scan keeps the batch on the sublane axis"

    x2d = x.reshape(n, d)
    for i in range(depth):
        qkv, u_tm, gates = _inproj(x2d, g_mix[i][None], w_in[i].astype(BF16), bsz, s_len)
        o = _attn(qkv.reshape(bsz, s_len, 3 * SB_WIDTH),
                  jnp.tile(g_q[i], 2)[None], jnp.tile(g_k[i], 2)[None])
        a_bc, b_cat, c_cat = _s5_params(a_re[i], a_im[i], log_dt[i], b_re[i], b_im[i],
                                        c_re[i], c_im[i], bsz)
        y_tm = _s5(u_tm.reshape(s_len, bsz, SSM_WIDTH), a_bc, b_cat, c_cat, d_skip[i][None])

        wr_t = w_router[i].T
        wr_hi = wr_t.astype(BF16)
        wr_lo = (wr_t - wr_hi.astype(F32)).astype(BF16)
        x1, xn, idx, gate, rank, counts = _merge(
            x2d, o.reshape(n, SB_WIDTH), y_tm.reshape(s_len, bsz * SSM_WIDTH), gates,
            w_attn_branch[i].astype(BF16), w_glu[i].astype(BF16), b_glu[i][None],
            w_out[i].astype(BF16), g_ffn[i][None], wr_hi, wr_lo, b_router[i][:, None], s_len)

        counts_i = counts[:, 0].astype(jnp.int32)
        blk, exp, wexp, flags, starts33 = _build_items(counts_i, n * TOP_K)
        hot = idx[:TOP_K, :, None] == jnp.arange(N_EXPERTS, dtype=jnp.int32)
        slots = jnp.sum(jnp.where(hot, starts33[:N_EXPERTS], 0), axis=-1) + rank[:TOP_K]
        slots_flat = slots.reshape(TOP_K, n // TOK_TILE, TOK_TILE).transpose(1, 0, 2).reshape(-1)

        xs3 = _dispatch(slots_flat, xn.reshape(n, ROW_TILES, LANES))
        ys2d = _experts(blk, exp, wexp, flags, starts33, xs3.reshape(n * TOP_K * ROW_TILES, LANES),
                        w_up[i], b_up[i][:, None, :], w_down[i], b_down[i][:, None, :])
        x2d = _combine(slots_flat, ys2d, x1, gate, p[i].reshape(n, PLE_DIM),
                       g_ple_gate[i][None], w_ple_gate[i].astype(BF16),
                       w_ple_proj[i].astype(BF16), g_ple_post[i][None])
    return x2d.reshape(bsz, s_len, d)
```

```python
import jax
import jax.numpy as jnp
from jax import lax
from jax.experimental import pallas as pl
from jax.experimental.pallas import tpu as pltpu

F32 = jnp.float32
BF16 = jnp.bfloat16

D_MODEL = 1024
SB_HEADS = 8
SB_HEAD_DIM = 64
SB_WIDTH = SB_HEADS * SB_HEAD_DIM
SSM_GROUP = 16
SSM_WIDTH = 512
SSM_GROUPS = SSM_WIDTH // SSM_GROUP
SSM_STATE = 64
PLE_DIM = 256
N_EXPERTS = 32
TOP_K = 4
D_FF = D_MODEL
SWIGLU_LIMIT = 7.0
SWIGLU_ALPHA = 1.702
EPS = 1e-6

LANES = 128
SUBLANES = 8
VMEM_LIMIT = 56 * 1024 * 1024

TOK_TILE = 256
TILE_GROUPS = TOK_TILE // SUBLANES
ROW_TILES = D_MODEL // LANES
assert ROW_TILES == SUBLANES
MERGE_SUBTILES = 2
ATT_BLK = 128
DEAD_LOG_WEIGHT = -104.0
ATT_HEADS_PER_STEP = 8
SCAN_CHUNK = 64
SCAN_LANES = 512
HALF_U = SSM_WIDTH // 2
HALF_STATE = SSM_GROUPS // 2 * SSM_STATE
ROW_BLK = 256
FF_CHUNK = 256
N_ITEMS_EXTRA = N_EXPERTS - 1


def _cparams(n_axes):
    return pltpu.CompilerParams(
        dimension_semantics=("arbitrary",) * n_axes,
        vmem_limit_bytes=VMEM_LIMIT)


def _rms(x, g):
    ms = jnp.mean(x * x, axis=-1, keepdims=True)
    return x * lax.rsqrt(ms + EPS) * g


def _inproj_kernel(x_ref, g_ref, w_ref, qkv_ref, u_ref, gates_ref):
    h = _rms(x_ref[...], g_ref[...]).astype(BF16)
    n_qkv = 3 * SB_WIDTH
    n_u = n_qkv + SSM_WIDTH
    qkv_ref[...] = jnp.dot(h, w_ref[:, :n_qkv], preferred_element_type=F32)
    u_ref[...] = jnp.dot(h, w_ref[:, n_qkv:n_u], preferred_element_type=F32)
    gates_ref[...] = jnp.dot(h, w_ref[:, n_u:], preferred_element_type=F32)


def _inproj(x2d, g_mix, w_in_bf, bsz, s_len):
    n = x2d.shape[0]
    tiles_per_seq = s_len // TOK_TILE
    in_cols = w_in_bf.shape[1]
    return pl.pallas_call(
        _inproj_kernel,
        grid=(n // TOK_TILE,),
        in_specs=[
            pl.BlockSpec((TOK_TILE, D_MODEL), lambda i: (i, 0)),
            pl.BlockSpec((1, D_MODEL), lambda i: (0, 0)),
            pl.BlockSpec((D_MODEL, in_cols), lambda i: (0, 0)),
        ],
        out_specs=[
            pl.BlockSpec((TOK_TILE, 3 * SB_WIDTH), lambda i: (i, 0)),
            pl.BlockSpec((TOK_TILE, SSM_WIDTH),
                         lambda i: (i % tiles_per_seq, i // tiles_per_seq)),
            pl.BlockSpec((TOK_TILE, 2 * D_MODEL), lambda i: (i, 0)),
        ],
        out_shape=[
            jax.ShapeDtypeStruct((n, 3 * SB_WIDTH), F32),
            jax.ShapeDtypeStruct((s_len, bsz * SSM_WIDTH), F32),
            jax.ShapeDtypeStruct((n, 2 * D_MODEL), F32),
        ],
        compiler_params=_cparams(1),
        name="inproj",
    )(x2d, g_mix, w_in_bf)


def _attn_kernel(q_ref, k_ref, v_ref, gq_ref, gk_ref, o_ref,
                 qs_ref, ks_ref, vs_ref, w2_ref, c_ref, acc_ref):
    s_len = q_ref.shape[1]
    n_blk = s_len // ATT_BLK
    n_pairs = q_ref.shape[2] // LANES
    lane = lax.broadcasted_iota(jnp.int32, (1, LANES), 1)
    head0 = lane < SB_HEAD_DIM

    li = lax.broadcasted_iota(jnp.int32, (LANES, LANES), 0) // SB_HEAD_DIM
    lj = lax.broadcasted_iota(jnp.int32, (LANES, LANES), 1) // SB_HEAD_DIM
    head_mean = jnp.where(li == lj, 1.0 / SB_HEAD_DIM, 0.0).astype(BF16)

    def head_rms(t, g):
        sq = t * t
        hi = sq.astype(BF16)
        lo = (sq - hi.astype(F32)).astype(BF16)
        ms = (jnp.dot(hi, head_mean, preferred_element_type=F32)
              + jnp.dot(lo, head_mean, preferred_element_type=F32))
        return t * lax.rsqrt(ms + EPS) * g

    scale = SB_HEAD_DIM ** -0.5

    def prep(qi, _):
        rows = pl.ds(pl.multiple_of(qi * ATT_BLK, ATT_BLK), ATT_BLK)
        for p in range(n_pairs):
            cols = slice(p * LANES, (p + 1) * LANES)
            qn = head_rms(q_ref[0, rows, cols], gq_ref[...]) * scale
            qs_ref[p, qi, :ATT_BLK, :] = jnp.where(head0, qn, 0.0).astype(BF16)
            qs_ref[p, qi, ATT_BLK:, :] = jnp.where(head0, 0.0, qn).astype(BF16)
            ks_ref[p, rows, :] = head_rms(k_ref[0, rows, cols], gk_ref[...]).astype(BF16)
            vs_ref[p, rows, :] = v_ref[0, rows, cols].astype(BF16)
        return 0

    lax.fori_loop(0, n_blk, prep, 0)

    r = lax.broadcasted_iota(jnp.int32, (2 * ATT_BLK, 2 * ATT_BLK), 0)
    c = lax.broadcasted_iota(jnp.int32, (2 * ATT_BLK, 2 * ATT_BLK), 1)
    r = jnp.where(r >= ATT_BLK, r - ATT_BLK, r)
    w2_ref[...] = jnp.where((c >= ATT_BLK) | (r > c), 1.0, 0.0).astype(BF16)

    ti = lax.broadcasted_iota(jnp.int32, (2 * ATT_BLK, ATT_BLK), 0)
    si = lax.broadcasted_iota(jnp.int32, (2 * ATT_BLK, ATT_BLK), 1)
    causal = si < jnp.where(ti >= ATT_BLK, ti - ATT_BLK, ti)

    def tiles(qi, kv_rows, diag):
        pairs = range(n_pairs)
        z = [lax.dot_general(qs_ref[p, qi], ks_ref[p, kv_rows, :], (((1,), (1,)), ((), ())),
                             preferred_element_type=F32) for p in pairs]
        log_beta, stacked = [], []
        for p in pairs:
            sp = jnp.maximum(z[p], 0.0) + jnp.log(1.0 + jnp.exp(-jnp.abs(z[p])))
            log_keep = -sp
            log_beta.append(z[p] - sp)
            if diag:
                log_keep = jnp.where(causal, log_keep, 0.0)
            hi = log_keep.astype(BF16)
            lo = (log_keep - hi.astype(F32)).astype(BF16)
            stacked.append(jnp.concatenate([hi, lo], axis=1))
        sums = [jnp.dot(stacked[p], w2_ref[...], preferred_element_type=F32) for p in pairs]
        w = []
        for p in pairs:
            wp = jnp.exp(log_beta[p] + sums[p][:, :ATT_BLK] + c_ref[p])
            if diag:
                wp = jnp.where(causal, wp, 0.0)
            w.append(wp.astype(BF16))
            c_ref[p] += sums[p][:, ATT_BLK:]
        pv = [jnp.dot(w[p], vs_ref[p, kv_rows, :], preferred_element_type=F32) for p in pairs]
        for p in pairs:
            acc_ref[p] += pv[p]

    def qblock(qi, _):
        q_rows = pl.ds(pl.multiple_of(qi * ATT_BLK, ATT_BLK), ATT_BLK)
        c_ref[...] = jnp.zeros_like(c_ref)
        acc_ref[...] = jnp.zeros_like(acc_ref)
        tiles(qi, q_rows, True)

        def live(carry):
            jj, c_max = carry
            return (jj < qi) & (c_max > DEAD_LOG_WEIGHT)

        def kvblock(carry):
            jj, _ = carry
            j = qi - 1 - jj
            kv_rows = pl.ds(pl.multiple_of(j * ATT_BLK, ATT_BLK), ATT_BLK)
            tiles(qi, kv_rows, False)
            return jj + 1, jnp.max(c_ref[...])

        lax.while_loop(live, kvblock, (jnp.int32(0), jnp.max(c_ref[...])))
        for p in range(n_pairs):
            o_ref[0, q_rows, p * LANES:(p + 1) * LANES] = jnp.where(
                head0, acc_ref[p, :ATT_BLK, :], acc_ref[p, ATT_BLK:, :])
        return 0

    lax.fori_loop(0, n_blk, qblock, 0)


def _attn(qkv3, gq2, gk2):
    bsz, s_len, _ = qkv3.shape
    n_pairs = ATT_HEADS_PER_STEP // 2
    n_steps = SB_HEADS // ATT_HEADS_PER_STEP
    blk = (1, s_len, n_pairs * LANES)
    return pl.pallas_call(
        _attn_kernel,
        grid=(bsz, n_steps),
        in_specs=[
            pl.BlockSpec(blk, lambda b, p: (b, 0, p)),
            pl.BlockSpec(blk, lambda b, p: (b, 0, n_steps + p)),
            pl.BlockSpec(blk, lambda b, p: (b, 0, 2 * n_steps + p)),
            pl.BlockSpec((1, LANES), lambda b, p: (0, 0)),
            pl.BlockSpec((1, LANES), lambda b, p: (0, 0)),
        ],
        out_specs=pl.BlockSpec(blk, lambda b, p: (b, 0, p)),
        out_shape=jax.ShapeDtypeStruct((bsz, s_len, SB_WIDTH), F32),
        scratch_shapes=[
            pltpu.VMEM((n_pairs, s_len // ATT_BLK, 2 * ATT_BLK, LANES), BF16),
            pltpu.VMEM((n_pairs, s_len, LANES), BF16),
            pltpu.VMEM((n_pairs, s_len, LANES), BF16),
            pltpu.VMEM((2 * ATT_BLK, 2 * ATT_BLK), BF16),
            pltpu.VMEM((n_pairs, 2 * ATT_BLK, ATT_BLK), F32),
            pltpu.VMEM((n_pairs, 2 * ATT_BLK, LANES), F32),
        ],
        compiler_params=_cparams(2),
        name="attn",
    )(qkv3, qkv3, qkv3, gq2, gk2)


def _s5_kernel(u_ref, a_ref, b_ref, c_ref, d_ref, y_ref, hbuf_ref, state_ref):
    tc, bsz, _ = u_ref.shape
    rows = tc * bsz

    @pl.when(pl.program_id(0) == 0)
    def _():
        state_ref[...] = jnp.zeros_like(state_ref)

    u2 = u_ref[...].reshape(rows, SSM_WIDTH)
    ub = u2.astype(BF16)
    for hf in range(2):
        uh = ub[:, hf * HALF_U:(hf + 1) * HALF_U]
        xh = jnp.dot(uh, b_ref[hf], preferred_element_type=F32)
        hbuf_ref[...] = xh.reshape(tc, bsz, 2 * HALF_STATE)
        for lc in range(HALF_STATE // SCAN_LANES):
            re = pl.ds(lc * SCAN_LANES, SCAN_LANES)
            im = pl.ds(HALF_STATE + lc * SCAN_LANES, SCAN_LANES)
            ar = a_ref[hf, :, re]
            ai = a_ref[hf, :, im]

            def step(t, carry, re=re, im=im, ar=ar, ai=ai):
                hr, hi = carry
                nr = ar * hr - ai * hi + hbuf_ref[t, :, re]
                ni = ar * hi + ai * hr + hbuf_ref[t, :, im]
                hbuf_ref[t, :, re] = nr
                hbuf_ref[t, :, im] = ni
                return nr, ni

            hr, hi = lax.fori_loop(0, tc, step,
                                   (state_ref[hf, :, re], state_ref[hf, :, im]),
                                   unroll=8)
            state_ref[hf, :, re] = hr
            state_ref[hf, :, im] = hi
        hb = hbuf_ref[...].reshape(rows, 2 * HALF_STATE).astype(BF16)
        yh = jnp.dot(hb, c_ref[hf], preferred_element_type=F32)
        cols = slice(hf * HALF_U, (hf + 1) * HALF_U)
        yh = yh + d_ref[:, cols] * u2[:, cols]
        y_ref[:, :, cols] = yh.reshape(tc, bsz, HALF_U)


def _s5(u_tm3, a_bc, b_cat, c_cat, d_skip):
    s_len, bsz, _ = u_tm3.shape
    return pl.pallas_call(
        _s5_kernel,
        grid=(s_len // SCAN_CHUNK,),
        in_specs=[
            pl.BlockSpec((SCAN_CHUNK, bsz, SSM_WIDTH), lambda c: (c, 0, 0)),
            pl.BlockSpec((2, bsz, 2 * HALF_STATE), lambda c: (0, 0, 0)),
            pl.BlockSpec((2, HALF_U, 2 * HALF_STATE), lambda c: (0, 0, 0)),
            pl.BlockSpec((2, 2 * HALF_STATE, HALF_U), lambda c: (0, 0, 0)),
            pl.BlockSpec((1, SSM_WIDTH), lambda c: (0, 0)),
        ],
        out_specs=pl.BlockSpec((SCAN_CHUNK, bsz, SSM_WIDTH), lambda c: (c, 0, 0)),
        out_shape=jax.ShapeDtypeStruct((s_len, bsz, SSM_WIDTH), F32),
        scratch_shapes=[
            pltpu.VMEM((SCAN_CHUNK, bsz, 2 * HALF_STATE), F32),
            pltpu.VMEM((2, bsz, 2 * HALF_STATE), F32),
        ],
        compiler_params=_cparams(1),
        name="s5",
    )(u_tm3, a_bc, b_cat, c_cat, d_skip)


def _s5_params(a_re, a_im, log_dt, b_re, b_im, c_re, c_im, bsz):
    dt = jnp.exp(log_dt)[:, None]
    mag = jnp.exp(a_re * dt)
    abar_r = mag * jnp.cos(a_im * dt)
    abar_i = mag * jnp.sin(a_im * dt)
    den = a_re * a_re + a_im * a_im
    nr = abar_r - 1.0
    ni = abar_i
    fr = (nr * a_re + ni * a_im) / den
    fi = (ni * a_re - nr * a_im) / den
    bbar_r = fr[..., None] * b_re - fi[..., None] * b_im
    bbar_i = fr[..., None] * b_im + fi[..., None] * b_re
    gh = SSM_GROUPS // 2
    eye = jnp.eye(gh, dtype=F32)

    def a_half(hf):
        sl = slice(hf * gh, (hf + 1) * gh)
        row = jnp.concatenate([abar_r[sl].reshape(-1), abar_i[sl].reshape(-1)])
        return jnp.broadcast_to(row[None], (bsz, 2 * HALF_STATE))

    def b_half(bb, hf):
        blk = bb[hf * gh:(hf + 1) * gh]
        return jnp.einsum('gpc,gk->gckp', blk, eye).reshape(HALF_U, HALF_STATE)

    def c_half(cc, hf):
        blk = cc[hf * gh:(hf + 1) * gh]
        return jnp.einsum('gcp,gk->gpkc', blk, eye).reshape(HALF_STATE, HALF_U)

    a_bc = jnp.stack([a_half(0), a_half(1)])
    b_cat = jnp.stack([jnp.concatenate([b_half(bbar_r, hf), b_half(bbar_i, hf)], axis=1)
                       for hf in range(2)]).astype(BF16)
    c_cat = jnp.stack([jnp.concatenate([c_half(c_re, hf), -c_half(c_im, hf)], axis=0)
                       for hf in range(2)]).astype(BF16)
    return a_bc, b_cat, c_cat


def _merge_kernel(x_ref, o_ref, y_ref, gates_ref, wab_ref, wglu_ref, bglu_ref, wout_ref,
                  gffn_ref, wrh_ref, wrl_ref, br_ref,
                  x1_ref, xn_ref, idx_ref, gate_ref, rank_ref, cnt_ref, carry_ref):
    @pl.when(pl.program_id(0) == 0)
    def _():
        carry_ref[...] = jnp.zeros_like(carry_ref)

    subs = [pl.ds(s * TOK_TILE, TOK_TILE) for s in range(MERGE_SUBTILES)]
    nt = (((1,), (1,)), ((), ()))

    attn_branch = [jnp.dot(o_ref[s, :].astype(BF16), wab_ref[...], preferred_element_type=F32)
                   for s in subs]
    zg = [jnp.dot(jax.nn.gelu(y_ref[s, :]).astype(BF16), wglu_ref[...],
                  preferred_element_type=F32) + bglu_ref[...] for s in subs]
    mixed = []
    for i, s in enumerate(subs):
        ssm_branch = zg[i][:, :D_MODEL] * jax.nn.sigmoid(zg[i][:, D_MODEL:])
        mixed.append((jax.nn.sigmoid(gates_ref[s, :D_MODEL]) * attn_branch[i]
                      + jax.nn.sigmoid(gates_ref[s, D_MODEL:]) * ssm_branch).astype(BF16))
    x1 = [x_ref[s, :] + jnp.dot(mixed[i], wout_ref[...], preferred_element_type=F32)
          for i, s in enumerate(subs)]
    xh, xl = [], []
    for i, s in enumerate(subs):
        x1_ref[s, :] = x1[i]
        xn = _rms(x1[i], gffn_ref[...])
        for c in range(ROW_TILES):
            xn_ref[pl.ds(i * TOK_TILE * ROW_TILES + c, TOK_TILE, stride=ROW_TILES), :] = (
                xn[:, c * LANES:(c + 1) * LANES])
        xh.append(xn.astype(BF16))
        xl.append((xn - xh[i].astype(F32)).astype(BF16))
    logits = [(lax.dot_general(wrh_ref[...], xh[i], nt, preferred_element_type=F32)
               + lax.dot_general(wrh_ref[...], xl[i], nt, preferred_element_type=F32)
               + lax.dot_general(wrl_ref[...], xh[i], nt, preferred_element_type=F32))
              + br_ref[...] for i in range(MERGE_SUBTILES)]

    e_iota = lax.broadcasted_iota(jnp.int32, (N_EXPERTS, TOK_TILE), 0).astype(F32)
    k_iota = lax.broadcasted_iota(jnp.int32, (SUBLANES, TOK_TILE), 0)
    rr = lax.broadcasted_iota(jnp.int32, (TOK_TILE, TOK_TILE), 0)
    cc = lax.broadcasted_iota(jnp.int32, (TOK_TILE, TOK_TILE), 1)
    earlier = jnp.where(rr < cc, 1.0, 0.0).astype(BF16)

    for i, s in enumerate(subs):
        work = logits[i]
        hits, vals = [], []
        sel = jnp.zeros((N_EXPERTS, TOK_TILE), F32)
        for _ in range(TOP_K):
            m = jnp.max(work, axis=0, keepdims=True)
            pick = jnp.min(jnp.where(work == m, e_iota, float(N_EXPERTS)), axis=0, keepdims=True)
            hit = e_iota == pick
            work = jnp.where(hit, -jnp.inf, work)
            sel = sel + jnp.where(hit, 1.0, 0.0)
            hits.append((hit, pick))
            vals.append(m)
        before = jnp.dot(sel.astype(BF16), earlier, preferred_element_type=F32) + carry_ref[...]
        exps = [jnp.exp(v - vals[0]) for v in vals]
        denom = exps[0] + exps[1] + exps[2] + exps[3]
        idx = jnp.zeros((SUBLANES, TOK_TILE), jnp.int32)
        gate = jnp.zeros((SUBLANES, TOK_TILE), F32)
        rank = jnp.zeros((SUBLANES, TOK_TILE), jnp.int32)
        for k in range(TOP_K):
            hit, pick = hits[k]
            rk = jnp.sum(jnp.where(hit, before, 0.0), axis=0, keepdims=True)
            idx = jnp.where(k_iota == k, pick.astype(jnp.int32), idx)
            gate = jnp.where(k_iota == k, exps[k] / denom, gate)
            rank = jnp.where(k_iota == k, rk.astype(jnp.int32), rank)
        idx_ref[:, s] = idx
        gate_ref[:, s] = gate
        rank_ref[:, s] = rank
        carry_ref[...] += jnp.sum(sel, axis=1, keepdims=True)
    cnt_ref[...] = jnp.broadcast_to(carry_ref[...], cnt_ref.shape)


def _merge(x2d, o2d, y_tm2, gates, wab, wglu, bglu, wout, gffn, wrh_t, wrl_t, br_col, s_len):
    n = x2d.shape[0]
    tile = MERGE_SUBTILES * TOK_TILE
    tiles_per_seq = s_len // tile
    full = lambda shape: pl.BlockSpec(shape, lambda i: (0,) * len(shape))
    row = lambda w: pl.BlockSpec((tile, w), lambda i: (i, 0))
    col = pl.BlockSpec((SUBLANES, tile), lambda i: (0, i))
    return pl.pallas_call(
        _merge_kernel,
        grid=(n // tile,),
        in_specs=[
            row(D_MODEL), row(SB_WIDTH),
            pl.BlockSpec((tile, SSM_WIDTH),
                         lambda i: (i % tiles_per_seq, i // tiles_per_seq)),
            row(2 * D_MODEL),
            full((SB_WIDTH, D_MODEL)), full((SSM_WIDTH, 2 * D_MODEL)), full((1, 2 * D_MODEL)),
            full((D_MODEL, D_MODEL)), full((1, D_MODEL)),
            full((N_EXPERTS, D_MODEL)), full((N_EXPERTS, D_MODEL)), full((N_EXPERTS, 1)),
        ],
        out_specs=[row(D_MODEL), pl.BlockSpec((tile * ROW_TILES, LANES), lambda i: (i, 0)),
                   col, col, col, full((N_EXPERTS, LANES))],
        out_shape=[
            jax.ShapeDtypeStruct((n, D_MODEL), F32),
            jax.ShapeDtypeStruct((n * ROW_TILES, LANES), F32),
            jax.ShapeDtypeStruct((SUBLANES, n), jnp.int32),
            jax.ShapeDtypeStruct((SUBLANES, n), F32),
            jax.ShapeDtypeStruct((SUBLANES, n), jnp.int32),
            jax.ShapeDtypeStruct((N_EXPERTS, LANES), F32),
        ],
        scratch_shapes=[pltpu.VMEM((N_EXPERTS, 1), F32)],
        compiler_params=_cparams(1),
        name="merge",
    )(x2d, o2d, y_tm2, gates, wab, wglu, bglu, wout, gffn, wrh_t, wrl_t, br_col)


def _lane_tile(ref2d, c, n_rows):
    return ref2d.at[pl.ds(c, n_rows, stride=ROW_TILES), :]


def _load_row_tiled(ref2d, n_rows):
    return jnp.concatenate([_lane_tile(ref2d, c, n_rows)[...] for c in range(ROW_TILES)], axis=1)


def _dispatch_kernel(slot_ref, xn_ref, xs_ref, sem):
    def issue(g, _):
        for k in range(TOP_K):
            for j in range(SUBLANES):
                t = g * SUBLANES + j
                pltpu.make_async_copy(xn_ref.at[t], xs_ref.at[slot_ref[k * TOK_TILE + t]],
                                      sem).start(priority=j % 2)
        return 0

    lax.fori_loop(0, TILE_GROUPS, issue, 0)
    for _ in range(TOP_K):
        pltpu.make_async_copy(xn_ref, xs_ref.at[pl.ds(0, TOK_TILE)], sem).wait()


def _dispatch(slots_flat, xn3):
    n = xn3.shape[0]
    return pl.pallas_call(
        _dispatch_kernel,
        grid=(n // TOK_TILE,),
        in_specs=[
            pl.BlockSpec((TOP_K * TOK_TILE,), lambda i: (i,), memory_space=pltpu.SMEM),
            pl.BlockSpec((TOK_TILE, ROW_TILES, LANES), lambda i: (i, 0, 0)),
        ],
        out_specs=pl.BlockSpec(memory_space=pl.ANY),
        out_shape=jax.ShapeDtypeStruct((n * TOP_K, ROW_TILES, LANES), F32),
        scratch_shapes=[pltpu.SemaphoreType.DMA],
        compiler_params=_cparams(1),
        name="dispatch",
    )(slots_flat, xn3)


FLAG_VALID, FLAG_FIRST_VISIT, FLAG_NEW_EXPERT = 1, 2, 4


def _experts_kernel(blk_ref, exp_ref, nexp_ref, flag_ref, start_ref,
                    xs_ref, wup_hbm_ref, bup_ref, wdn_hbm_ref, bdn_ref, ys_ref,
                    wup_bf_ref, wdn_bf_ref, wup_f32_ref, wdn_f32_ref, wsem):
    i = pl.program_id(0)
    flags = flag_ref[i]

    def weight_copies(e):
        return (pltpu.make_async_copy(wup_hbm_ref.at[e], wup_f32_ref, wsem.at[0]),
                pltpu.make_async_copy(wdn_hbm_ref.at[e], wdn_f32_ref, wsem.at[1]))

    @pl.when(i == 0)
    def _():
        for cp in weight_copies(exp_ref[0]):
            cp.start(priority=1)

    @pl.when((flags & FLAG_NEW_EXPERT) != 0)
    def _():
        for cp in weight_copies(exp_ref[i]):
            cp.wait()
        wup_bf_ref[...] = wup_f32_ref[...].astype(BF16)
        wdn_bf_ref[...] = wdn_f32_ref[...].astype(BF16)
        nxt = nexp_ref[i]

        @pl.when(nxt >= 0)
        def _():
            for cp in weight_copies(nxt):
                cp.start(priority=1)

    @pl.when((flags & FLAG_VALID) != 0)
    def _():
        e = exp_ref[i]
        x = _load_row_tiled(xs_ref, ROW_BLK).astype(BF16)
        acts = []
        for c in range(D_FF // FF_CHUNK):
            gcols = pl.ds(c * FF_CHUNK, FF_CHUNK)
            ucols = pl.ds(D_FF + c * FF_CHUNK, FF_CHUNK)
            g = jnp.dot(x, wup_bf_ref[:, gcols], preferred_element_type=F32) + bup_ref[0, :, gcols]
            up = jnp.dot(x, wup_bf_ref[:, ucols], preferred_element_type=F32) + bup_ref[0, :, ucols]
            g = jnp.minimum(g, SWIGLU_LIMIT)
            up = jnp.clip(up, -SWIGLU_LIMIT, SWIGLU_LIMIT)
            acts.append(((up + 1.0) * g * jax.nn.sigmoid(SWIGLU_ALPHA * g)).astype(BF16))
        y = jnp.dot(jnp.concatenate(acts, axis=1), wdn_bf_ref[...],
                    preferred_element_type=F32) + bdn_ref[0]
        row = blk_ref[i] * ROW_BLK + lax.broadcasted_iota(jnp.int32, (ROW_BLK, 1), 0)
        mine_or_later = jnp.broadcast_to(row >= start_ref[e], (ROW_BLK, LANES))

        @pl.when((flags & FLAG_FIRST_VISIT) != 0)
        def _():
            for c in range(ROW_TILES):
                _lane_tile(ys_ref, c, ROW_BLK)[...] = y[:, c * LANES:(c + 1) * LANES]

        @pl.when((flags & FLAG_FIRST_VISIT) == 0)
        def _():
            for c in range(ROW_TILES):
                tile = _lane_tile(ys_ref, c, ROW_BLK)
                tile[...] = jnp.where(mine_or_later, y[:, c * LANES:(c + 1) * LANES], tile[...])


def _experts(item_blk, item_exp, item_nexp, item_flag, starts, xs2d, w_up, b_up3, w_down, b_down3):
    n_items = item_blk.shape[0]
    rows_blk = pl.BlockSpec((ROW_BLK * ROW_TILES, LANES), lambda i, b, e, x, f, s: (b[i], 0))
    grid_spec = pltpu.PrefetchScalarGridSpec(
        num_scalar_prefetch=5,
        grid=(n_items,),
        in_specs=[
            rows_blk,
            pl.BlockSpec(memory_space=pl.ANY),
            pl.BlockSpec((1, 1, 2 * D_FF), lambda i, b, e, x, f, s: (e[i], 0, 0)),
            pl.BlockSpec(memory_space=pl.ANY),
            pl.BlockSpec((1, 1, D_MODEL), lambda i, b, e, x, f, s: (e[i], 0, 0)),
        ],
        out_specs=rows_blk,
        scratch_shapes=[
            pltpu.VMEM((D_MODEL, 2 * D_FF), BF16),
            pltpu.VMEM((D_FF, D_MODEL), BF16),
            pltpu.VMEM((D_MODEL, 2 * D_FF), F32),
            pltpu.VMEM((D_FF, D_MODEL), F32),
            pltpu.SemaphoreType.DMA((2,)),
        ],
    )
    return pl.pallas_call(
        _experts_kernel,
        grid_spec=grid_spec,
        out_shape=jax.ShapeDtypeStruct(xs2d.shape, F32),
        compiler_params=_cparams(1),
        name="experts",
    )(item_blk, item_exp, item_nexp, item_flag, starts, xs2d, w_up, b_up3, w_down, b_down3)


def _build_items(counts, n_rows):
    ends = jnp.cumsum(counts)
    starts = ends - counts
    n_blocks = n_rows // ROW_BLK
    n_items = n_blocks + N_ITEMS_EXTRA
    lo = jnp.arange(n_blocks, dtype=jnp.int32)[:, None] * ROW_BLK
    overlap = jnp.minimum(ends[None, :], lo + ROW_BLK) - jnp.maximum(starts[None, :], lo)
    live = (overlap > 0).reshape(-1)
    n_live = jnp.sum(live.astype(jnp.int32))
    (pos,) = jnp.nonzero(live, size=n_items, fill_value=0)
    pos = pos.astype(jnp.int32)
    k = jnp.arange(n_items, dtype=jnp.int32)
    valid = k < n_live
    pos = jnp.where(valid, pos, pos[jnp.maximum(n_live - 1, 0)])
    blk = pos // N_EXPERTS
    exp = pos % N_EXPERTS
    prev_blk = jnp.concatenate([jnp.full((1,), -1, jnp.int32), blk[:-1]])
    prev_exp = jnp.concatenate([jnp.full((1,), -1, jnp.int32), exp[:-1]])
    new_exp = valid & (exp != prev_exp)
    flags = (jnp.where(valid, FLAG_VALID, 0)
             | jnp.where(valid & (blk != prev_blk), FLAG_FIRST_VISIT, 0)
             | jnp.where(new_exp, FLAG_NEW_EXPERT, 0)).astype(jnp.int32)
    first_at = jnp.where(new_exp, k, n_items)
    next_first = jnp.flip(lax.cummin(jnp.flip(first_at)))
    next_first = jnp.concatenate([next_first[1:], jnp.full((1,), n_items, jnp.int32)])
    nexp = jnp.where(next_first < n_items, exp[jnp.minimum(next_first, n_items - 1)], -1)
    starts33 = jnp.concatenate([starts, ends[-1:]]).astype(jnp.int32)
    return blk, exp, nexp.astype(jnp.int32), flags, starts33


def _combine_kernel(slot_cur_ref, slot_next_ref, ys3_ref, ys2d_ref, x1_ref, gate_ref, p_ref,
                    gpg_ref, wpg_ref, wpp_ref, gpp_ref, out_ref, rows_ref, sem):
    i = pl.program_id(0)
    last = pl.num_programs(0) - 1
    cur = i % 2
    nxt = 1 - cur

    def start(slot_ref, buf):
        for k in range(TOP_K):
            for t in range(TOK_TILE):
                pltpu.make_async_copy(ys3_ref.at[slot_ref[k * TOK_TILE + t]],
                                      rows_ref.at[buf, k, pl.ds(t * ROW_TILES, ROW_TILES)],
                                      sem.at[buf]).start(priority=t % 2)

    def wait(buf):
        for k in range(TOP_K):
            pltpu.make_async_copy(ys2d_ref.at[pl.ds(0, TOK_TILE * ROW_TILES)],
                                  rows_ref.at[buf, k], sem.at[buf]).wait()

    @pl.when(i == 0)
    def _():
        start(slot_cur_ref, 0)

    wait(cur)
    gate = jnp.concatenate([gate_ref[...], jnp.zeros((LANES - SUBLANES, TOK_TILE), F32)],
                           axis=0).T
    moe = []
    for c in range(ROW_TILES):
        acc = gate[:, 0:1] * _lane_tile(rows_ref.at[cur, 0], c, TOK_TILE)[...]
        for k in range(1, TOP_K):
            acc = acc + gate[:, k:k + 1] * _lane_tile(rows_ref.at[cur, k], c, TOK_TILE)[...]
        moe.append(acc)
    x2 = x1_ref[...] + jnp.concatenate(moe, axis=1)
    start(slot_next_ref, nxt)
    ple = _rms(jnp.dot(p_ref[...].astype(BF16), wpp_ref[...], preferred_element_type=F32),
               gpp_ref[...])
    pg = jax.nn.sigmoid(jnp.dot(_rms(x2, gpg_ref[...]).astype(BF16), wpg_ref[...],
                                preferred_element_type=F32))
    out_ref[...] = x2 + pg * ple

    @pl.when(i == last)
    def _():
        wait(nxt)


def _combine(slots_flat, ys2d, x1, gate, p2d, gpg, wpg, wpp, gpp):
    n = x1.shape[0]
    n_tiles = n // TOK_TILE
    full = lambda shape: pl.BlockSpec(shape, lambda i: (0,) * len(shape))
    row = lambda w: pl.BlockSpec((TOK_TILE, w), lambda i: (i, 0))
    return pl.pallas_call(
        _combine_kernel,
        grid=(n_tiles,),
        in_specs=[
            pl.BlockSpec((TOP_K * TOK_TILE,), lambda i: (i,), memory_space=pltpu.SMEM),
            pl.BlockSpec((TOP_K * TOK_TILE,), lambda i: (jnp.minimum(i + 1, n_tiles - 1),),
                         memory_space=pltpu.SMEM),
            pl.BlockSpec(memory_space=pl.ANY), pl.BlockSpec(memory_space=pl.ANY),
            row(D_MODEL), pl.BlockSpec((SUBLANES, TOK_TILE), lambda i: (0, i)), row(PLE_DIM),
            full((1, D_MODEL)), full((D_MODEL, D_MODEL)), full((PLE_DIM, D_MODEL)),
            full((1, D_MODEL)),
        ],
        out_specs=row(D_MODEL),
        out_shape=jax.ShapeDtypeStruct((n, D_MODEL), F32),
        scratch_shapes=[pltpu.VMEM((2, TOP_K, TOK_TILE * ROW_TILES, LANES), F32),
                        pltpu.SemaphoreType.DMA((2,))],
        compiler_params=_cparams(1),
        name="combine",
    )(slots_flat, slots_flat, ys2d.reshape(-1, ROW_TILES, LANES), ys2d, x1, gate, p2d,
      gpg, wpg, wpp, gpp)


def kernel(x, p, g_mix, w_in, g_q, g_k, w_attn_branch, a_re, a_im, log_dt, b_re, b_im, c_re, c_im, d_skip, w_glu, b_glu, w_out, g_ffn, w_router, b_router, w_up, b_up, w_down, b_down, g_ple_gate, w_ple_gate, w_ple_proj, g_ple_post):
    bsz, s_len, d = x.shape
    depth = w_in.shape[0]
    n = bsz * s_len
    assert d == D_MODEL and s_len % (MERGE_SUBTILES * TOK_TILE) == 0 and s_len % SCAN_CHUNK == 0
    assert bsz == SUBLANES, "the S5 scan keeps the batch on the sublane axis"

    x2d = x.reshape(n, d)
    for i in range(depth):
        qkv, u_tm, gates = _inproj(x2d, g_mix[i][None], w_in[i].astype(BF16), bsz, s_len)
        o = _attn(qkv.reshape(bsz, s_len, 3 * SB_WIDTH),
                  jnp.tile(g_q[i], 2)[None], jnp.tile(g_k[i], 2)[None])
        a_bc, b_cat, c_cat = _s5_params(a_re[i], a_im[i], log_dt[i], b_re[i], b_im[i],
                                        c_re[i], c_im[i], bsz)
        y_tm = _s5(u_tm.reshape(s_len, bsz, SSM_WIDTH), a_bc, b_cat, c_cat, d_skip[i][None])

        wr_t = w_router[i].T
        wr_hi = wr_t.astype(BF16)
        wr_lo = (wr_t - wr_hi.astype(F32)).astype(BF16)
        x1, xn, idx, gate, rank, counts = _merge(
            x2d, o.reshape(n, SB_WIDTH), y_tm.reshape(s_len, bsz * SSM_WIDTH), gates,
            w_attn_branch[i].astype(BF16), w_glu[i].astype(BF16), b_glu[i][None],
            w_out[i].astype(BF16), g_ffn[i][None], wr_hi, wr_lo, b_router[i][:, None], s_len)

        counts_i = counts[:, 0].astype(jnp.int32)
        blk, exp, nexp, flags, starts33 = _build_items(counts_i, n * TOP_K)
        hot = idx[:TOP_K, :, None] == jnp.arange(N_EXPERTS, dtype=jnp.int32)
        slots = jnp.sum(jnp.where(hot, starts33[:N_EXPERTS], 0), axis=-1) + rank[:TOP_K]
        slots_flat = slots.reshape(TOP_K, n // TOK_TILE, TOK_TILE).transpose(1, 0, 2).reshape(-1)

        xs3 = _dispatch(slots_flat, xn.reshape(n, ROW_TILES, LANES))
        ys2d = _experts(blk, exp, nexp, flags, starts33, xs3.reshape(n * TOP_K * ROW_TILES, LANES),
                        w_up[i], b_up[i][:, None, :], w_down[i], b_down[i][:, None, :])
        x2d = _combine(slots_flat, ys2d, x1, gate, p[i].reshape(n, PLE_DIM),
                       g_ple_gate[i][None], w_ple_gate[i].astype(BF16),
                       w_ple_proj[i].astype(BF16), g_ple_post[i][None])
    return x2d.reshape(bsz, s_len, d)
```

```python
import jax
import jax.numpy as jnp
from jax import lax
from jax.experimental import pallas as pl
from jax.experimental.pallas import tpu as pltpu

F32 = jnp.float32
BF16 = jnp.bfloat16

D_MODEL = 1024
SB_HEADS = 8
SB_HEAD_DIM = 64
SB_WIDTH = SB_HEADS * SB_HEAD_DIM
SSM_GROUP = 16
SSM_WIDTH = 512
SSM_GROUPS = SSM_WIDTH // SSM_GROUP
SSM_STATE = 64
PLE_DIM = 256
N_EXPERTS = 32
TOP_K = 4
D_FF = D_MODEL
SWIGLU_LIMIT = 7.0
SWIGLU_ALPHA = 1.702
EPS = 1e-6

LANES = 128
SUBLANES = 8
VMEM_LIMIT = 56 * 1024 * 1024

TOK_TILE = 256
TILE_GROUPS = TOK_TILE // SUBLANES
ROW_TILES = D_MODEL // LANES
assert ROW_TILES == SUBLANES
MERGE_SUBTILES = 2
ATT_BLK = 128
DEAD_LOG_WEIGHT = -104.0
ATT_HEADS_PER_STEP = 8
SCAN_CHUNK = 64
SCAN_LANES = 512
HALF_U = SSM_WIDTH // 2
HALF_STATE = SSM_GROUPS // 2 * SSM_STATE
ROW_BLK = 256
FF_CHUNK = 256
N_ITEMS_EXTRA = N_EXPERTS - 1


def _cparams(n_axes):
    return pltpu.CompilerParams(
        dimension_semantics=("arbitrary",) * n_axes,
        vmem_limit_bytes=VMEM_LIMIT)


def _rms(x, g):
    ms = jnp.mean(x * x, axis=-1, keepdims=True)
    return x * lax.rsqrt(ms + EPS) * g


def _inproj_kernel(x_ref, g_ref, w_ref, qkv_ref, u_ref, gates_ref):
    h = _rms(x_ref[...], g_ref[...]).astype(BF16)
    n_qkv = 3 * SB_WIDTH
    n_u = n_qkv + SSM_WIDTH
    qkv_ref[...] = jnp.dot(h, w_ref[:, :n_qkv], preferred_element_type=F32)
    u_ref[...] = jnp.dot(h, w_ref[:, n_qkv:n_u], preferred_element_type=F32)
    gates_ref[...] = jnp.dot(h, w_ref[:, n_u:], preferred_element_type=F32)


def _inproj(x2d, g_mix, w_in_bf, bsz, s_len):
    n = x2d.shape[0]
    tiles_per_seq = s_len // TOK_TILE
    in_cols = w_in_bf.shape[1]
    return pl.pallas_call(
        _inproj_kernel,
        grid=(n // TOK_TILE,),
        in_specs=[
            pl.BlockSpec((TOK_TILE, D_MODEL), lambda i: (i, 0)),
            pl.BlockSpec((1, D_MODEL), lambda i: (0, 0)),
            pl.BlockSpec((D_MODEL, in_cols), lambda i: (0, 0)),
        ],
        out_specs=[
            pl.BlockSpec((TOK_TILE, 3 * SB_WIDTH), lambda i: (i, 0)),
            pl.BlockSpec((TOK_TILE, SSM_WIDTH),
                         lambda i: (i % tiles_per_seq, i // tiles_per_seq)),
            pl.BlockSpec((TOK_TILE, 2 * D_MODEL), lambda i: (i, 0)),
        ],
        out_shape=[
            jax.ShapeDtypeStruct((n, 3 * SB_WIDTH), F32),
            jax.ShapeDtypeStruct((s_len, bsz * SSM_WIDTH), F32),
            jax.ShapeDtypeStruct((n, 2 * D_MODEL), F32),
        ],
        compiler_params=_cparams(1),
        name="inproj",
    )(x2d, g_mix, w_in_bf)


def _attn_kernel(q_ref, k_ref, v_ref, gq_ref, gk_ref, o_ref,
                 qs_ref, ks_ref, vs_ref, w2_ref, c_ref, acc_ref):
    s_len = q_ref.shape[1]
    n_blk = s_len // ATT_BLK
    n_pairs = q_ref.shape[2] // LANES
    lane = lax.broadcasted_iota(jnp.int32, (1, LANES), 1)
    head0 = lane < SB_HEAD_DIM

    li = lax.broadcasted_iota(jnp.int32, (LANES, LANES), 0) // SB_HEAD_DIM
    lj = lax.broadcasted_iota(jnp.int32, (LANES, LANES), 1) // SB_HEAD_DIM
    head_mean = jnp.where(li == lj, 1.0 / SB_HEAD_DIM, 0.0).astype(BF16)

    def head_rms(t, g):
        sq = t * t
        hi = sq.astype(BF16)
        lo = (sq - hi.astype(F32)).astype(BF16)
        ms = (jnp.dot(hi, head_mean, preferred_element_type=F32)
              + jnp.dot(lo, head_mean, preferred_element_type=F32))
        return t * lax.rsqrt(ms + EPS) * g

    scale = SB_HEAD_DIM ** -0.5

    def prep(qi, _):
        rows = pl.ds(pl.multiple_of(qi * ATT_BLK, ATT_BLK), ATT_BLK)
        for p in range(n_pairs):
            cols = slice(p * LANES, (p + 1) * LANES)
            qn = head_rms(q_ref[0, rows, cols], gq_ref[...]) * scale
            qs_ref[p, qi, :ATT_BLK, :] = jnp.where(head0, qn, 0.0).astype(BF16)
            qs_ref[p, qi, ATT_BLK:, :] = jnp.where(head0, 0.0, qn).astype(BF16)
            ks_ref[p, rows, :] = head_rms(k_ref[0, rows, cols], gk_ref[...]).astype(BF16)
            vs_ref[p, rows, :] = v_ref[0, rows, cols].astype(BF16)
        return 0

    lax.fori_loop(0, n_blk, prep, 0)

    r = lax.broadcasted_iota(jnp.int32, (2 * ATT_BLK, 2 * ATT_BLK), 0)
    c = lax.broadcasted_iota(jnp.int32, (2 * ATT_BLK, 2 * ATT_BLK), 1)
    r = jnp.where(r >= ATT_BLK, r - ATT_BLK, r)
    w2_ref[...] = jnp.where((c >= ATT_BLK) | (r > c), 1.0, 0.0).astype(BF16)

    ti = lax.broadcasted_iota(jnp.int32, (2 * ATT_BLK, ATT_BLK), 0)
    si = lax.broadcasted_iota(jnp.int32, (2 * ATT_BLK, ATT_BLK), 1)
    causal = si < jnp.where(ti >= ATT_BLK, ti - ATT_BLK, ti)

    def rows_of(blk):
        if isinstance(blk, int):
            return pl.ds(blk * ATT_BLK, ATT_BLK)
        return pl.ds(pl.multiple_of(blk * ATT_BLK, ATT_BLK), ATT_BLK)

    def tiles(sweeps, diag):
        chains = [(w, qi, rows_of(kv), p) for (w, qi, kv) in sweeps for p in range(n_pairs)]
        z = [lax.dot_general(qs_ref[p, qi], ks_ref[p, kv_rows, :], (((1,), (1,)), ((), ())),
                             preferred_element_type=F32)
             for (w, qi, kv_rows, p) in chains]
        log_beta, stacked = [], []
        for zi in z:
            sp = jnp.maximum(zi, 0.0) + jnp.log(1.0 + jnp.exp(-jnp.abs(zi)))
            log_keep = -sp
            log_beta.append(zi - sp)
            if diag:
                log_keep = jnp.where(causal, log_keep, 0.0)
            hi = log_keep.astype(BF16)
            lo = (log_keep - hi.astype(F32)).astype(BF16)
            stacked.append(jnp.concatenate([hi, lo], axis=1))
        sums = [jnp.dot(st, w2_ref[...], preferred_element_type=F32) for st in stacked]
        weights = []
        for i, (w, qi, kv_rows, p) in enumerate(chains):
            wp = jnp.exp(log_beta[i] + sums[i][:, :ATT_BLK] + c_ref[w, p])
            if diag:
                wp = jnp.where(causal, wp, 0.0)
            weights.append(wp.astype(BF16))
            c_ref[w, p] += sums[i][:, ATT_BLK:]
        pv = [jnp.dot(weights[i], vs_ref[p, kv_rows, :], preferred_element_type=F32)
              for i, (w, qi, kv_rows, p) in enumerate(chains)]
        for i, (w, qi, kv_rows, p) in enumerate(chains):
            acc_ref[w, p] += pv[i]

    def qblocks(qb, _):
        qa = 2 * qb
        qc = qa + 1
        c_ref[...] = jnp.zeros_like(c_ref)
        acc_ref[...] = jnp.zeros_like(acc_ref)
        tiles([(0, qa, qa), (1, qc, qc)], True)

        def live(carry):
            jj, c_max = carry
            return (jj < qa) & (c_max > DEAD_LOG_WEIGHT)

        def kvblock(carry):
            jj, _ = carry
            tiles([(0, qa, qa - 1 - jj), (1, qc, qa - jj)], False)
            return jj + 1, jnp.max(c_ref[...])

        jj, _ = lax.while_loop(live, kvblock, (jnp.int32(0), jnp.max(c_ref[...])))

        @pl.when((jj == qa) & (jnp.max(c_ref[1]) > DEAD_LOG_WEIGHT))
        def _():
            tiles([(1, qc, 0)], False)

        for w, qi in ((0, qa), (1, qc)):
            for p in range(n_pairs):
                o_ref[0, rows_of(qi), p * LANES:(p + 1) * LANES] = jnp.where(
                    head0, acc_ref[w, p, :ATT_BLK, :], acc_ref[w, p, ATT_BLK:, :])
        return 0

    lax.fori_loop(0, n_blk // 2, qblocks, 0)


def _attn(qkv3, gq2, gk2):
    bsz, s_len, _ = qkv3.shape
    n_pairs = ATT_HEADS_PER_STEP // 2
    n_steps = SB_HEADS // ATT_HEADS_PER_STEP
    blk = (1, s_len, n_pairs * LANES)
    return pl.pallas_call(
        _attn_kernel,
        grid=(bsz, n_steps),
        in_specs=[
            pl.BlockSpec(blk, lambda b, p: (b, 0, p)),
            pl.BlockSpec(blk, lambda b, p: (b, 0, n_steps + p)),
            pl.BlockSpec(blk, lambda b, p: (b, 0, 2 * n_steps + p)),
            pl.BlockSpec((1, LANES), lambda b, p: (0, 0)),
            pl.BlockSpec((1, LANES), lambda b, p: (0, 0)),
        ],
        out_specs=pl.BlockSpec(blk, lambda b, p: (b, 0, p)),
        out_shape=jax.ShapeDtypeStruct((bsz, s_len, SB_WIDTH), F32),
        scratch_shapes=[
            pltpu.VMEM((n_pairs, s_len // ATT_BLK, 2 * ATT_BLK, LANES), BF16),
            pltpu.VMEM((n_pairs, s_len, LANES), BF16),
            pltpu.VMEM((n_pairs, s_len, LANES), BF16),
            pltpu.VMEM((2 * ATT_BLK, 2 * ATT_BLK), BF16),
            pltpu.VMEM((2, n_pairs, 2 * ATT_BLK, ATT_BLK), F32),
            pltpu.VMEM((2, n_pairs, 2 * ATT_BLK, LANES), F32),
        ],
        compiler_params=_cparams(2),
        name="attn",
    )(qkv3, qkv3, qkv3, gq2, gk2)


def _s5_kernel(u_ref, a_ref, b_ref, c_ref, d_ref, y_ref, hbuf_ref, state_ref):
    tc, bsz, _ = u_ref.shape
    rows = tc * bsz

    @pl.when(pl.program_id(0) == 0)
    def _():
        state_ref[...] = jnp.zeros_like(state_ref)

    u2 = u_ref[...].reshape(rows, SSM_WIDTH)
    ub = u2.astype(BF16)
    for hf in range(2):
        uh = ub[:, hf * HALF_U:(hf + 1) * HALF_U]
        xh = jnp.dot(uh, b_ref[hf], preferred_element_type=F32)
        hbuf_ref[...] = xh.reshape(tc, bsz, 2 * HALF_STATE)
        for lc in range(HALF_STATE // SCAN_LANES):
            re = pl.ds(lc * SCAN_LANES, SCAN_LANES)
            im = pl.ds(HALF_STATE + lc * SCAN_LANES, SCAN_LANES)
            ar = a_ref[hf, :, re]
            ai = a_ref[hf, :, im]

            def step(t, carry, re=re, im=im, ar=ar, ai=ai):
                hr, hi = carry
                nr = ar * hr - ai * hi + hbuf_ref[t, :, re]
                ni = ar * hi + ai * hr + hbuf_ref[t, :, im]
                hbuf_ref[t, :, re] = nr
                hbuf_ref[t, :, im] = ni
                return nr, ni

            hr, hi = lax.fori_loop(0, tc, step,
                                   (state_ref[hf, :, re], state_ref[hf, :, im]),
                                   unroll=8)
            state_ref[hf, :, re] = hr
            state_ref[hf, :, im] = hi
        hb = hbuf_ref[...].reshape(rows, 2 * HALF_STATE).astype(BF16)
        yh = jnp.dot(hb, c_ref[hf], preferred_element_type=F32)
        cols = slice(hf * HALF_U, (hf + 1) * HALF_U)
        yh = yh + d_ref[:, cols] * u2[:, cols]
        y_ref[:, :, cols] = yh.reshape(tc, bsz, HALF_U)


def _s5(u_tm3, a_bc, b_cat, c_cat, d_skip):
    s_len, bsz, _ = u_tm3.shape
    return pl.pallas_call(
        _s5_kernel,
        grid=(s_len // SCAN_CHUNK,),
        in_specs=[
            pl.BlockSpec((SCAN_CHUNK, bsz, SSM_WIDTH), lambda c: (c, 0, 0)),
            pl.BlockSpec((2, bsz, 2 * HALF_STATE), lambda c: (0, 0, 0)),
            pl.BlockSpec((2, HALF_U, 2 * HALF_STATE), lambda c: (0, 0, 0)),
            pl.BlockSpec((2, 2 * HALF_STATE, HALF_U), lambda c: (0, 0, 0)),
            pl.BlockSpec((1, SSM_WIDTH), lambda c: (0, 0)),
        ],
        out_specs=pl.BlockSpec((SCAN_CHUNK, bsz, SSM_WIDTH), lambda c: (c, 0, 0)),
        out_shape=jax.ShapeDtypeStruct((s_len, bsz, SSM_WIDTH), F32),
        scratch_shapes=[
            pltpu.VMEM((SCAN_CHUNK, bsz, 2 * HALF_STATE), F32),
            pltpu.VMEM((2, bsz, 2 * HALF_STATE), F32),
        ],
        compiler_params=_cparams(1),
        name="s5",
    )(u_tm3, a_bc, b_cat, c_cat, d_skip)


def _s5_params(a_re, a_im, log_dt, b_re, b_im, c_re, c_im, bsz):
    dt = jnp.exp(log_dt)[:, None]
    mag = jnp.exp(a_re * dt)
    abar_r = mag * jnp.cos(a_im * dt)
    abar_i = mag * jnp.sin(a_im * dt)
    den = a_re * a_re + a_im * a_im
    nr = abar_r - 1.0
    ni = abar_i
    fr = (nr * a_re + ni * a_im) / den
    fi = (ni * a_re - nr * a_im) / den
    bbar_r = fr[..., None] * b_re - fi[..., None] * b_im
    bbar_i = fr[..., None] * b_im + fi[..., None] * b_re
    gh = SSM_GROUPS // 2
    eye = jnp.eye(gh, dtype=F32)

    def a_half(hf):
        sl = slice(hf * gh, (hf + 1) * gh)
        row = jnp.concatenate([abar_r[sl].reshape(-1), abar_i[sl].reshape(-1)])
        return jnp.broadcast_to(row[None], (bsz, 2 * HALF_STATE))

    def b_half(bb, hf):
        blk = bb[hf * gh:(hf + 1) * gh]
        return jnp.einsum('gpc,gk->gckp', blk, eye).reshape(HALF_U, HALF_STATE)

    def c_half(cc, hf):
        blk = cc[hf * gh:(hf + 1) * gh]
        return jnp.einsum('gcp,gk->gpkc', blk, eye).reshape(HALF_STATE, HALF_U)

    a_bc = jnp.stack([a_half(0), a_half(1)])
    b_cat = jnp.stack([jnp.concatenate([b_half(bbar_r, hf), b_half(bbar_i, hf)], axis=1)
                       for hf in range(2)]).astype(BF16)
    c_cat = jnp.stack([jnp.concatenate([c_half(c_re, hf), -c_half(c_im, hf)], axis=0)
                       for hf in range(2)]).astype(BF16)
    return a_bc, b_cat, c_cat


def _merge_kernel(x_ref, o_ref, y_ref, gates_ref, wab_ref, wglu_ref, bglu_ref, wout_ref,
                  gffn_ref, wrh_ref, wrl_ref, br_ref,
                  x1_ref, xn_ref, idx_ref, gate_ref, rank_ref, cnt_ref, carry_ref):
    @pl.when(pl.program_id(0) == 0)
    def _():
        carry_ref[...] = jnp.zeros_like(carry_ref)

    subs = [pl.ds(s * TOK_TILE, TOK_TILE) for s in range(MERGE_SUBTILES)]
    nt = (((1,), (1,)), ((), ()))

    attn_branch = [jnp.dot(o_ref[s, :].astype(BF16), wab_ref[...], preferred_element_type=F32)
                   for s in subs]
    zg = [jnp.dot(jax.nn.gelu(y_ref[s, :]).astype(BF16), wglu_ref[...],
                  preferred_element_type=F32) + bglu_ref[...] for s in subs]
    mixed = []
    for i, s in enumerate(subs):
        ssm_branch = zg[i][:, :D_MODEL] * jax.nn.sigmoid(zg[i][:, D_MODEL:])
        mixed.append((jax.nn.sigmoid(gates_ref[s, :D_MODEL]) * attn_branch[i]
                      + jax.nn.sigmoid(gates_ref[s, D_MODEL:]) * ssm_branch).astype(BF16))
    x1 = [x_ref[s, :] + jnp.dot(mixed[i], wout_ref[...], preferred_element_type=F32)
          for i, s in enumerate(subs)]
    xh, xl = [], []
    for i, s in enumerate(subs):
        x1_ref[s, :] = x1[i]
        xn = _rms(x1[i], gffn_ref[...])
        for c in range(ROW_TILES):
            xn_ref[pl.ds(i * TOK_TILE * ROW_TILES + c, TOK_TILE, stride=ROW_TILES), :] = (
                xn[:, c * LANES:(c + 1) * LANES])
        xh.append(xn.astype(BF16))
        xl.append((xn - xh[i].astype(F32)).astype(BF16))
    logits = [(lax.dot_general(wrh_ref[...], xh[i], nt, preferred_element_type=F32)
               + lax.dot_general(wrh_ref[...], xl[i], nt, preferred_element_type=F32)
               + lax.dot_general(wrl_ref[...], xh[i], nt, preferred_element_type=F32))
              + br_ref[...] for i in range(MERGE_SUBTILES)]

    e_iota = lax.broadcasted_iota(jnp.int32, (N_EXPERTS, TOK_TILE), 0).astype(F32)
    k_iota = lax.broadcasted_iota(jnp.int32, (SUBLANES, TOK_TILE), 0)
    rr = lax.broadcasted_iota(jnp.int32, (TOK_TILE, TOK_TILE), 0)
    cc = lax.broadcasted_iota(jnp.int32, (TOK_TILE, TOK_TILE), 1)
    earlier = jnp.where(rr < cc, 1.0, 0.0).astype(BF16)

    for i, s in enumerate(subs):
        work = logits[i]
        hits, vals = [], []
        sel = jnp.zeros((N_EXPERTS, TOK_TILE), F32)
        for _ in range(TOP_K):
            m = jnp.max(work, axis=0, keepdims=True)
            pick = jnp.min(jnp.where(work == m, e_iota, float(N_EXPERTS)), axis=0, keepdims=True)
            hit = e_iota == pick
            work = jnp.where(hit, -jnp.inf, work)
            sel = sel + jnp.where(hit, 1.0, 0.0)
            hits.append((hit, pick))
            vals.append(m)
        before = jnp.dot(sel.astype(BF16), earlier, preferred_element_type=F32) + carry_ref[...]
        exps = [jnp.exp(v - vals[0]) for v in vals]
        denom = exps[0] + exps[1] + exps[2] + exps[3]
        idx = jnp.zeros((SUBLANES, TOK_TILE), jnp.int32)
        gate = jnp.zeros((SUBLANES, TOK_TILE), F32)
        rank = jnp.zeros((SUBLANES, TOK_TILE), jnp.int32)
        for k in range(TOP_K):
            hit, pick = hits[k]
            rk = jnp.sum(jnp.where(hit, before, 0.0), axis=0, keepdims=True)
            idx = jnp.where(k_iota == k, pick.astype(jnp.int32), idx)
            gate = jnp.where(k_iota == k, exps[k] / denom, gate)
            rank = jnp.where(k_iota == k, rk.astype(jnp.int32), rank)
        idx_ref[:, s] = idx
        gate_ref[:, s] = gate
        rank_ref[:, s] = rank
        carry_ref[...] += jnp.sum(sel, axis=1, keepdims=True)
    cnt_ref[...] = jnp.broadcast_to(carry_ref[...], cnt_ref.shape)


def _merge(x2d, o2d, y_tm2, gates, wab, wglu, bglu, wout, gffn, wrh_t, wrl_t, br_col, s_len):
    n = x2d.shape[0]
    tile = MERGE_SUBTILES * TOK_TILE
    tiles_per_seq = s_len // tile
    full = lambda shape: pl.BlockSpec(shape, lambda i: (0,) * len(shape))
    row = lambda w: pl.BlockSpec((tile, w), lambda i: (i, 0))
    col = pl.BlockSpec((SUBLANES, tile), lambda i: (0, i))
    return pl.pallas_call(
        _merge_kernel,
        grid=(n // tile,),
        in_specs=[
            row(D_MODEL), row(SB_WIDTH),
            pl.BlockSpec((tile, SSM_WIDTH),
                         lambda i: (i % tiles_per_seq, i // tiles_per_seq)),
            row(2 * D_MODEL),
            full((SB_WIDTH, D_MODEL)), full((SSM_WIDTH, 2 * D_MODEL)), full((1, 2 * D_MODEL)),
            full((D_MODEL, D_MODEL)), full((1, D_MODEL)),
            full((N_EXPERTS, D_MODEL)), full((N_EXPERTS, D_MODEL)), full((N_EXPERTS, 1)),
        ],
        out_specs=[row(D_MODEL), pl.BlockSpec((tile * ROW_TILES, LANES), lambda i: (i, 0)),
                   col, col, col, full((N_EXPERTS, LANES))],
        out_shape=[
            jax.ShapeDtypeStruct((n, D_MODEL), F32),
            jax.ShapeDtypeStruct((n * ROW_TILES, LANES), F32),
            jax.ShapeDtypeStruct((SUBLANES, n), jnp.int32),
            jax.ShapeDtypeStruct((SUBLANES, n), F32),
            jax.ShapeDtypeStruct((SUBLANES, n), jnp.int32),
            jax.ShapeDtypeStruct((N_EXPERTS, LANES), F32),
        ],
        scratch_shapes=[pltpu.VMEM((N_EXPERTS, 1), F32)],
        compiler_params=_cparams(1),
        name="merge",
    )(x2d, o2d, y_tm2, gates, wab, wglu, bglu, wout, gffn, wrh_t, wrl_t, br_col)


def _lane_tile(ref2d, c, n_rows):
    return ref2d.at[pl.ds(c, n_rows, stride=ROW_TILES), :]


def _load_row_tiled(ref2d, n_rows):
    return jnp.concatenate([_lane_tile(ref2d, c, n_rows)[...] for c in range(ROW_TILES)], axis=1)


def _dispatch_kernel(slot_ref, xn_ref, xs_ref, sem):
    def issue(g, _):
        for k in range(TOP_K):
            for j in range(SUBLANES):
                t = g * SUBLANES + j
                pltpu.make_async_copy(xn_ref.at[t], xs_ref.at[slot_ref[k * TOK_TILE + t]],
                                      sem).start(priority=j % 2)
        return 0

    lax.fori_loop(0, TILE_GROUPS, issue, 0)
    for _ in range(TOP_K):
        pltpu.make_async_copy(xn_ref, xs_ref.at[pl.ds(0, TOK_TILE)], sem).wait()


def _dispatch(slots_flat, xn3):
    n = xn3.shape[0]
    return pl.pallas_call(
        _dispatch_kernel,
        grid=(n // TOK_TILE,),
        in_specs=[
            pl.BlockSpec((TOP_K * TOK_TILE,), lambda i: (i,), memory_space=pltpu.SMEM),
            pl.BlockSpec((TOK_TILE, ROW_TILES, LANES), lambda i: (i, 0, 0)),
        ],
        out_specs=pl.BlockSpec(memory_space=pl.ANY),
        out_shape=jax.ShapeDtypeStruct((n * TOP_K, ROW_TILES, LANES), F32),
        scratch_shapes=[pltpu.SemaphoreType.DMA],
        compiler_params=_cparams(1),
        name="dispatch",
    )(slots_flat, xn3)


FLAG_VALID, FLAG_FIRST_VISIT, FLAG_NEW_EXPERT = 1, 2, 4


def _experts_kernel(blk_ref, exp_ref, nexp_ref, flag_ref, start_ref,
                    xs_ref, wup_hbm_ref, bup_ref, wdn_hbm_ref, bdn_ref, ys_ref,
                    wup_bf_ref, wdn_bf_ref, wup_f32_ref, wdn_f32_ref, wsem):
    i = pl.program_id(0)
    flags = flag_ref[i]

    def weight_copies(e):
        return (pltpu.make_async_copy(wup_hbm_ref.at[e], wup_f32_ref, wsem.at[0]),
                pltpu.make_async_copy(wdn_hbm_ref.at[e], wdn_f32_ref, wsem.at[1]))

    @pl.when(i == 0)
    def _():
        for cp in weight_copies(exp_ref[0]):
            cp.start(priority=1)

    @pl.when((flags & FLAG_NEW_EXPERT) != 0)
    def _():
        for cp in weight_copies(exp_ref[i]):
            cp.wait()
        wup_bf_ref[...] = wup_f32_ref[...].astype(BF16)
        wdn_bf_ref[...] = wdn_f32_ref[...].astype(BF16)
        nxt = nexp_ref[i]

        @pl.when(nxt >= 0)
        def _():
            for cp in weight_copies(nxt):
                cp.start(priority=1)

    @pl.when((flags & FLAG_VALID) != 0)
    def _():
        e = exp_ref[i]
        x = _load_row_tiled(xs_ref, ROW_BLK).astype(BF16)
        acts = []
        for c in range(D_FF // FF_CHUNK):
            gcols = pl.ds(c * FF_CHUNK, FF_CHUNK)
            ucols = pl.ds(D_FF + c * FF_CHUNK, FF_CHUNK)
            g = jnp.dot(x, wup_bf_ref[:, gcols], preferred_element_type=F32) + bup_ref[0, :, gcols]
            up = jnp.dot(x, wup_bf_ref[:, ucols], preferred_element_type=F32) + bup_ref[0, :, ucols]
            g = jnp.minimum(g, SWIGLU_LIMIT)
            up = jnp.clip(up, -SWIGLU_LIMIT, SWIGLU_LIMIT)
            acts.append(((up + 1.0) * g * jax.nn.sigmoid(SWIGLU_ALPHA * g)).astype(BF16))
        y = jnp.dot(jnp.concatenate(acts, axis=1), wdn_bf_ref[...],
                    preferred_element_type=F32) + bdn_ref[0]
        row = blk_ref[i] * ROW_BLK + lax.broadcasted_iota(jnp.int32, (ROW_BLK, 1), 0)
        mine_or_later = jnp.broadcast_to(row >= start_ref[e], (ROW_BLK, LANES))

        @pl.when((flags & FLAG_FIRST_VISIT) != 0)
        def _():
            for c in range(ROW_TILES):
                _lane_tile(ys_ref, c, ROW_BLK)[...] = y[:, c * LANES:(c + 1) * LANES]

        @pl.when((flags & FLAG_FIRST_VISIT) == 0)
        def _():
            for c in range(ROW_TILES):
                tile = _lane_tile(ys_ref, c, ROW_BLK)
                tile[...] = jnp.where(mine_or_later, y[:, c * LANES:(c + 1) * LANES], tile[...])


def _experts(item_blk, item_exp, item_nexp, item_flag, starts, xs2d, w_up, b_up3, w_down, b_down3):
    n_items = item_blk.shape[0]
    rows_blk = pl.BlockSpec((ROW_BLK * ROW_TILES, LANES), lambda i, b, e, x, f, s: (b[i], 0))
    grid_spec = pltpu.PrefetchScalarGridSpec(
        num_scalar_prefetch=5,
        grid=(n_items,),
        in_specs=[
            rows_blk,
            pl.BlockSpec(memory_space=pl.ANY),
            pl.BlockSpec((1, 1, 2 * D_FF), lambda i, b, e, x, f, s: (e[i], 0, 0)),
            pl.BlockSpec(memory_space=pl.ANY),
            pl.BlockSpec((1, 1, D_MODEL), lambda i, b, e, x, f, s: (e[i], 0, 0)),
        ],
        out_specs=rows_blk,
        scratch_shapes=[
            pltpu.VMEM((D_MODEL, 2 * D_FF), BF16),
            pltpu.VMEM((D_FF, D_MODEL), BF16),
            pltpu.VMEM((D_MODEL, 2 * D_FF), F32),
            pltpu.VMEM((D_FF, D_MODEL), F32),
            pltpu.SemaphoreType.DMA((2,)),
        ],
    )
    return pl.pallas_call(
        _experts_kernel,
        grid_spec=grid_spec,
        out_shape=jax.ShapeDtypeStruct(xs2d.shape, F32),
        compiler_params=_cparams(1),
        name="experts",
    )(item_blk, item_exp, item_nexp, item_flag, starts, xs2d, w_up, b_up3, w_down, b_down3)


def _build_items(counts, n_rows):
    ends = jnp.cumsum(counts)
    starts = ends - counts
    n_blocks = n_rows // ROW_BLK
    n_items = n_blocks + N_ITEMS_EXTRA
    lo = jnp.arange(n_blocks, dtype=jnp.int32)[:, None] * ROW_BLK
    overlap = jnp.minimum(ends[None, :], lo + ROW_BLK) - jnp.maximum(starts[None, :], lo)
    live = (overlap > 0).reshape(-1)
    n_live = jnp.sum(live.astype(jnp.int32))
    (pos,) = jnp.nonzero(live, size=n_items, fill_value=0)
    pos = pos.astype(jnp.int32)
    k = jnp.arange(n_items, dtype=jnp.int32)
    valid = k < n_live
    pos = jnp.where(valid, pos, pos[jnp.maximum(n_live - 1, 0)])
    blk = pos // N_EXPERTS
    exp = pos % N_EXPERTS
    prev_blk = jnp.concatenate([jnp.full((1,), -1, jnp.int32), blk[:-1]])
    prev_exp = jnp.concatenate([jnp.full((1,), -1, jnp.int32), exp[:-1]])
    new_exp = valid & (exp != prev_exp)
    flags = (jnp.where(valid, FLAG_VALID, 0)
             | jnp.where(valid & (blk != prev_blk), FLAG_FIRST_VISIT, 0)
             | jnp.where(new_exp, FLAG_NEW_EXPERT, 0)).astype(jnp.int32)
    first_at = jnp.where(new_exp, k, n_items)
    next_first = jnp.flip(lax.cummin(jnp.flip(first_at)))
    next_first = jnp.concatenate([next_first[1:], jnp.full((1,), n_items, jnp.int32)])
    nexp = jnp.where(next_first < n_items, exp[jnp.minimum(next_first, n_items - 1)], -1)
    starts33 = jnp.concatenate([starts, ends[-1:]]).astype(jnp.int32)
    return blk, exp, nexp.astype(jnp.int32), flags, starts33


def _combine_kernel(slot_cur_ref, slot_next_ref, ys3_ref, ys2d_ref, x1_ref, gate_ref, p_ref,
                    gpg_ref, wpg_ref, wpp_ref, gpp_ref, out_ref, rows_ref, sem):
    i = pl.program_id(0)
    last = pl.num_programs(0) - 1
    cur = i % 2
    nxt = 1 - cur

    def start(slot_ref, buf):
        for k in range(TOP_K):
            for t in range(TOK_TILE):
                pltpu.make_async_copy(ys3_ref.at[slot_ref[k * TOK_TILE + t]],
                                      rows_ref.at[buf, k, pl.ds(t * ROW_TILES, ROW_TILES)],
                                      sem.at[buf]).start(priority=t % 2)

    def wait(buf):
        for k in range(TOP_K):
            pltpu.make_async_copy(ys2d_ref.at[pl.ds(0, TOK_TILE * ROW_TILES)],
                                  rows_ref.at[buf, k], sem.at[buf]).wait()

    @pl.when(i == 0)
    def _():
        start(slot_cur_ref, 0)

    wait(cur)
    gate = jnp.concatenate([gate_ref[...], jnp.zeros((LANES - SUBLANES, TOK_TILE), F32)],
                           axis=0).T
    moe = []
    for c in range(ROW_TILES):
        acc = gate[:, 0:1] * _lane_tile(rows_ref.at[cur, 0], c, TOK_TILE)[...]
        for k in range(1, TOP_K):
            acc = acc + gate[:, k:k + 1] * _lane_tile(rows_ref.at[cur, k], c, TOK_TILE)[...]
        moe.append(acc)
    x2 = x1_ref[...] + jnp.concatenate(moe, axis=1)
    start(slot_next_ref, nxt)
    ple = _rms(jnp.dot(p_ref[...].astype(BF16), wpp_ref[...], preferred_element_type=F32),
               gpp_ref[...])
    pg = jax.nn.sigmoid(jnp.dot(_rms(x2, gpg_ref[...]).astype(BF16), wpg_ref[...],
                                preferred_element_type=F32))
    out_ref[...] = x2 + pg * ple

    @pl.when(i == last)
    def _():
        wait(nxt)


def _combine(slots_flat, ys2d, x1, gate, p2d, gpg, wpg, wpp, gpp):
    n = x1.shape[0]
    n_tiles = n // TOK_TILE
    full = lambda shape: pl.BlockSpec(shape, lambda i: (0,) * len(shape))
    row = lambda w: pl.BlockSpec((TOK_TILE, w), lambda i: (i, 0))
    return pl.pallas_call(
        _combine_kernel,
        grid=(n_tiles,),
        in_specs=[
            pl.BlockSpec((TOP_K * TOK_TILE,), lambda i: (i,), memory_space=pltpu.SMEM),
            pl.BlockSpec((TOP_K * TOK_TILE,), lambda i: (jnp.minimum(i + 1, n_tiles - 1),),
                         memory_space=pltpu.SMEM),
            pl.BlockSpec(memory_space=pl.ANY), pl.BlockSpec(memory_space=pl.ANY),
            row(D_MODEL), pl.BlockSpec((SUBLANES, TOK_TILE), lambda i: (0, i)), row(PLE_DIM),
            full((1, D_MODEL)), full((D_MODEL, D_MODEL)), full((PLE_DIM, D_MODEL)),
            full((1, D_MODEL)),
        ],
        out_specs=row(D_MODEL),
        out_shape=jax.ShapeDtypeStruct((n, D_MODEL), F32),
        scratch_shapes=[pltpu.VMEM((2, TOP_K, TOK_TILE * ROW_TILES, LANES), F32),
                        pltpu.SemaphoreType.DMA((2,))],
        compiler_params=_cparams(1),
        name="combine",
    )(slots_flat, slots_flat, ys2d.reshape(-1, ROW_TILES, LANES), ys2d, x1, gate, p2d,
      gpg, wpg, wpp, gpp)


def kernel(x, p, g_mix, w_in, g_q, g_k, w_attn_branch, a_re, a_im, log_dt, b_re, b_im, c_re, c_im, d_skip, w_glu, b_glu, w_out, g_ffn, w_router, b_router, w_up, b_up, w_down, b_down, g_ple_gate, w_ple_gate, w_ple_proj, g_ple_post):
    bsz, s_len, d = x.shape
    depth = w_in.shape[0]
    n = bsz * s_len
    assert d == D_MODEL and s_len % (MERGE_SUBTILES * TOK_TILE) == 0 and s_len % SCAN_CHUNK == 0
    assert bsz == SUBLANES, "the S5 scan keeps the batch on the sublane axis"

    x2d = x.reshape(n, d)
    for i in range(depth):
        qkv, u_tm, gates = _inproj(x2d, g_mix[i][None], w_in[i].astype(BF16), bsz, s_len)
        o = _attn(qkv.reshape(bsz, s_len, 3 * SB_WIDTH),
                  jnp.tile(g_q[i], 2)[None], jnp.tile(g_k[i], 2)[None])
        a_bc, b_cat, c_cat = _s5_params(a_re[i], a_im[i], log_dt[i], b_re[i], b_im[i],
                                        c_re[i], c_im[i], bsz)
        y_tm = _s5(u_tm.reshape(s_len, bsz, SSM_WIDTH), a_bc, b_cat, c_cat, d_skip[i][None])

        wr_t = w_router[i].T
        wr_hi = wr_t.astype(BF16)
        wr_lo = (wr_t - wr_hi.astype(F32)).astype(BF16)
        x1, xn, idx, gate, rank, counts = _merge(
            x2d, o.reshape(n, SB_WIDTH), y_tm.reshape(s_len, bsz * SSM_WIDTH), gates,
            w_attn_branch[i].astype(BF16), w_glu[i].astype(BF16), b_glu[i][None],
            w_out[i].astype(BF16), g_ffn[i][None], wr_hi, wr_lo, b_router[i][:, None], s_len)

        counts_i = counts[:, 0].astype(jnp.int32)
        blk, exp, nexp, flags, starts33 = _build_items(counts_i, n * TOP_K)
        hot = idx[:TOP_K, :, None] == jnp.arange(N_EXPERTS, dtype=jnp.int32)
        slots = jnp.sum(jnp.where(hot, starts33[:N_EXPERTS], 0), axis=-1) + rank[:TOP_K]
        slots_flat = slots.reshape(TOP_K, n // TOK_TILE, TOK_TILE).transpose(1, 0, 2).reshape(-1)

        xs3 = _dispatch(slots_flat, xn.reshape(n, ROW_TILES, LANES))
        ys2d = _experts(blk, exp, nexp, flags, starts33, xs3.reshape(n * TOP_K * ROW_TILES, LANES),
                        w_up[i], b_up[i][:, None, :], w_down[i], b_down[i][:, None, :])
        x2d = _combine(slots_flat, ys2d, x1, gate, p[i].reshape(n, PLE_DIM),
                       g_ple_gate[i][None], w_ple_gate[i].astype(BF16),
                       w_ple_proj[i].astype(BF16), g_ple_post[i][None])
    return x2d.reshape(bsz, s_len, d)
```

```python
import jax
import jax.numpy as jnp
from jax import lax
from jax.experimental import pallas as pl
from jax.experimental.pallas import tpu as pltpu

F32 = jnp.float32
BF16 = jnp.bfloat16

D_MODEL = 1024
SB_HEADS = 8
SB_HEAD_DIM = 64
SB_WIDTH = SB_HEADS * SB_HEAD_DIM
SSM_GROUP = 16
SSM_WIDTH = 512
SSM_GROUPS = SSM_WIDTH // SSM_GROUP
SSM_STATE = 64
PLE_DIM = 256
N_EXPERTS = 32
TOP_K = 4
D_FF = D_MODEL
SWIGLU_LIMIT = 7.0
SWIGLU_ALPHA = 1.702
EPS = 1e-6

LANES = 128
SUBLANES = 8
VMEM_LIMIT = 56 * 1024 * 1024

TOK_TILE = 256
TILE_GROUPS = TOK_TILE // SUBLANES
ROW_TILES = D_MODEL // LANES
assert ROW_TILES == SUBLANES
MERGE_SUBTILES = 2
ATT_BLK = 128
DEAD_LOG_WEIGHT = -104.0
ATT_HEADS_PER_STEP = 8
SCAN_CHUNK = 64
SCAN_LANES = 512
HALF_U = SSM_WIDTH // 2
HALF_STATE = SSM_GROUPS // 2 * SSM_STATE
ROW_BLK = 256
FF_CHUNK = 256
N_ITEMS_EXTRA = N_EXPERTS - 1


def _cparams(n_axes):
    return pltpu.CompilerParams(
        dimension_semantics=("arbitrary",) * n_axes,
        vmem_limit_bytes=VMEM_LIMIT)


def _rms(x, g):
    ms = jnp.mean(x * x, axis=-1, keepdims=True)
    return x * lax.rsqrt(ms + EPS) * g


def _inproj_kernel(x_ref, g_ref, w_ref, qkv_ref, u_ref, gates_ref):
    h = _rms(x_ref[...], g_ref[...]).astype(BF16)
    n_qkv = 3 * SB_WIDTH
    n_u = n_qkv + SSM_WIDTH
    qkv_ref[...] = jnp.dot(h, w_ref[:, :n_qkv], preferred_element_type=F32)
    u_ref[...] = jnp.dot(h, w_ref[:, n_qkv:n_u], preferred_element_type=F32)
    gates_ref[...] = jnp.dot(h, w_ref[:, n_u:], preferred_element_type=F32)


def _inproj(x2d, g_mix, w_in_bf, bsz, s_len):
    n = x2d.shape[0]
    tiles_per_seq = s_len // TOK_TILE
    in_cols = w_in_bf.shape[1]
    return pl.pallas_call(
        _inproj_kernel,
        grid=(n // TOK_TILE,),
        in_specs=[
            pl.BlockSpec((TOK_TILE, D_MODEL), lambda i: (i, 0)),
            pl.BlockSpec((1, D_MODEL), lambda i: (0, 0)),
            pl.BlockSpec((D_MODEL, in_cols), lambda i: (0, 0)),
        ],
        out_specs=[
            pl.BlockSpec((TOK_TILE, 3 * SB_WIDTH), lambda i: (i, 0)),
            pl.BlockSpec((TOK_TILE, SSM_WIDTH),
                         lambda i: (i % tiles_per_seq, i // tiles_per_seq)),
            pl.BlockSpec((TOK_TILE, 2 * D_MODEL), lambda i: (i, 0)),
        ],
        out_shape=[
            jax.ShapeDtypeStruct((n, 3 * SB_WIDTH), F32),
            jax.ShapeDtypeStruct((s_len, bsz * SSM_WIDTH), F32),
            jax.ShapeDtypeStruct((n, 2 * D_MODEL), F32),
        ],
        compiler_params=_cparams(1),
        name="inproj",
    )(x2d, g_mix, w_in_bf)


def _attn_kernel(q_ref, k_ref, v_ref, gq_ref, gk_ref, o_ref,
                 qs_ref, ks_ref, vs_ref, w2_ref, c_ref, acc_ref):
    s_len = q_ref.shape[1]
    n_blk = s_len // ATT_BLK
    n_pairs = q_ref.shape[2] // LANES
    lane = lax.broadcasted_iota(jnp.int32, (1, LANES), 1)
    head0 = lane < SB_HEAD_DIM

    li = lax.broadcasted_iota(jnp.int32, (LANES, LANES), 0) // SB_HEAD_DIM
    lj = lax.broadcasted_iota(jnp.int32, (LANES, LANES), 1) // SB_HEAD_DIM
    head_mean = jnp.where(li == lj, 1.0 / SB_HEAD_DIM, 0.0).astype(BF16)

    def head_rms(t, g):
        sq = t * t
        hi = sq.astype(BF16)
        lo = (sq - hi.astype(F32)).astype(BF16)
        ms = (jnp.dot(hi, head_mean, preferred_element_type=F32)
              + jnp.dot(lo, head_mean, preferred_element_type=F32))
        return t * lax.rsqrt(ms + EPS) * g

    scale = SB_HEAD_DIM ** -0.5

    def prep(qi, _):
        rows = pl.ds(pl.multiple_of(qi * ATT_BLK, ATT_BLK), ATT_BLK)
        for p in range(n_pairs):
            cols = slice(p * LANES, (p + 1) * LANES)
            qn = head_rms(q_ref[0, rows, cols], gq_ref[...]) * scale
            qs_ref[p, qi, :ATT_BLK, :] = jnp.where(head0, qn, 0.0).astype(BF16)
            qs_ref[p, qi, ATT_BLK:, :] = jnp.where(head0, 0.0, qn).astype(BF16)
            ks_ref[p, rows, :] = head_rms(k_ref[0, rows, cols], gk_ref[...]).astype(BF16)
            vs_ref[p, rows, :] = v_ref[0, rows, cols].astype(BF16)
        return 0

    lax.fori_loop(0, n_blk, prep, 0)

    r = lax.broadcasted_iota(jnp.int32, (2 * ATT_BLK, 2 * ATT_BLK), 0)
    c = lax.broadcasted_iota(jnp.int32, (2 * ATT_BLK, 2 * ATT_BLK), 1)
    r = jnp.where(r >= ATT_BLK, r - ATT_BLK, r)
    w2_ref[...] = jnp.where((c >= ATT_BLK) | (r > c), 1.0, 0.0).astype(BF16)

    ti = lax.broadcasted_iota(jnp.int32, (2 * ATT_BLK, ATT_BLK), 0)
    si = lax.broadcasted_iota(jnp.int32, (2 * ATT_BLK, ATT_BLK), 1)
    causal = si < jnp.where(ti >= ATT_BLK, ti - ATT_BLK, ti)

    def rows_of(blk):
        if isinstance(blk, int):
            return pl.ds(blk * ATT_BLK, ATT_BLK)
        return pl.ds(pl.multiple_of(blk * ATT_BLK, ATT_BLK), ATT_BLK)

    def tiles(sweeps, diag):
        chains = [(w, qi, rows_of(kv), p) for (w, qi, kv) in sweeps for p in range(n_pairs)]
        z = [lax.dot_general(qs_ref[p, qi], ks_ref[p, kv_rows, :], (((1,), (1,)), ((), ())),
                             preferred_element_type=F32)
             for (w, qi, kv_rows, p) in chains]
        log_beta, stacked = [], []
        for zi in z:
            sp = jnp.maximum(zi, 0.0) + jnp.log(1.0 + jnp.exp(-jnp.abs(zi)))
            log_keep = -sp
            log_beta.append(zi - sp)
            if diag:
                log_keep = jnp.where(causal, log_keep, 0.0)
            hi = log_keep.astype(BF16)
            lo = (log_keep - hi.astype(F32)).astype(BF16)
            stacked.append(jnp.concatenate([hi, lo], axis=1))
        sums = [jnp.dot(st, w2_ref[...], preferred_element_type=F32) for st in stacked]
        weights = []
        for i, (w, qi, kv_rows, p) in enumerate(chains):
            wp = jnp.exp(log_beta[i] + sums[i][:, :ATT_BLK] + c_ref[w, p])
            if diag:
                wp = jnp.where(causal, wp, 0.0)
            weights.append(wp.astype(BF16))
            c_ref[w, p] += sums[i][:, ATT_BLK:]
        pv = [jnp.dot(weights[i], vs_ref[p, kv_rows, :], preferred_element_type=F32)
              for i, (w, qi, kv_rows, p) in enumerate(chains)]
        for i, (w, qi, kv_rows, p) in enumerate(chains):
            acc_ref[w, p] += pv[i]

    def qblocks(qb, _):
        qa = 2 * qb
        qc = qa + 1
        c_ref[...] = jnp.zeros_like(c_ref)
        acc_ref[...] = jnp.zeros_like(acc_ref)
        tiles([(0, qa, qa), (1, qc, qc)], True)

        def live(carry):
            jj, c_max = carry
            return (jj < qa) & (c_max > DEAD_LOG_WEIGHT)

        def kvblock(carry):
            jj, _ = carry
            tiles([(0, qa, qa - 1 - jj), (1, qc, qa - jj)], False)
            return jj + 1, jnp.max(c_ref[...])

        jj, _ = lax.while_loop(live, kvblock, (jnp.int32(0), jnp.max(c_ref[...])))

        @pl.when((jj == qa) & (jnp.max(c_ref[1]) > DEAD_LOG_WEIGHT))
        def _():
            tiles([(1, qc, 0)], False)

        for w, qi in ((0, qa), (1, qc)):
            for p in range(n_pairs):
                o_ref[0, rows_of(qi), p * LANES:(p + 1) * LANES] = jnp.where(
                    head0, acc_ref[w, p, :ATT_BLK, :], acc_ref[w, p, ATT_BLK:, :])
        return 0

    lax.fori_loop(0, n_blk // 2, qblocks, 0)


def _attn(qkv3, gq2, gk2):
    bsz, s_len, _ = qkv3.shape
    n_pairs = ATT_HEADS_PER_STEP // 2
    n_steps = SB_HEADS // ATT_HEADS_PER_STEP
    blk = (1, s_len, n_pairs * LANES)
    return pl.pallas_call(
        _attn_kernel,
        grid=(bsz, n_steps),
        in_specs=[
            pl.BlockSpec(blk, lambda b, p: (b, 0, p)),
            pl.BlockSpec(blk, lambda b, p: (b, 0, n_steps + p)),
            pl.BlockSpec(blk, lambda b, p: (b, 0, 2 * n_steps + p)),
            pl.BlockSpec((1, LANES), lambda b, p: (0, 0)),
            pl.BlockSpec((1, LANES), lambda b, p: (0, 0)),
        ],
        out_specs=pl.BlockSpec(blk, lambda b, p: (b, 0, p)),
        out_shape=jax.ShapeDtypeStruct((bsz, s_len, SB_WIDTH), F32),
        scratch_shapes=[
            pltpu.VMEM((n_pairs, s_len // ATT_BLK, 2 * ATT_BLK, LANES), BF16),
            pltpu.VMEM((n_pairs, s_len, LANES), BF16),
            pltpu.VMEM((n_pairs, s_len, LANES), BF16),
            pltpu.VMEM((2 * ATT_BLK, 2 * ATT_BLK), BF16),
            pltpu.VMEM((2, n_pairs, 2 * ATT_BLK, ATT_BLK), F32),
            pltpu.VMEM((2, n_pairs, 2 * ATT_BLK, LANES), F32),
        ],
        compiler_params=_cparams(2),
        name="attn",
    )(qkv3, qkv3, qkv3, gq2, gk2)


def _s5_kernel(u_ref, a_ref, b_ref, c_ref, d_ref, y_ref, hbuf_ref, state_ref):
    tc, bsz, _ = u_ref.shape
    rows = tc * bsz

    @pl.when(pl.program_id(0) == 0)
    def _():
        state_ref[...] = jnp.zeros_like(state_ref)

    u2 = u_ref[...].reshape(rows, SSM_WIDTH)
    ub = u2.astype(BF16)
    halves = range(2)
    for hf in halves:
        uh = ub[:, hf * HALF_U:(hf + 1) * HALF_U]
        xh = jnp.dot(uh, b_ref[hf], preferred_element_type=F32)
        hbuf_ref[hf] = xh.reshape(tc, bsz, 2 * HALF_STATE)
    for hf in halves:
        for lc in range(HALF_STATE // SCAN_LANES):
            re = pl.ds(lc * SCAN_LANES, SCAN_LANES)
            im = pl.ds(HALF_STATE + lc * SCAN_LANES, SCAN_LANES)
            ar = a_ref[hf, :, re]
            ai = a_ref[hf, :, im]
            hr = state_ref[hf, :, re]
            hi = state_ref[hf, :, im]
            for t in range(tc):
                hr, hi = (ar * hr - ai * hi + hbuf_ref[hf, t, :, re],
                          ar * hi + ai * hr + hbuf_ref[hf, t, :, im])
                hbuf_ref[hf, t, :, re] = hr
                hbuf_ref[hf, t, :, im] = hi
            state_ref[hf, :, re] = hr
            state_ref[hf, :, im] = hi
        hb = hbuf_ref[hf].reshape(rows, 2 * HALF_STATE).astype(BF16)
        yh = jnp.dot(hb, c_ref[hf], preferred_element_type=F32)
        cols = slice(hf * HALF_U, (hf + 1) * HALF_U)
        yh = yh + d_ref[:, cols] * u2[:, cols]
        y_ref[:, :, cols] = yh.reshape(tc, bsz, HALF_U)


def _s5(u_tm3, a_bc, b_cat, c_cat, d_skip):
    s_len, bsz, _ = u_tm3.shape
    return pl.pallas_call(
        _s5_kernel,
        grid=(s_len // SCAN_CHUNK,),
        in_specs=[
            pl.BlockSpec((SCAN_CHUNK, bsz, SSM_WIDTH), lambda c: (c, 0, 0)),
            pl.BlockSpec((2, bsz, 2 * HALF_STATE), lambda c: (0, 0, 0)),
            pl.BlockSpec((2, HALF_U, 2 * HALF_STATE), lambda c: (0, 0, 0)),
            pl.BlockSpec((2, 2 * HALF_STATE, HALF_U), lambda c: (0, 0, 0)),
            pl.BlockSpec((1, SSM_WIDTH), lambda c: (0, 0)),
        ],
        out_specs=pl.BlockSpec((SCAN_CHUNK, bsz, SSM_WIDTH), lambda c: (c, 0, 0)),
        out_shape=jax.ShapeDtypeStruct((s_len, bsz, SSM_WIDTH), F32),
        scratch_shapes=[
            pltpu.VMEM((2, SCAN_CHUNK, bsz, 2 * HALF_STATE), F32),
            pltpu.VMEM((2, bsz, 2 * HALF_STATE), F32),
        ],
        compiler_params=_cparams(1),
        name="s5",
    )(u_tm3, a_bc, b_cat, c_cat, d_skip)


def _s5_params(a_re, a_im, log_dt, b_re, b_im, c_re, c_im, bsz):
    dt = jnp.exp(log_dt)[:, None]
    mag = jnp.exp(a_re * dt)
    abar_r = mag * jnp.cos(a_im * dt)
    abar_i = mag * jnp.sin(a_im * dt)
    den = a_re * a_re + a_im * a_im
    nr = abar_r - 1.0
    ni = abar_i
    fr = (nr * a_re + ni * a_im) / den
    fi = (ni * a_re - nr * a_im) / den
    bbar_r = fr[..., None] * b_re - fi[..., None] * b_im
    bbar_i = fr[..., None] * b_im + fi[..., None] * b_re
    gh = SSM_GROUPS // 2
    eye = jnp.eye(gh, dtype=F32)

    def a_half(hf):
        sl = slice(hf * gh, (hf + 1) * gh)
        row = jnp.concatenate([abar_r[sl].reshape(-1), abar_i[sl].reshape(-1)])
        return jnp.broadcast_to(row[None], (bsz, 2 * HALF_STATE))

    def b_half(bb, hf):
        blk = bb[hf * gh:(hf + 1) * gh]
        return jnp.einsum('gpc,gk->gckp', blk, eye).reshape(HALF_U, HALF_STATE)

    def c_half(cc, hf):
        blk = cc[hf * gh:(hf + 1) * gh]
        return jnp.einsum('gcp,gk->gpkc', blk, eye).reshape(HALF_STATE, HALF_U)

    a_bc = jnp.stack([a_half(0), a_half(1)])
    b_cat = jnp.stack([jnp.concatenate([b_half(bbar_r, hf), b_half(bbar_i, hf)], axis=1)
                       for hf in range(2)]).astype(BF16)
    c_cat = jnp.stack([jnp.concatenate([c_half(c_re, hf), -c_half(c_im, hf)], axis=0)
                       for hf in range(2)]).astype(BF16)
    return a_bc, b_cat, c_cat


def _merge_kernel(x_ref, o_ref, y_ref, gates_ref, wab_ref, wglu_ref, bglu_ref, wout_ref,
                  gffn_ref, wrh_ref, wrl_ref, br_ref,
                  x1_ref, xn_ref, idx_ref, gate_ref, rank_ref, cnt_ref, carry_ref):
    @pl.when(pl.program_id(0) == 0)
    def _():
        carry_ref[...] = jnp.zeros_like(carry_ref)

    subs = [pl.ds(s * TOK_TILE, TOK_TILE) for s in range(MERGE_SUBTILES)]
    nt = (((1,), (1,)), ((), ()))

    attn_branch = [jnp.dot(o_ref[s, :].astype(BF16), wab_ref[...], preferred_element_type=F32)
                   for s in subs]
    zg = [jnp.dot(jax.nn.gelu(y_ref[s, :]).astype(BF16), wglu_ref[...],
                  preferred_element_type=F32) + bglu_ref[...] for s in subs]
    mixed = []
    for i, s in enumerate(subs):
        ssm_branch = zg[i][:, :D_MODEL] * jax.nn.sigmoid(zg[i][:, D_MODEL:])
        mixed.append((jax.nn.sigmoid(gates_ref[s, :D_MODEL]) * attn_branch[i]
                      + jax.nn.sigmoid(gates_ref[s, D_MODEL:]) * ssm_branch).astype(BF16))
    x1 = [x_ref[s, :] + jnp.dot(mixed[i], wout_ref[...], preferred_element_type=F32)
          for i, s in enumerate(subs)]
    xh, xl = [], []
    for i, s in enumerate(subs):
        x1_ref[s, :] = x1[i]
        xn = _rms(x1[i], gffn_ref[...])
        for c in range(ROW_TILES):
            xn_ref[pl.ds(i * TOK_TILE * ROW_TILES + c, TOK_TILE, stride=ROW_TILES), :] = (
                xn[:, c * LANES:(c + 1) * LANES])
        xh.append(xn.astype(BF16))
        xl.append((xn - xh[i].astype(F32)).astype(BF16))
    logits = [(lax.dot_general(wrh_ref[...], xh[i], nt, preferred_element_type=F32)
               + lax.dot_general(wrh_ref[...], xl[i], nt, preferred_element_type=F32)
               + lax.dot_general(wrl_ref[...], xh[i], nt, preferred_element_type=F32))
              + br_ref[...] for i in range(MERGE_SUBTILES)]

    e_iota = lax.broadcasted_iota(jnp.int32, (N_EXPERTS, TOK_TILE), 0).astype(F32)
    k_iota = lax.broadcasted_iota(jnp.int32, (SUBLANES, TOK_TILE), 0)
    rr = lax.broadcasted_iota(jnp.int32, (TOK_TILE, TOK_TILE), 0)
    cc = lax.broadcasted_iota(jnp.int32, (TOK_TILE, TOK_TILE), 1)
    earlier = jnp.where(rr < cc, 1.0, 0.0).astype(BF16)

    for i, s in enumerate(subs):
        work = logits[i]
        hits, vals = [], []
        sel = jnp.zeros((N_EXPERTS, TOK_TILE), F32)
        for _ in range(TOP_K):
            m = jnp.max(work, axis=0, keepdims=True)
            pick = jnp.min(jnp.where(work == m, e_iota, float(N_EXPERTS)), axis=0, keepdims=True)
            hit = e_iota == pick
            work = jnp.where(hit, -jnp.inf, work)
            sel = sel + jnp.where(hit, 1.0, 0.0)
            hits.append((hit, pick))
            vals.append(m)
        before = jnp.dot(sel.astype(BF16), earlier, preferred_element_type=F32) + carry_ref[...]
        exps = [jnp.exp(v - vals[0]) for v in vals]
        denom = exps[0] + exps[1] + exps[2] + exps[3]
        idx = jnp.zeros((SUBLANES, TOK_TILE), jnp.int32)
        gate = jnp.zeros((SUBLANES, TOK_TILE), F32)
        rank = jnp.zeros((SUBLANES, TOK_TILE), jnp.int32)
        for k in range(TOP_K):
            hit, pick = hits[k]
            rk = jnp.sum(jnp.where(hit, before, 0.0), axis=0, keepdims=True)
            idx = jnp.where(k_iota == k, pick.astype(jnp.int32), idx)
            gate = jnp.where(k_iota == k, exps[k] / denom, gate)
            rank = jnp.where(k_iota == k, rk.astype(jnp.int32), rank)
        idx_ref[:, s] = idx
        gate_ref[:, s] = gate
        rank_ref[:, s] = rank
        carry_ref[...] += jnp.sum(sel, axis=1, keepdims=True)
    cnt_ref[...] = jnp.broadcast_to(carry_ref[...], cnt_ref.shape)


def _merge(x2d, o2d, y_tm2, gates, wab, wglu, bglu, wout, gffn, wrh_t, wrl_t, br_col, s_len):
    n = x2d.shape[0]
    tile = MERGE_SUBTILES * TOK_TILE
    tiles_per_seq = s_len // tile
    full = lambda shape: pl.BlockSpec(shape, lambda i: (0,) * len(shape))
    row = lambda w: pl.BlockSpec((tile, w), lambda i: (i, 0))
    col = pl.BlockSpec((SUBLANES, tile), lambda i: (0, i))
    return pl.pallas_call(
        _merge_kernel,
        grid=(n // tile,),
        in_specs=[
            row(D_MODEL), row(SB_WIDTH),
            pl.BlockSpec((tile, SSM_WIDTH),
                         lambda i: (i % tiles_per_seq, i // tiles_per_seq)),
            row(2 * D_MODEL),
            full((SB_WIDTH, D_MODEL)), full((SSM_WIDTH, 2 * D_MODEL)), full((1, 2 * D_MODEL)),
            full((D_MODEL, D_MODEL)), full((1, D_MODEL)),
            full((N_EXPERTS, D_MODEL)), full((N_EXPERTS, D_MODEL)), full((N_EXPERTS, 1)),
        ],
        out_specs=[row(D_MODEL), pl.BlockSpec((tile * ROW_TILES, LANES), lambda i: (i, 0)),
                   col, col, col, full((N_EXPERTS, LANES))],
        out_shape=[
            jax.ShapeDtypeStruct((n, D_MODEL), F32),
            jax.ShapeDtypeStruct((n * ROW_TILES, LANES), F32),
            jax.ShapeDtypeStruct((SUBLANES, n), jnp.int32),
            jax.ShapeDtypeStruct((SUBLANES, n), F32),
            jax.ShapeDtypeStruct((SUBLANES, n), jnp.int32),
            jax.ShapeDtypeStruct((N_EXPERTS, LANES), F32),
        ],
        scratch_shapes=[pltpu.VMEM((N_EXPERTS, 1), F32)],
        compiler_params=_cparams(1),
        name="merge",
    )(x2d, o2d, y_tm2, gates, wab, wglu, bglu, wout, gffn, wrh_t, wrl_t, br_col)


def _lane_tile(ref2d, c, n_rows):
    return ref2d.at[pl.ds(c, n_rows, stride=ROW_TILES), :]


def _load_row_tiled(ref2d, n_rows):
    return jnp.concatenate([_lane_tile(ref2d, c, n_rows)[...] for c in range(ROW_TILES)], axis=1)


def _dispatch_kernel(slot_ref, xn_ref, xs_ref, sem):
    def issue(g, _):
        for k in range(TOP_K):
            for j in range(SUBLANES):
                t = g * SUBLANES + j
                pltpu.make_async_copy(xn_ref.at[t], xs_ref.at[slot_ref[k * TOK_TILE + t]],
                                      sem).start(priority=j % 2)
        return 0

    lax.fori_loop(0, TILE_GROUPS, issue, 0)
    for _ in range(TOP_K):
        pltpu.make_async_copy(xn_ref, xs_ref.at[pl.ds(0, TOK_TILE)], sem).wait()


def _dispatch(slots_flat, xn3):
    n = xn3.shape[0]
    return pl.pallas_call(
        _dispatch_kernel,
        grid=(n // TOK_TILE,),
        in_specs=[
            pl.BlockSpec((TOP_K * TOK_TILE,), lambda i: (i,), memory_space=pltpu.SMEM),
            pl.BlockSpec((TOK_TILE, ROW_TILES, LANES), lambda i: (i, 0, 0)),
        ],
        out_specs=pl.BlockSpec(memory_space=pl.ANY),
        out_shape=jax.ShapeDtypeStruct((n * TOP_K, ROW_TILES, LANES), F32),
        scratch_shapes=[pltpu.SemaphoreType.DMA],
        compiler_params=_cparams(1),
        name="dispatch",
    )(slots_flat, xn3)


FLAG_VALID, FLAG_FIRST_VISIT, FLAG_NEW_EXPERT = 1, 2, 4


def _experts_kernel(blk_ref, exp_ref, nexp_ref, flag_ref, start_ref,
                    xs_ref, wup_hbm_ref, bup_ref, wdn_hbm_ref, bdn_ref, ys_ref,
                    wup_bf_ref, wdn_bf_ref, wup_f32_ref, wdn_f32_ref, wsem):
    i = pl.program_id(0)
    flags = flag_ref[i]

    def weight_copies(e):
        return (pltpu.make_async_copy(wup_hbm_ref.at[e], wup_f32_ref, wsem.at[0]),
                pltpu.make_async_copy(wdn_hbm_ref.at[e], wdn_f32_ref, wsem.at[1]))

    @pl.when(i == 0)
    def _():
        for cp in weight_copies(exp_ref[0]):
            cp.start(priority=1)

    @pl.when((flags & FLAG_NEW_EXPERT) != 0)
    def _():
        for cp in weight_copies(exp_ref[i]):
            cp.wait()
        wup_bf_ref[...] = wup_f32_ref[...].astype(BF16)
        wdn_bf_ref[...] = wdn_f32_ref[...].astype(BF16)
        nxt = nexp_ref[i]

        @pl.when(nxt >= 0)
        def _():
            for cp in weight_copies(nxt):
                cp.start(priority=1)

    @pl.when((flags & FLAG_VALID) != 0)
    def _():
        e = exp_ref[i]
        x = _load_row_tiled(xs_ref, ROW_BLK).astype(BF16)
        acts = []
        for c in range(D_FF // FF_CHUNK):
            gcols = pl.ds(c * FF_CHUNK, FF_CHUNK)
            ucols = pl.ds(D_FF + c * FF_CHUNK, FF_CHUNK)
            g = jnp.dot(x, wup_bf_ref[:, gcols], preferred_element_type=F32) + bup_ref[0, :, gcols]
            up = jnp.dot(x, wup_bf_ref[:, ucols], preferred_element_type=F32) + bup_ref[0, :, ucols]
            g = jnp.minimum(g, SWIGLU_LIMIT)
            up = jnp.clip(up, -SWIGLU_LIMIT, SWIGLU_LIMIT)
            acts.append(((up + 1.0) * g * jax.nn.sigmoid(SWIGLU_ALPHA * g)).astype(BF16))
        y = jnp.dot(jnp.concatenate(acts, axis=1), wdn_bf_ref[...],
                    preferred_element_type=F32) + bdn_ref[0]
        row = blk_ref[i] * ROW_BLK + lax.broadcasted_iota(jnp.int32, (ROW_BLK, 1), 0)
        mine_or_later = jnp.broadcast_to(row >= start_ref[e], (ROW_BLK, LANES))

        @pl.when((flags & FLAG_FIRST_VISIT) != 0)
        def _():
            for c in range(ROW_TILES):
                _lane_tile(ys_ref, c, ROW_BLK)[...] = y[:, c * LANES:(c + 1) * LANES]

        @pl.when((flags & FLAG_FIRST_VISIT) == 0)
        def _():
            for c in range(ROW_TILES):
                tile = _lane_tile(ys_ref, c, ROW_BLK)
                tile[...] = jnp.where(mine_or_later, y[:, c * LANES:(c + 1) * LANES], tile[...])


def _experts(item_blk, item_exp, item_nexp, item_flag, starts, xs2d, w_up, b_up3, w_down, b_down3):
    n_items = item_blk.shape[0]
    rows_blk = pl.BlockSpec((ROW_BLK * ROW_TILES, LANES), lambda i, b, e, x, f, s: (b[i], 0))
    grid_spec = pltpu.PrefetchScalarGridSpec(
        num_scalar_prefetch=5,
        grid=(n_items,),
        in_specs=[
            rows_blk,
            pl.BlockSpec(memory_space=pl.ANY),
            pl.BlockSpec((1, 1, 2 * D_FF), lambda i, b, e, x, f, s: (e[i], 0, 0)),
            pl.BlockSpec(memory_space=pl.ANY),
            pl.BlockSpec((1, 1, D_MODEL), lambda i, b, e, x, f, s: (e[i], 0, 0)),
        ],
        out_specs=rows_blk,
        scratch_shapes=[
            pltpu.VMEM((D_MODEL, 2 * D_FF), BF16),
            pltpu.VMEM((D_FF, D_MODEL), BF16),
            pltpu.VMEM((D_MODEL, 2 * D_FF), F32),
            pltpu.VMEM((D_FF, D_MODEL), F32),
            pltpu.SemaphoreType.DMA((2,)),
        ],
    )
    return pl.pallas_call(
        _experts_kernel,
        grid_spec=grid_spec,
        out_shape=jax.ShapeDtypeStruct(xs2d.shape, F32),
        compiler_params=_cparams(1),
        name="experts",
    )(item_blk, item_exp, item_nexp, item_flag, starts, xs2d, w_up, b_up3, w_down, b_down3)


def _build_items(counts, n_rows):
    ends = jnp.cumsum(counts)
    starts = ends - counts
    n_blocks = n_rows // ROW_BLK
    n_items = n_blocks + N_ITEMS_EXTRA
    lo = jnp.arange(n_blocks, dtype=jnp.int32)[:, None] * ROW_BLK
    overlap = jnp.minimum(ends[None, :], lo + ROW_BLK) - jnp.maximum(starts[None, :], lo)
    live = (overlap > 0).reshape(-1)
    n_live = jnp.sum(live.astype(jnp.int32))
    (pos,) = jnp.nonzero(live, size=n_items, fill_value=0)
    pos = pos.astype(jnp.int32)
    k = jnp.arange(n_items, dtype=jnp.int32)
    valid = k < n_live
    pos = jnp.where(valid, pos, pos[jnp.maximum(n_live - 1, 0)])
    blk = pos // N_EXPERTS
    exp = pos % N_EXPERTS
    prev_blk = jnp.concatenate([jnp.full((1,), -1, jnp.int32), blk[:-1]])
    prev_exp = jnp.concatenate([jnp.full((1,), -1, jnp.int32), exp[:-1]])
    new_exp = valid & (exp != prev_exp)
    flags = (jnp.where(valid, FLAG_VALID, 0)
             | jnp.where(valid & (blk != prev_blk), FLAG_FIRST_VISIT, 0)
             | jnp.where(new_exp, FLAG_NEW_EXPERT, 0)).astype(jnp.int32)
    first_at = jnp.where(new_exp, k, n_items)
    next_first = jnp.flip(lax.cummin(jnp.flip(first_at)))
    next_first = jnp.concatenate([next_first[1:], jnp.full((1,), n_items, jnp.int32)])
    nexp = jnp.where(next_first < n_items, exp[jnp.minimum(next_first, n_items - 1)], -1)
    starts33 = jnp.concatenate([starts, ends[-1:]]).astype(jnp.int32)
    return blk, exp, nexp.astype(jnp.int32), flags, starts33


def _combine_kernel(slot_cur_ref, slot_next_ref, ys3_ref, ys2d_ref, x1_ref, gate_ref, p_ref,
                    gpg_ref, wpg_ref, wpp_ref, gpp_ref, out_ref, rows_ref, sem):
    i = pl.program_id(0)
    last = pl.num_programs(0) - 1
    cur = i % 2
    nxt = 1 - cur

    def start(slot_ref, buf):
        for k in range(TOP_K):
            for t in range(TOK_TILE):
                pltpu.make_async_copy(ys3_ref.at[slot_ref[k * TOK_TILE + t]],
                                      rows_ref.at[buf, k, pl.ds(t * ROW_TILES, ROW_TILES)],
                                      sem.at[buf]).start(priority=t % 2)

    def wait(buf):
        for k in range(TOP_K):
            pltpu.make_async_copy(ys2d_ref.at[pl.ds(0, TOK_TILE * ROW_TILES)],
                                  rows_ref.at[buf, k], sem.at[buf]).wait()

    @pl.when(i == 0)
    def _():
        start(slot_cur_ref, 0)

    wait(cur)
    gate = jnp.concatenate([gate_ref[...], jnp.zeros((LANES - SUBLANES, TOK_TILE), F32)],
                           axis=0).T
    moe = []
    for c in range(ROW_TILES):
        acc = gate[:, 0:1] * _lane_tile(rows_ref.at[cur, 0], c, TOK_TILE)[...]
        for k in range(1, TOP_K):
            acc = acc + gate[:, k:k + 1] * _lane_tile(rows_ref.at[cur, k], c, TOK_TILE)[...]
        moe.append(acc)
    x2 = x1_ref[...] + jnp.concatenate(moe, axis=1)
    start(slot_next_ref, nxt)
    ple = _rms(jnp.dot(p_ref[...].astype(BF16), wpp_ref[...], preferred_element_type=F32),
               gpp_ref[...])
    pg = jax.nn.sigmoid(jnp.dot(_rms(x2, gpg_ref[...]).astype(BF16), wpg_ref[...],
                                preferred_element_type=F32))
    out_ref[...] = x2 + pg * ple

    @pl.when(i == last)
    def _():
        wait(nxt)


def _combine(slots_flat, ys2d, x1, gate, p2d, gpg, wpg, wpp, gpp):
    n = x1.shape[0]
    n_tiles = n // TOK_TILE
    full = lambda shape: pl.BlockSpec(shape, lambda i: (0,) * len(shape))
    row = lambda w: pl.BlockSpec((TOK_TILE, w), lambda i: (i, 0))
    return pl.pallas_call(
        _combine_kernel,
        grid=(n_tiles,),
        in_specs=[
            pl.BlockSpec((TOP_K * TOK_TILE,), lambda i: (i,), memory_space=pltpu.SMEM),
            pl.BlockSpec((TOP_K * TOK_TILE,), lambda i: (jnp.minimum(i + 1, n_tiles - 1),),
                         memory_space=pltpu.SMEM),
            pl.BlockSpec(memory_space=pl.ANY), pl.BlockSpec(memory_space=pl.ANY),
            row(D_MODEL), pl.BlockSpec((SUBLANES, TOK_TILE), lambda i: (0, i)), row(PLE_DIM),
            full((1, D_MODEL)), full((D_MODEL, D_MODEL)), full((PLE_DIM, D_MODEL)),
            full((1, D_MODEL)),
        ],
        out_specs=row(D_MODEL),
        out_shape=jax.ShapeDtypeStruct((n, D_MODEL), F32),
        scratch_shapes=[pltpu.VMEM((2, TOP_K, TOK_TILE * ROW_TILES, LANES), F32),
                        pltpu.SemaphoreType.DMA((2,))],
        compiler_params=_cparams(1),
        name="combine",
    )(slots_flat, slots_flat, ys2d.reshape(-1, ROW_TILES, LANES), ys2d, x1, gate, p2d,
      gpg, wpg, wpp, gpp)


def kernel(x, p, g_mix, w_in, g_q, g_k, w_attn_branch, a_re, a_im, log_dt, b_re, b_im, c_re, c_im, d_skip, w_glu, b_glu, w_out, g_ffn, w_router, b_router, w_up, b_up, w_down, b_down, g_ple_gate, w_ple_gate, w_ple_proj, g_ple_post):
    bsz, s_len, d = x.shape
    depth = w_in.shape[0]
    n = bsz * s_len
    assert d == D_MODEL and s_len % (MERGE_SUBTILES * TOK_TILE) == 0 and s_len % SCAN_CHUNK == 0
    assert bsz == SUBLANES, "the S5 scan keeps the batch on the sublane axis"

    x2d = x.reshape(n, d)
    for i in range(depth):
        qkv, u_tm, gates = _inproj(x2d, g_mix[i][None], w_in[i].astype(BF16), bsz, s_len)
        o = _attn(qkv.reshape(bsz, s_len, 3 * SB_WIDTH),
                  jnp.tile(g_q[i], 2)[None], jnp.tile(g_k[i], 2)[None])
        a_bc, b_cat, c_cat = _s5_params(a_re[i], a_im[i], log_dt[i], b_re[i], b_im[i],
                                        c_re[i], c_im[i], bsz)
        y_tm = _s5(u_tm.reshape(s_len, bsz, SSM_WIDTH), a_bc, b_cat, c_cat, d_skip[i][None])

        wr_t = w_router[i].T
        wr_hi = wr_t.astype(BF16)
        wr_lo = (wr_t - wr_hi.astype(F32)).astype(BF16)
        x1, xn, idx, gate, rank, counts = _merge(
            x2d, o.reshape(n, SB_WIDTH), y_tm.reshape(s_len, bsz * SSM_WIDTH), gates,
            w_attn_branch[i].astype(BF16), w_glu[i].astype(BF16), b_glu[i][None],
            w_out[i].astype(BF16), g_ffn[i][None], wr_hi, wr_lo, b_router[i][:, None], s_len)

        counts_i = counts[:, 0].astype(jnp.int32)
        blk, exp, nexp, flags, starts33 = _build_items(counts_i, n * TOP_K)
        hot = idx[:TOP_K, :, None] == jnp.arange(N_EXPERTS, dtype=jnp.int32)
        slots = jnp.sum(jnp.where(hot, starts33[:N_EXPERTS], 0), axis=-1) + rank[:TOP_K]
        slots_flat = slots.reshape(TOP_K, n // TOK_TILE, TOK_TILE).transpose(1, 0, 2).reshape(-1)

        xs3 = _dispatch(slots_flat, xn.reshape(n, ROW_TILES, LANES))
        ys2d = _experts(blk, exp, nexp, flags, starts33, xs3.reshape(n * TOP_K * ROW_TILES, LANES),
                        w_up[i], b_up[i][:, None, :], w_down[i], b_down[i][:, None, :])
        x2d = _combine(slots_flat, ys2d, x1, gate, p[i].reshape(n, PLE_DIM),
                       g_ple_gate[i][None], w_ple_gate[i].astype(BF16),
                       w_ple_proj[i].astype(BF16), g_ple_post[i][None])
    return x2d.reshape(bsz, s_len, d)
```

```python
import jax
import jax.numpy as jnp
from jax import lax
from jax.experimental import pallas as pl
from jax.experimental.pallas import tpu as pltpu

F32 = jnp.float32
BF16 = jnp.bfloat16

D_MODEL = 1024
SB_HEADS = 8
SB_HEAD_DIM = 64
SB_WIDTH = SB_HEADS * SB_HEAD_DIM
SSM_GROUP = 16
SSM_WIDTH = 512
SSM_GROUPS = SSM_WIDTH // SSM_GROUP
SSM_STATE = 64
PLE_DIM = 256
N_EXPERTS = 32
TOP_K = 4
D_FF = D_MODEL
SWIGLU_LIMIT = 7.0
SWIGLU_ALPHA = 1.702
EPS = 1e-6

LANES = 128
SUBLANES = 8
VMEM_LIMIT = 56 * 1024 * 1024

TOK_TILE = 256
TILE_GROUPS = TOK_TILE // SUBLANES
ROW_TILES = D_MODEL // LANES
assert ROW_TILES == SUBLANES
MERGE_SUBTILES = 2
ATT_BLK = 128
DEAD_LOG_WEIGHT = -104.0
HEAD_PAIRS = SB_WIDTH // LANES
SCAN_CHUNK = 64
SCAN_LANES = 512
HALF_U = SSM_WIDTH // 2
HALF_STATE = SSM_GROUPS // 2 * SSM_STATE
ROW_BLK = 256
FF_CHUNK = 256
N_ITEMS_EXTRA = N_EXPERTS - 1


def _cparams(n_axes):
    return pltpu.CompilerParams(
        dimension_semantics=("arbitrary",) * n_axes,
        vmem_limit_bytes=VMEM_LIMIT)


def _rms(x, g):
    ms = jnp.mean(x * x, axis=-1, keepdims=True)
    return x * lax.rsqrt(ms + EPS) * g


def _inproj_kernel(x_ref, g_ref, w_ref, gq_ref, gk_ref, qs_ref, k_ref, v_ref, u_ref, gates_ref):
    h = _rms(x_ref[...], g_ref[...]).astype(BF16)
    n_qkv = 3 * SB_WIDTH
    n_u = n_qkv + SSM_WIDTH
    qkv = jnp.dot(h, w_ref[:, :n_qkv], preferred_element_type=F32)
    u_ref[...] = jnp.dot(h, w_ref[:, n_qkv:n_u], preferred_element_type=F32)
    gates_ref[...] = jnp.dot(h, w_ref[:, n_u:], preferred_element_type=F32)

    head0 = lax.broadcasted_iota(jnp.int32, (1, LANES), 1) < SB_HEAD_DIM

    def head_rms(t, g):
        sq = t * t
        s0 = jnp.sum(jnp.where(head0, sq, 0.0), axis=-1, keepdims=True)
        s1 = jnp.sum(jnp.where(head0, 0.0, sq), axis=-1, keepdims=True)
        ms = jnp.where(head0, s0, s1) * (1.0 / SB_HEAD_DIM)
        return t * lax.rsqrt(ms + EPS) * g

    scale = SB_HEAD_DIM ** -0.5
    for p in range(HEAD_PAIRS):
        cols = slice(p * LANES, (p + 1) * LANES)
        qn = head_rms(qkv[:, cols], gq_ref[...]) * scale
        q0 = jnp.where(head0, qn, 0.0).astype(BF16)
        q1 = jnp.where(head0, 0.0, qn).astype(BF16)
        for t in range(TOK_TILE // ATT_BLK):
            rows = slice(t * ATT_BLK, (t + 1) * ATT_BLK)
            qs_ref[0, p, t, :ATT_BLK, :] = q0[rows]
            qs_ref[0, p, t, ATT_BLK:, :] = q1[rows]
        kcols = slice(SB_WIDTH + p * LANES, SB_WIDTH + (p + 1) * LANES)
        k_ref[:, cols] = head_rms(qkv[:, kcols], gk_ref[...]).astype(BF16)
    v_ref[...] = qkv[:, 2 * SB_WIDTH:].astype(BF16)


def _inproj(x2d, g_mix, w_in_bf, gq2, gk2, bsz, s_len):
    n = x2d.shape[0]
    tiles_per_seq = s_len // TOK_TILE
    blks_per_tile = TOK_TILE // ATT_BLK
    in_cols = w_in_bf.shape[1]
    return pl.pallas_call(
        _inproj_kernel,
        grid=(n // TOK_TILE,),
        in_specs=[
            pl.BlockSpec((TOK_TILE, D_MODEL), lambda i: (i, 0)),
            pl.BlockSpec((1, D_MODEL), lambda i: (0, 0)),
            pl.BlockSpec((D_MODEL, in_cols), lambda i: (0, 0)),
            pl.BlockSpec((1, LANES), lambda i: (0, 0)),
            pl.BlockSpec((1, LANES), lambda i: (0, 0)),
        ],
        out_specs=[
            pl.BlockSpec((1, HEAD_PAIRS, blks_per_tile, 2 * ATT_BLK, LANES),
                         lambda i: (i // tiles_per_seq, 0, i % tiles_per_seq, 0, 0)),
            pl.BlockSpec((TOK_TILE, SB_WIDTH), lambda i: (i, 0)),
            pl.BlockSpec((TOK_TILE, SB_WIDTH), lambda i: (i, 0)),
            pl.BlockSpec((TOK_TILE, SSM_WIDTH),
                         lambda i: (i % tiles_per_seq, i // tiles_per_seq)),
            pl.BlockSpec((TOK_TILE, 2 * D_MODEL), lambda i: (i, 0)),
        ],
        out_shape=[
            jax.ShapeDtypeStruct((bsz, HEAD_PAIRS, s_len // ATT_BLK, 2 * ATT_BLK, LANES), BF16),
            jax.ShapeDtypeStruct((n, SB_WIDTH), BF16),
            jax.ShapeDtypeStruct((n, SB_WIDTH), BF16),
            jax.ShapeDtypeStruct((s_len, bsz * SSM_WIDTH), F32),
            jax.ShapeDtypeStruct((n, 2 * D_MODEL), F32),
        ],
        compiler_params=_cparams(1),
        name="inproj",
    )(x2d, g_mix, w_in_bf, gq2, gk2)


def _attn_kernel(qs_ref, ks_ref, vs_ref, o_ref, w2_ref, c_ref, acc_ref):
    s_len = ks_ref.shape[1]
    n_blk = s_len // ATT_BLK
    n_pairs = HEAD_PAIRS
    head0 = lax.broadcasted_iota(jnp.int32, (1, LANES), 1) < SB_HEAD_DIM

    r = lax.broadcasted_iota(jnp.int32, (2 * ATT_BLK, 2 * ATT_BLK), 0)
    c = lax.broadcasted_iota(jnp.int32, (2 * ATT_BLK, 2 * ATT_BLK), 1)
    r = jnp.where(r >= ATT_BLK, r - ATT_BLK, r)
    w2_ref[...] = jnp.where((c >= ATT_BLK) | (r > c), 1.0, 0.0).astype(BF16)

    ti = lax.broadcasted_iota(jnp.int32, (2 * ATT_BLK, ATT_BLK), 0)
    si = lax.broadcasted_iota(jnp.int32, (2 * ATT_BLK, ATT_BLK), 1)
    causal = si < jnp.where(ti >= ATT_BLK, ti - ATT_BLK, ti)

    def rows_of(blk):
        if isinstance(blk, int):
            return pl.ds(blk * ATT_BLK, ATT_BLK)
        return pl.ds(pl.multiple_of(blk * ATT_BLK, ATT_BLK), ATT_BLK)

    def tiles(sweeps, diag):
        chains = [(w, qi, rows_of(kv), p) for (w, qi, kv) in sweeps for p in range(n_pairs)]
        z = [lax.dot_general(qs_ref[0, p, qi], ks_ref[0, kv_rows, p * LANES:(p + 1) * LANES],
                             (((1,), (1,)), ((), ())),
                             preferred_element_type=F32)
             for (w, qi, kv_rows, p) in chains]
        log_beta, stacked = [], []
        for zi in z:
            sp = jnp.maximum(zi, 0.0) + jnp.log(1.0 + jnp.exp(-jnp.abs(zi)))
            log_keep = -sp
            log_beta.append(zi - sp)
            if diag:
                log_keep = jnp.where(causal, log_keep, 0.0)
            hi = log_keep.astype(BF16)
            lo = (log_keep - hi.astype(F32)).astype(BF16)
            stacked.append(jnp.concatenate([hi, lo], axis=1))
        sums = [jnp.dot(st, w2_ref[...], preferred_element_type=F32) for st in stacked]
        weights = []
        for i, (w, qi, kv_rows, p) in enumerate(chains):
            wp = jnp.exp(log_beta[i] + sums[i][:, :ATT_BLK] + c_ref[w, p])
            if diag:
                wp = jnp.where(causal, wp, 0.0)
            weights.append(wp.astype(BF16))
            c_ref[w, p] += sums[i][:, ATT_BLK:]
        pv = [jnp.dot(weights[i], vs_ref[0, kv_rows, p * LANES:(p + 1) * LANES],
                      preferred_element_type=F32)
              for i, (w, qi, kv_rows, p) in enumerate(chains)]
        for i, (w, qi, kv_rows, p) in enumerate(chains):
            acc_ref[w, p] += pv[i]

    def qblocks(qb, _):
        qa = 2 * qb
        qc = qa + 1
        c_ref[...] = jnp.zeros_like(c_ref)
        acc_ref[...] = jnp.zeros_like(acc_ref)
        tiles([(0, qa, qa), (1, qc, qc)], True)

        def live(carry):
            jj, c_max = carry
            return (jj < qa) & (c_max > DEAD_LOG_WEIGHT)

        def kvblock(carry):
            jj, _ = carry
            tiles([(0, qa, qa - 1 - jj), (1, qc, qa - jj)], False)
            return jj + 1, jnp.max(c_ref[...])

        jj, _ = lax.while_loop(live, kvblock, (jnp.int32(0), jnp.max(c_ref[...])))

        @pl.when((jj == qa) & (jnp.max(c_ref[1]) > DEAD_LOG_WEIGHT))
        def _():
            tiles([(1, qc, 0)], False)

        for w, qi in ((0, qa), (1, qc)):
            for p in range(n_pairs):
                o_ref[0, rows_of(qi), p * LANES:(p + 1) * LANES] = jnp.where(
                    head0, acc_ref[w, p, :ATT_BLK, :], acc_ref[w, p, ATT_BLK:, :])
        return 0

    lax.fori_loop(0, n_blk // 2, qblocks, 0)


def _attn(qs, k3, v3):
    bsz, s_len, _ = k3.shape
    kv_blk = pl.BlockSpec((1, s_len, SB_WIDTH), lambda b: (b, 0, 0))
    return pl.pallas_call(
        _attn_kernel,
        grid=(bsz,),
        in_specs=[
            pl.BlockSpec((1,) + qs.shape[1:], lambda b: (b, 0, 0, 0, 0)),
            kv_blk, kv_blk,
        ],
        out_specs=pl.BlockSpec((1, s_len, SB_WIDTH), lambda b: (b, 0, 0)),
        out_shape=jax.ShapeDtypeStruct((bsz, s_len, SB_WIDTH), F32),
        scratch_shapes=[
            pltpu.VMEM((2 * ATT_BLK, 2 * ATT_BLK), BF16),
            pltpu.VMEM((2, HEAD_PAIRS, 2 * ATT_BLK, ATT_BLK), F32),
            pltpu.VMEM((2, HEAD_PAIRS, 2 * ATT_BLK, LANES), F32),
        ],
        compiler_params=_cparams(1),
        name="attn",
    )(qs, k3, v3)


def _s5_kernel(u_ref, a_ref, b_ref, c_ref, d_ref, y_ref, hbuf_ref, state_ref):
    tc, bsz, _ = u_ref.shape
    rows = tc * bsz

    @pl.when(pl.program_id(0) == 0)
    def _():
        state_ref[...] = jnp.zeros_like(state_ref)

    u2 = u_ref[...].reshape(rows, SSM_WIDTH)
    ub = u2.astype(BF16)
    halves = range(2)
    for hf in halves:
        uh = ub[:, hf * HALF_U:(hf + 1) * HALF_U]
        xh = jnp.dot(uh, b_ref[hf], preferred_element_type=F32)
        hbuf_ref[hf] = xh.reshape(tc, bsz, 2 * HALF_STATE)
    for hf in halves:
        for lc in range(HALF_STATE // SCAN_LANES):
            re = pl.ds(lc * SCAN_LANES, SCAN_LANES)
            im = pl.ds(HALF_STATE + lc * SCAN_LANES, SCAN_LANES)
            ar = a_ref[hf, :, re]
            ai = a_ref[hf, :, im]
            hr = state_ref[hf, :, re]
            hi = state_ref[hf, :, im]
            for t in range(tc):
                hr, hi = (ar * hr - ai * hi + hbuf_ref[hf, t, :, re],
                          ar * hi + ai * hr + hbuf_ref[hf, t, :, im])
                hbuf_ref[hf, t, :, re] = hr
                hbuf_ref[hf, t, :, im] = hi
            state_ref[hf, :, re] = hr
            state_ref[hf, :, im] = hi
        hb = hbuf_ref[hf].reshape(rows, 2 * HALF_STATE).astype(BF16)
        yh = jnp.dot(hb, c_ref[hf], preferred_element_type=F32)
        cols = slice(hf * HALF_U, (hf + 1) * HALF_U)
        yh = yh + d_ref[:, cols] * u2[:, cols]
        y_ref[:, :, cols] = yh.reshape(tc, bsz, HALF_U)


def _s5(u_tm3, a_bc, b_cat, c_cat, d_skip):
    s_len, bsz, _ = u_tm3.shape
    return pl.pallas_call(
        _s5_kernel,
        grid=(s_len // SCAN_CHUNK,),
        in_specs=[
            pl.BlockSpec((SCAN_CHUNK, bsz, SSM_WIDTH), lambda c: (c, 0, 0)),
            pl.BlockSpec((2, bsz, 2 * HALF_STATE), lambda c: (0, 0, 0)),
            pl.BlockSpec((2, HALF_U, 2 * HALF_STATE), lambda c: (0, 0, 0)),
            pl.BlockSpec((2, 2 * HALF_STATE, HALF_U), lambda c: (0, 0, 0)),
            pl.BlockSpec((1, SSM_WIDTH), lambda c: (0, 0)),
        ],
        out_specs=pl.BlockSpec((SCAN_CHUNK, bsz, SSM_WIDTH), lambda c: (c, 0, 0)),
        out_shape=jax.ShapeDtypeStruct((s_len, bsz, SSM_WIDTH), F32),
        scratch_shapes=[
            pltpu.VMEM((2, SCAN_CHUNK, bsz, 2 * HALF_STATE), F32),
            pltpu.VMEM((2, bsz, 2 * HALF_STATE), F32),
        ],
        compiler_params=_cparams(1),
        name="s5",
    )(u_tm3, a_bc, b_cat, c_cat, d_skip)


def _s5_params(a_re, a_im, log_dt, b_re, b_im, c_re, c_im, bsz):
    dt = jnp.exp(log_dt)[:, None]
    mag = jnp.exp(a_re * dt)
    abar_r = mag * jnp.cos(a_im * dt)
    abar_i = mag * jnp.sin(a_im * dt)
    den = a_re * a_re + a_im * a_im
    nr = abar_r - 1.0
    ni = abar_i
    fr = (nr * a_re + ni * a_im) / den
    fi = (ni * a_re - nr * a_im) / den
    bbar_r = fr[..., None] * b_re - fi[..., None] * b_im
    bbar_i = fr[..., None] * b_im + fi[..., None] * b_re
    gh = SSM_GROUPS // 2
    eye = jnp.eye(gh, dtype=F32)

    def a_half(hf):
        sl = slice(hf * gh, (hf + 1) * gh)
        row = jnp.concatenate([abar_r[sl].reshape(-1), abar_i[sl].reshape(-1)])
        return jnp.broadcast_to(row[None], (bsz, 2 * HALF_STATE))

    def b_half(bb, hf):
        blk = bb[hf * gh:(hf + 1) * gh]
        return jnp.einsum('gpc,gk->gckp', blk, eye).reshape(HALF_U, HALF_STATE)

    def c_half(cc, hf):
        blk = cc[hf * gh:(hf + 1) * gh]
        return jnp.einsum('gcp,gk->gpkc', blk, eye).reshape(HALF_STATE, HALF_U)

    a_bc = jnp.stack([a_half(0), a_half(1)])
    b_cat = jnp.stack([jnp.concatenate([b_half(bbar_r, hf), b_half(bbar_i, hf)], axis=1)
                       for hf in range(2)]).astype(BF16)
    c_cat = jnp.stack([jnp.concatenate([c_half(c_re, hf), -c_half(c_im, hf)], axis=0)
                       for hf in range(2)]).astype(BF16)
    return a_bc, b_cat, c_cat


def _merge_kernel(x_ref, o_ref, y_ref, gates_ref, wab_ref, wglu_ref, bglu_ref, wout_ref,
                  gffn_ref, wrh_ref, wrl_ref, br_ref,
                  x1_ref, xn_ref, idx_ref, gate_ref, rank_ref, cnt_ref, carry_ref):
    @pl.when(pl.program_id(0) == 0)
    def _():
        carry_ref[...] = jnp.zeros_like(carry_ref)

    subs = [pl.ds(s * TOK_TILE, TOK_TILE) for s in range(MERGE_SUBTILES)]
    nt = (((1,), (1,)), ((), ()))

    attn_branch = [jnp.dot(o_ref[s, :].astype(BF16), wab_ref[...], preferred_element_type=F32)
                   for s in subs]
    zg = [jnp.dot(jax.nn.gelu(y_ref[s, :]).astype(BF16), wglu_ref[...],
                  preferred_element_type=F32) + bglu_ref[...] for s in subs]
    mixed = []
    for i, s in enumerate(subs):
        ssm_branch = zg[i][:, :D_MODEL] * jax.nn.sigmoid(zg[i][:, D_MODEL:])
        mixed.append((jax.nn.sigmoid(gates_ref[s, :D_MODEL]) * attn_branch[i]
                      + jax.nn.sigmoid(gates_ref[s, D_MODEL:]) * ssm_branch).astype(BF16))
    x1 = [x_ref[s, :] + jnp.dot(mixed[i], wout_ref[...], preferred_element_type=F32)
          for i, s in enumerate(subs)]
    xh, xl = [], []
    for i, s in enumerate(subs):
        x1_ref[s, :] = x1[i]
        xn = _rms(x1[i], gffn_ref[...])
        for c in range(ROW_TILES):
            xn_ref[pl.ds(i * TOK_TILE * ROW_TILES + c, TOK_TILE, stride=ROW_TILES), :] = (
                xn[:, c * LANES:(c + 1) * LANES])
        xh.append(xn.astype(BF16))
        xl.append((xn - xh[i].astype(F32)).astype(BF16))
    logits = [(lax.dot_general(wrh_ref[...], xh[i], nt, preferred_element_type=F32)
               + lax.dot_general(wrh_ref[...], xl[i], nt, preferred_element_type=F32)
               + lax.dot_general(wrl_ref[...], xh[i], nt, preferred_element_type=F32))
              + br_ref[...] for i in range(MERGE_SUBTILES)]

    e_iota = lax.broadcasted_iota(jnp.int32, (N_EXPERTS, TOK_TILE), 0).astype(F32)
    k_iota = lax.broadcasted_iota(jnp.int32, (SUBLANES, TOK_TILE), 0)
    rr = lax.broadcasted_iota(jnp.int32, (TOK_TILE, TOK_TILE), 0)
    cc = lax.broadcasted_iota(jnp.int32, (TOK_TILE, TOK_TILE), 1)
    earlier = jnp.where(rr < cc, 1.0, 0.0).astype(BF16)

    for i, s in enumerate(subs):
        work = logits[i]
        hits, vals = [], []
        sel = jnp.zeros((N_EXPERTS, TOK_TILE), F32)
        for _ in range(TOP_K):
            m = jnp.max(work, axis=0, keepdims=True)
            pick = jnp.min(jnp.where(work == m, e_iota, float(N_EXPERTS)), axis=0, keepdims=True)
            hit = e_iota == pick
            work = jnp.where(hit, -jnp.inf, work)
            sel = sel + jnp.where(hit, 1.0, 0.0)
            hits.append((hit, pick))
            vals.append(m)
        before = jnp.dot(sel.astype(BF16), earlier, preferred_element_type=F32) + carry_ref[...]
        exps = [jnp.exp(v - vals[0]) for v in vals]
        denom = exps[0] + exps[1] + exps[2] + exps[3]
        idx = jnp.zeros((SUBLANES, TOK_TILE), jnp.int32)
        gate = jnp.zeros((SUBLANES, TOK_TILE), F32)
        rank = jnp.zeros((SUBLANES, TOK_TILE), jnp.int32)
        for k in range(TOP_K):
            hit, pick = hits[k]
            rk = jnp.sum(jnp.where(hit, before, 0.0), axis=0, keepdims=True)
            idx = jnp.where(k_iota == k, pick.astype(jnp.int32), idx)
            gate = jnp.where(k_iota == k, exps[k] / denom, gate)
            rank = jnp.where(k_iota == k, rk.astype(jnp.int32), rank)
        idx_ref[:, s] = idx
        gate_ref[:, s] = gate
        rank_ref[:, s] = rank
        carry_ref[...] += jnp.sum(sel, axis=1, keepdims=True)
    cnt_ref[...] = jnp.broadcast_to(carry_ref[...], cnt_ref.shape)


def _merge(x2d, o2d, y_tm2, gates, wab, wglu, bglu, wout, gffn, wrh_t, wrl_t, br_col, s_len):
    n = x2d.shape[0]
    tile = MERGE_SUBTILES * TOK_TILE
    tiles_per_seq = s_len // tile
    full = lambda shape: pl.BlockSpec(shape, lambda i: (0,) * len(shape))
    row = lambda w: pl.BlockSpec((tile, w), lambda i: (i, 0))
    col = pl.BlockSpec((SUBLANES, tile), lambda i: (0, i))
    return pl.pallas_call(
        _merge_kernel,
        grid=(n // tile,),
        in_specs=[
            row(D_MODEL), row(SB_WIDTH),
            pl.BlockSpec((tile, SSM_WIDTH),
                         lambda i: (i % tiles_per_seq, i // tiles_per_seq)),
            row(2 * D_MODEL),
            full((SB_WIDTH, D_MODEL)), full((SSM_WIDTH, 2 * D_MODEL)), full((1, 2 * D_MODEL)),
            full((D_MODEL, D_MODEL)), full((1, D_MODEL)),
            full((N_EXPERTS, D_MODEL)), full((N_EXPERTS, D_MODEL)), full((N_EXPERTS, 1)),
        ],
        out_specs=[row(D_MODEL), pl.BlockSpec((tile * ROW_TILES, LANES), lambda i: (i, 0)),
                   col, col, col, full((N_EXPERTS, LANES))],
        out_shape=[
            jax.ShapeDtypeStruct((n, D_MODEL), F32),
            jax.ShapeDtypeStruct((n * ROW_TILES, LANES), F32),
            jax.ShapeDtypeStruct((SUBLANES, n), jnp.int32),
            jax.ShapeDtypeStruct((SUBLANES, n), F32),
            jax.ShapeDtypeStruct((SUBLANES, n), jnp.int32),
            jax.ShapeDtypeStruct((N_EXPERTS, LANES), F32),
        ],
        scratch_shapes=[pltpu.VMEM((N_EXPERTS, 1), F32)],
        compiler_params=_cparams(1),
        name="merge",
    )(x2d, o2d, y_tm2, gates, wab, wglu, bglu, wout, gffn, wrh_t, wrl_t, br_col)


def _lane_tile(ref2d, c, n_rows):
    return ref2d.at[pl.ds(c, n_rows, stride=ROW_TILES), :]


def _load_row_tiled(ref2d, n_rows):
    return jnp.concatenate([_lane_tile(ref2d, c, n_rows)[...] for c in range(ROW_TILES)], axis=1)


def _dispatch_kernel(slot_ref, xn_ref, xs_ref, sem):
    def issue(g, _):
        for k in range(TOP_K):
            for j in range(SUBLANES):
                t = g * SUBLANES + j
                pltpu.make_async_copy(xn_ref.at[t], xs_ref.at[slot_ref[k * TOK_TILE + t]],
                                      sem).start(priority=j % 2)
        return 0

    lax.fori_loop(0, TILE_GROUPS, issue, 0)
    for _ in range(TOP_K):
        pltpu.make_async_copy(xn_ref, xs_ref.at[pl.ds(0, TOK_TILE)], sem).wait()


def _dispatch(slots_flat, xn3):
    n = xn3.shape[0]
    return pl.pallas_call(
        _dispatch_kernel,
        grid=(n // TOK_TILE,),
        in_specs=[
            pl.BlockSpec((TOP_K * TOK_TILE,), lambda i: (i,), memory_space=pltpu.SMEM),
            pl.BlockSpec((TOK_TILE, ROW_TILES, LANES), lambda i: (i, 0, 0)),
        ],
        out_specs=pl.BlockSpec(memory_space=pl.ANY),
        out_shape=jax.ShapeDtypeStruct((n * TOP_K, ROW_TILES, LANES), F32),
        scratch_shapes=[pltpu.SemaphoreType.DMA],
        compiler_params=_cparams(1),
        name="dispatch",
    )(slots_flat, xn3)


FLAG_VALID, FLAG_FIRST_VISIT, FLAG_NEW_EXPERT = 1, 2, 4


def _experts_kernel(blk_ref, exp_ref, nexp_ref, flag_ref, start_ref,
                    xs_ref, wup_hbm_ref, bup_ref, wdn_hbm_ref, bdn_ref, ys_ref,
                    wup_bf_ref, wdn_bf_ref, wup_f32_ref, wdn_f32_ref, keep_ref, wsem):
    i = pl.program_id(0)
    flags = flag_ref[i]

    def weight_copies(e):
        return (pltpu.make_async_copy(wup_hbm_ref.at[e], wup_f32_ref, wsem.at[0]),
                pltpu.make_async_copy(wdn_hbm_ref.at[e], wdn_f32_ref, wsem.at[1]))

    @pl.when(i == 0)
    def _():
        for cp in weight_copies(exp_ref[0]):
            cp.start(priority=1)

    @pl.when((flags & FLAG_NEW_EXPERT) != 0)
    def _():
        for cp in weight_copies(exp_ref[i]):
            cp.wait()
        wup_bf_ref[...] = wup_f32_ref[...].astype(BF16)
        wdn_bf_ref[...] = wdn_f32_ref[...].astype(BF16)
        nxt = nexp_ref[i]

        @pl.when(nxt >= 0)
        def _():
            for cp in weight_copies(nxt):
                cp.start(priority=1)

    revisit = ((flags & FLAG_VALID) != 0) & ((flags & FLAG_FIRST_VISIT) == 0)

    @pl.when(revisit)
    def _():
        keep_ref[...] = _load_row_tiled(ys_ref, ROW_BLK)

    @pl.when((flags & FLAG_VALID) != 0)
    def _():
        x = _load_row_tiled(xs_ref, ROW_BLK).astype(BF16)
        acts = []
        for c in range(D_FF // FF_CHUNK):
            gcols = pl.ds(c * FF_CHUNK, FF_CHUNK)
            ucols = pl.ds(D_FF + c * FF_CHUNK, FF_CHUNK)
            g = jnp.dot(x, wup_bf_ref[:, gcols], preferred_element_type=F32) + bup_ref[0, :, gcols]
            up = jnp.dot(x, wup_bf_ref[:, ucols], preferred_element_type=F32) + bup_ref[0, :, ucols]
            g = jnp.minimum(g, SWIGLU_LIMIT)
            up = jnp.clip(up, -SWIGLU_LIMIT, SWIGLU_LIMIT)
            acts.append(((up + 1.0) * g * jax.nn.sigmoid(SWIGLU_ALPHA * g)).astype(BF16))
        y = jnp.dot(jnp.concatenate(acts, axis=1), wdn_bf_ref[...],
                    preferred_element_type=F32) + bdn_ref[0]
        for c in range(ROW_TILES):
            _lane_tile(ys_ref, c, ROW_BLK)[...] = y[:, c * LANES:(c + 1) * LANES]

    @pl.when(revisit)
    def _():
        row = blk_ref[i] * ROW_BLK + lax.broadcasted_iota(jnp.int32, (ROW_BLK, 1), 0)
        earlier = jnp.broadcast_to(row < start_ref[exp_ref[i]], (ROW_BLK, LANES))
        for c in range(ROW_TILES):
            tile = _lane_tile(ys_ref, c, ROW_BLK)
            tile[...] = jnp.where(earlier, keep_ref[:, c * LANES:(c + 1) * LANES], tile[...])


def _experts(item_blk, item_exp, item_nexp, item_flag, starts, xs2d, w_up, b_up3, w_down, b_down3):
    n_items = item_blk.shape[0]
    rows_blk = pl.BlockSpec((ROW_BLK * ROW_TILES, LANES), lambda i, b, e, x, f, s: (b[i], 0))
    grid_spec = pltpu.PrefetchScalarGridSpec(
        num_scalar_prefetch=5,
        grid=(n_items,),
        in_specs=[
            rows_blk,
            pl.BlockSpec(memory_space=pl.ANY),
            pl.BlockSpec((1, 1, 2 * D_FF), lambda i, b, e, x, f, s: (e[i], 0, 0)),
            pl.BlockSpec(memory_space=pl.ANY),
            pl.BlockSpec((1, 1, D_MODEL), lambda i, b, e, x, f, s: (e[i], 0, 0)),
        ],
        out_specs=rows_blk,
        scratch_shapes=[
            pltpu.VMEM((D_MODEL, 2 * D_FF), BF16),
            pltpu.VMEM((D_FF, D_MODEL), BF16),
            pltpu.VMEM((D_MODEL, 2 * D_FF), F32),
            pltpu.VMEM((D_FF, D_MODEL), F32),
            pltpu.VMEM((ROW_BLK, D_MODEL), F32),
            pltpu.SemaphoreType.DMA((2,)),
        ],
    )
    return pl.pallas_call(
        _experts_kernel,
        grid_spec=grid_spec,
        out_shape=jax.ShapeDtypeStruct(xs2d.shape, F32),
        compiler_params=_cparams(1),
        name="experts",
    )(item_blk, item_exp, item_nexp, item_flag, starts, xs2d, w_up, b_up3, w_down, b_down3)


def _build_items(counts, n_rows):
    ends = jnp.cumsum(counts)
    starts = ends - counts
    n_blocks = n_rows // ROW_BLK
    n_items = n_blocks + N_ITEMS_EXTRA
    lo = jnp.arange(n_blocks, dtype=jnp.int32)[:, None] * ROW_BLK
    overlap = jnp.minimum(ends[None, :], lo + ROW_BLK) - jnp.maximum(starts[None, :], lo)
    live = (overlap > 0).reshape(-1)
    n_live = jnp.sum(live.astype(jnp.int32))
    (pos,) = jnp.nonzero(live, size=n_items, fill_value=0)
    pos = pos.astype(jnp.int32)
    k = jnp.arange(n_items, dtype=jnp.int32)
    valid = k < n_live
    pos = jnp.where(valid, pos, pos[jnp.maximum(n_live - 1, 0)])
    blk = pos // N_EXPERTS
    exp = pos % N_EXPERTS
    prev_blk = jnp.concatenate([jnp.full((1,), -1, jnp.int32), blk[:-1]])
    prev_exp = jnp.concatenate([jnp.full((1,), -1, jnp.int32), exp[:-1]])
    new_exp = valid & (exp != prev_exp)
    flags = (jnp.where(valid, FLAG_VALID, 0)
             | jnp.where(valid & (blk != prev_blk), FLAG_FIRST_VISIT, 0)
             | jnp.where(new_exp, FLAG_NEW_EXPERT, 0)).astype(jnp.int32)
    first_at = jnp.where(new_exp, k, n_items)
    next_first = jnp.flip(lax.cummin(jnp.flip(first_at)))
    next_first = jnp.concatenate([next_first[1:], jnp.full((1,), n_items, jnp.int32)])
    nexp = jnp.where(next_first < n_items, exp[jnp.minimum(next_first, n_items - 1)], -1)
    starts33 = jnp.concatenate([starts, ends[-1:]]).astype(jnp.int32)
    return blk, exp, nexp.astype(jnp.int32), flags, starts33


def _combine_kernel(slot_cur_ref, slot_next_ref, ys3_ref, ys2d_ref, x1_ref, gate_ref, p_ref,
                    gpg_ref, wpg_ref, wpp_ref, gpp_ref, out_ref, rows_ref, sem):
    i = pl.program_id(0)
    last = pl.num_programs(0) - 1
    cur = i % 2
    nxt = 1 - cur

    def start(slot_ref, buf):
        for k in range(TOP_K):
            for t in range(TOK_TILE):
                pltpu.make_async_copy(ys3_ref.at[slot_ref[k * TOK_TILE + t]],
                                      rows_ref.at[buf, k, pl.ds(t * ROW_TILES, ROW_TILES)],
                                      sem.at[buf]).start(priority=t % 2)

    def wait(buf):
        for k in range(TOP_K):
            pltpu.make_async_copy(ys2d_ref.at[pl.ds(0, TOK_TILE * ROW_TILES)],
                                  rows_ref.at[buf, k], sem.at[buf]).wait()

    @pl.when(i == 0)
    def _():
        start(slot_cur_ref, 0)

    wait(cur)
    gate = jnp.concatenate([gate_ref[...], jnp.zeros((LANES - SUBLANES, TOK_TILE), F32)],
                           axis=0).T
    moe = []
    for c in range(ROW_TILES):
        acc = gate[:, 0:1] * _lane_tile(rows_ref.at[cur, 0], c, TOK_TILE)[...]
        for k in range(1, TOP_K):
            acc = acc + gate[:, k:k + 1] * _lane_tile(rows_ref.at[cur, k], c, TOK_TILE)[...]
        moe.append(acc)
    x2 = x1_ref[...] + jnp.concatenate(moe, axis=1)
    start(slot_next_ref, nxt)
    ple = _rms(jnp.dot(p_ref[...].astype(BF16), wpp_ref[...], preferred_element_type=F32),
               gpp_ref[...])
    pg = jax.nn.sigmoid(jnp.dot(_rms(x2, gpg_ref[...]).astype(BF16), wpg_ref[...],
                                preferred_element_type=F32))
    out_ref[...] = x2 + pg * ple

    @pl.when(i == last)
    def _():
        wait(nxt)


def _combine(slots_flat, ys2d, x1, gate, p2d, gpg, wpg, wpp, gpp):
    n = x1.shape[0]
    n_tiles = n // TOK_TILE
    full = lambda shape: pl.BlockSpec(shape, lambda i: (0,) * len(shape))
    row = lambda w: pl.BlockSpec((TOK_TILE, w), lambda i: (i, 0))
    return pl.pallas_call(
        _combine_kernel,
        grid=(n_tiles,),
        in_specs=[
            pl.BlockSpec((TOP_K * TOK_TILE,), lambda i: (i,), memory_space=pltpu.SMEM),
            pl.BlockSpec((TOP_K * TOK_TILE,), lambda i: (jnp.minimum(i + 1, n_tiles - 1),),
                         memory_space=pltpu.SMEM),
            pl.BlockSpec(memory_space=pl.ANY), pl.BlockSpec(memory_space=pl.ANY),
            row(D_MODEL), pl.BlockSpec((SUBLANES, TOK_TILE), lambda i: (0, i)), row(PLE_DIM),
            full((1, D_MODEL)), full((D_MODEL, D_MODEL)), full((PLE_DIM, D_MODEL)),
            full((1, D_MODEL)),
        ],
        out_specs=row(D_MODEL),
        out_shape=jax.ShapeDtypeStruct((n, D_MODEL), F32),
        scratch_shapes=[pltpu.VMEM((2, TOP_K, TOK_TILE * ROW_TILES, LANES), F32),
                        pltpu.SemaphoreType.DMA((2,))],
        compiler_params=_cparams(1),
        name="combine",
    )(slots_flat, slots_flat, ys2d.reshape(-1, ROW_TILES, LANES), ys2d, x1, gate, p2d,
      gpg, wpg, wpp, gpp)


def kernel(x, p, g_mix, w_in, g_q, g_k, w_attn_branch, a_re, a_im, log_dt, b_re, b_im, c_re, c_im, d_skip, w_glu, b_glu, w_out, g_ffn, w_router, b_router, w_up, b_up, w_down, b_down, g_ple_gate, w_ple_gate, w_ple_proj, g_ple_post):
    bsz, s_len, d = x.shape
    depth = w_in.shape[0]
    n = bsz * s_len
    assert d == D_MODEL and s_len % (MERGE_SUBTILES * TOK_TILE) == 0 and s_len % SCAN_CHUNK == 0
    assert bsz == SUBLANES, "the S5 scan keeps the batch on the sublane axis"

    x2d = x.reshape(n, d)
    for i in range(depth):
        qs, kn, vb, u_tm, gates = _inproj(x2d, g_mix[i][None], w_in[i].astype(BF16),
                                          jnp.tile(g_q[i], 2)[None], jnp.tile(g_k[i], 2)[None],
                                          bsz, s_len)
        o = _attn(qs, kn.reshape(bsz, s_len, SB_WIDTH), vb.reshape(bsz, s_len, SB_WIDTH))
        a_bc, b_cat, c_cat = _s5_params(a_re[i], a_im[i], log_dt[i], b_re[i], b_im[i],
                                        c_re[i], c_im[i], bsz)
        y_tm = _s5(u_tm.reshape(s_len, bsz, SSM_WIDTH), a_bc, b_cat, c_cat, d_skip[i][None])

        wr_t = w_router[i].T
        wr_hi = wr_t.astype(BF16)
        wr_lo = (wr_t - wr_hi.astype(F32)).astype(BF16)
        x1, xn, idx, gate, rank, counts = _merge(
            x2d, o.reshape(n, SB_WIDTH), y_tm.reshape(s_len, bsz * SSM_WIDTH), gates,
            w_attn_branch[i].astype(BF16), w_glu[i].astype(BF16), b_glu[i][None],
            w_out[i].astype(BF16), g_ffn[i][None], wr_hi, wr_lo, b_router[i][:, None], s_len)

        counts_i = counts[:, 0].astype(jnp.int32)
        blk, exp, nexp, flags, starts33 = _build_items(counts_i, n * TOP_K)
        hot = idx[:TOP_K, :, None] == jnp.arange(N_EXPERTS, dtype=jnp.int32)
        slots = jnp.sum(jnp.where(hot, starts33[:N_EXPERTS], 0), axis=-1) + rank[:TOP_K]
        slots_flat = slots.reshape(TOP_K, n // TOK_TILE, TOK_TILE).transpose(1, 0, 2).reshape(-1)

        xs3 = _dispatch(slots_flat, xn.reshape(n, ROW_TILES, LANES))
        ys2d = _experts(blk, exp, nexp, flags, starts33, xs3.reshape(n * TOP_K * ROW_TILES, LANES),
                        w_up[i], b_up[i][:, None, :], w_down[i], b_down[i][:, None, :])
        x2d = _combine(slots_flat, ys2d, x1, gate, p[i].reshape(n, PLE_DIM),
                       g_ple_gate[i][None], w_ple_gate[i].astype(BF16),
                       w_ple_proj[i].astype(BF16), g_ple_post[i][None])
    return x2d.reshape(bsz, s_len, d)
```

```python
import jax
import jax.numpy as jnp
from jax import lax
from jax.experimental import pallas as pl
from jax.experimental.pallas import tpu as pltpu

F32 = jnp.float32
BF16 = jnp.bfloat16

D_MODEL = 1024
SB_HEADS = 8
SB_HEAD_DIM = 64
SB_WIDTH = SB_HEADS * SB_HEAD_DIM
SSM_GROUP = 16
SSM_WIDTH = 512
SSM_GROUPS = SSM_WIDTH // SSM_GROUP
SSM_STATE = 64
PLE_DIM = 256
N_EXPERTS = 32
TOP_K = 4
D_FF = D_MODEL
SWIGLU_LIMIT = 7.0
SWIGLU_ALPHA = 1.702
EPS = 1e-6

LANES = 128
SUBLANES = 8
VMEM_LIMIT = 56 * 1024 * 1024

TOK_TILE = 256
TILE_GROUPS = TOK_TILE // SUBLANES
ROW_TILES = D_MODEL // LANES
assert ROW_TILES == SUBLANES
MERGE_SUBTILES = 2
ATT_BLK = 128
DEAD_LOG_WEIGHT = -104.0
HEAD_PAIRS = SB_WIDTH // LANES
SCAN_CHUNK = 64
SCAN_LANES = 512
HALF_U = SSM_WIDTH // 2
HALF_STATE = SSM_GROUPS // 2 * SSM_STATE
ROW_BLK = 256
FF_CHUNK = 256
N_ITEMS_EXTRA = N_EXPERTS - 1


def _cparams(n_axes):
    return pltpu.CompilerParams(
        dimension_semantics=("arbitrary",) * n_axes,
        vmem_limit_bytes=VMEM_LIMIT)


def _rms(x, g):
    ms = jnp.mean(x * x, axis=-1, keepdims=True)
    return x * lax.rsqrt(ms + EPS) * g


def _inproj_kernel(x_ref, g_ref, w_ref, gq_ref, gk_ref, qs_ref, k_ref, v_ref, u_ref, gates_ref):
    h = _rms(x_ref[...], g_ref[...]).astype(BF16)
    n_qkv = 3 * SB_WIDTH
    n_u = n_qkv + SSM_WIDTH
    qkv = jnp.dot(h, w_ref[:, :n_qkv], preferred_element_type=F32)
    u_ref[...] = jnp.dot(h, w_ref[:, n_qkv:n_u], preferred_element_type=F32)
    gates_ref[...] = jnp.dot(h, w_ref[:, n_u:], preferred_element_type=F32)

    head0 = lax.broadcasted_iota(jnp.int32, (1, LANES), 1) < SB_HEAD_DIM

    def head_rms(t, g):
        sq = t * t
        s0 = jnp.sum(jnp.where(head0, sq, 0.0), axis=-1, keepdims=True)
        s1 = jnp.sum(jnp.where(head0, 0.0, sq), axis=-1, keepdims=True)
        ms = jnp.where(head0, s0, s1) * (1.0 / SB_HEAD_DIM)
        return t * lax.rsqrt(ms + EPS) * g

    scale = SB_HEAD_DIM ** -0.5
    for p in range(HEAD_PAIRS):
        cols = slice(p * LANES, (p + 1) * LANES)
        qn = head_rms(qkv[:, cols], gq_ref[...]) * scale
        q0 = jnp.where(head0, qn, 0.0).astype(BF16)
        q1 = jnp.where(head0, 0.0, qn).astype(BF16)
        for t in range(TOK_TILE // ATT_BLK):
            rows = slice(t * ATT_BLK, (t + 1) * ATT_BLK)
            qs_ref[0, p, t, :ATT_BLK, :] = q0[rows]
            qs_ref[0, p, t, ATT_BLK:, :] = q1[rows]
        kcols = slice(SB_WIDTH + p * LANES, SB_WIDTH + (p + 1) * LANES)
        k_ref[:, cols] = head_rms(qkv[:, kcols], gk_ref[...]).astype(BF16)
    v_ref[...] = qkv[:, 2 * SB_WIDTH:].astype(BF16)


def _inproj(x2d, g_mix, w_in_bf, gq2, gk2, bsz, s_len):
    n = x2d.shape[0]
    tiles_per_seq = s_len // TOK_TILE
    blks_per_tile = TOK_TILE // ATT_BLK
    in_cols = w_in_bf.shape[1]
    return pl.pallas_call(
        _inproj_kernel,
        grid=(n // TOK_TILE,),
        in_specs=[
            pl.BlockSpec((TOK_TILE, D_MODEL), lambda i: (i, 0)),
            pl.BlockSpec((1, D_MODEL), lambda i: (0, 0)),
            pl.BlockSpec((D_MODEL, in_cols), lambda i: (0, 0)),
            pl.BlockSpec((1, LANES), lambda i: (0, 0)),
            pl.BlockSpec((1, LANES), lambda i: (0, 0)),
        ],
        out_specs=[
            pl.BlockSpec((1, HEAD_PAIRS, blks_per_tile, 2 * ATT_BLK, LANES),
                         lambda i: (i // tiles_per_seq, 0, i % tiles_per_seq, 0, 0)),
            pl.BlockSpec((TOK_TILE, SB_WIDTH), lambda i: (i, 0)),
            pl.BlockSpec((TOK_TILE, SB_WIDTH), lambda i: (i, 0)),
            pl.BlockSpec((TOK_TILE, SSM_WIDTH),
                         lambda i: (i % tiles_per_seq, i // tiles_per_seq)),
            pl.BlockSpec((TOK_TILE, 2 * D_MODEL), lambda i: (i, 0)),
        ],
        out_shape=[
            jax.ShapeDtypeStruct((bsz, HEAD_PAIRS, s_len // ATT_BLK, 2 * ATT_BLK, LANES), BF16),
            jax.ShapeDtypeStruct((n, SB_WIDTH), BF16),
            jax.ShapeDtypeStruct((n, SB_WIDTH), BF16),
            jax.ShapeDtypeStruct((s_len, bsz * SSM_WIDTH), F32),
            jax.ShapeDtypeStruct((n, 2 * D_MODEL), F32),
        ],
        compiler_params=_cparams(1),
        name="inproj",
    )(x2d, g_mix, w_in_bf, gq2, gk2)


def _attn_kernel(qs_ref, ks_ref, vs_ref, o_ref, w2_ref, c_ref, acc_ref):
    s_len = ks_ref.shape[1]
    n_blk = s_len // ATT_BLK
    n_pairs = HEAD_PAIRS
    head0 = lax.broadcasted_iota(jnp.int32, (1, LANES), 1) < SB_HEAD_DIM

    r = lax.broadcasted_iota(jnp.int32, (2 * ATT_BLK, 2 * ATT_BLK), 0)
    c = lax.broadcasted_iota(jnp.int32, (2 * ATT_BLK, 2 * ATT_BLK), 1)
    r = jnp.where(r >= ATT_BLK, r - ATT_BLK, r)
    w2_ref[...] = jnp.where((c >= ATT_BLK) | (r > c), 1.0, 0.0).astype(BF16)

    ti = lax.broadcasted_iota(jnp.int32, (2 * ATT_BLK, ATT_BLK), 0)
    si = lax.broadcasted_iota(jnp.int32, (2 * ATT_BLK, ATT_BLK), 1)
    causal = si < jnp.where(ti >= ATT_BLK, ti - ATT_BLK, ti)

    def rows_of(blk):
        if isinstance(blk, int):
            return pl.ds(blk * ATT_BLK, ATT_BLK)
        return pl.ds(pl.multiple_of(blk * ATT_BLK, ATT_BLK), ATT_BLK)

    def tiles(sweeps, diag):
        chains = [(w, qi, rows_of(kv), p) for (w, qi, kv) in sweeps for p in range(n_pairs)]
        z = [lax.dot_general(qs_ref[0, p, qi], ks_ref[0, kv_rows, p * LANES:(p + 1) * LANES],
                             (((1,), (1,)), ((), ())),
                             preferred_element_type=F32)
             for (w, qi, kv_rows, p) in chains]
        log_beta, stacked = [], []
        for zi in z:
            sp = jnp.maximum(zi, 0.0) + jnp.log(1.0 + jnp.exp(-jnp.abs(zi)))
            log_keep = -sp
            log_beta.append(zi - sp)
            if diag:
                log_keep = jnp.where(causal, log_keep, 0.0)
            hi = log_keep.astype(BF16)
            lo = (log_keep - hi.astype(F32)).astype(BF16)
            stacked.append(jnp.concatenate([hi, lo], axis=1))
        sums = [jnp.dot(st, w2_ref[...], preferred_element_type=F32) for st in stacked]
        weights = []
        for i, (w, qi, kv_rows, p) in enumerate(chains):
            wp = jnp.exp(log_beta[i] + sums[i][:, :ATT_BLK] + c_ref[w, p])
            if diag:
                wp = jnp.where(causal, wp, 0.0)
            weights.append(wp.astype(BF16))
            c_ref[w, p] += sums[i][:, ATT_BLK:]
        pv = [jnp.dot(weights[i], vs_ref[0, kv_rows, p * LANES:(p + 1) * LANES],
                      preferred_element_type=F32)
              for i, (w, qi, kv_rows, p) in enumerate(chains)]
        for i, (w, qi, kv_rows, p) in enumerate(chains):
            acc_ref[w, p] += pv[i]

    def qblocks(qb, _):
        qa = 2 * qb
        qc = qa + 1
        c_ref[...] = jnp.zeros_like(c_ref)
        acc_ref[...] = jnp.zeros_like(acc_ref)
        tiles([(0, qa, qa), (1, qc, qc)], True)

        def live(carry):
            jj, c_max = carry
            return (jj < qa) & (c_max > DEAD_LOG_WEIGHT)

        def kvblock(carry):
            jj, _ = carry
            tiles([(0, qa, qa - 1 - jj), (1, qc, qa - jj)], False)
            return jj + 1, jnp.max(c_ref[...])

        jj, _ = lax.while_loop(live, kvblock, (jnp.int32(0), jnp.max(c_ref[...])))

        @pl.when((jj == qa) & (jnp.max(c_ref[1]) > DEAD_LOG_WEIGHT))
        def _():
            tiles([(1, qc, 0)], False)

        for w, qi in ((0, qa), (1, qc)):
            for p in range(n_pairs):
                o_ref[0, rows_of(qi), p * LANES:(p + 1) * LANES] = jnp.where(
                    head0, acc_ref[w, p, :ATT_BLK, :], acc_ref[w, p, ATT_BLK:, :])
        return 0

    lax.fori_loop(0, n_blk // 2, qblocks, 0)


def _attn(qs, k3, v3):
    bsz, s_len, _ = k3.shape
    kv_blk = pl.BlockSpec((1, s_len, SB_WIDTH), lambda b: (b, 0, 0))
    return pl.pallas_call(
        _attn_kernel,
        grid=(bsz,),
        in_specs=[
            pl.BlockSpec((1,) + qs.shape[1:], lambda b: (b, 0, 0, 0, 0)),
            kv_blk, kv_blk,
        ],
        out_specs=pl.BlockSpec((1, s_len, SB_WIDTH), lambda b: (b, 0, 0)),
        out_shape=jax.ShapeDtypeStruct((bsz, s_len, SB_WIDTH), F32),
        scratch_shapes=[
            pltpu.VMEM((2 * ATT_BLK, 2 * ATT_BLK), BF16),
            pltpu.VMEM((2, HEAD_PAIRS, 2 * ATT_BLK, ATT_BLK), F32),
            pltpu.VMEM((2, HEAD_PAIRS, 2 * ATT_BLK, LANES), F32),
        ],
        compiler_params=_cparams(1),
        name="attn",
    )(qs, k3, v3)


def _s5_kernel(u_ref, a_ref, b_ref, c_ref, d_ref, y_ref, hbuf_ref, state_ref):
    tc, bsz, _ = u_ref.shape
    rows = tc * bsz

    @pl.when(pl.program_id(0) == 0)
    def _():
        state_ref[...] = jnp.zeros_like(state_ref)

    u2 = u_ref[...].reshape(rows, SSM_WIDTH)
    ub = u2.astype(BF16)
    halves = range(2)
    for hf in halves:
        uh = ub[:, hf * HALF_U:(hf + 1) * HALF_U]
        xh = jnp.dot(uh, b_ref[hf], preferred_element_type=F32)
        hbuf_ref[hf] = xh.reshape(tc, bsz, 2 * HALF_STATE)
    for hf in halves:
        for lc in range(HALF_STATE // SCAN_LANES):
            re = pl.ds(lc * SCAN_LANES, SCAN_LANES)
            im = pl.ds(HALF_STATE + lc * SCAN_LANES, SCAN_LANES)
            ar = a_ref[hf, :, re]
            ai = a_ref[hf, :, im]
            hr = state_ref[hf, :, re]
            hi = state_ref[hf, :, im]
            for t in range(tc):
                hr, hi = (ar * hr - ai * hi + hbuf_ref[hf, t, :, re],
                          ar * hi + ai * hr + hbuf_ref[hf, t, :, im])
                hbuf_ref[hf, t, :, re] = hr
                hbuf_ref[hf, t, :, im] = hi
            state_ref[hf, :, re] = hr
            state_ref[hf, :, im] = hi
        hb = hbuf_ref[hf].reshape(rows, 2 * HALF_STATE).astype(BF16)
        yh = jnp.dot(hb, c_ref[hf], preferred_element_type=F32)
        cols = slice(hf * HALF_U, (hf + 1) * HALF_U)
        yh = yh + d_ref[:, cols] * u2[:, cols]
        y_ref[:, :, cols] = yh.reshape(tc, bsz, HALF_U)


def _s5(u_tm3, a_bc, b_cat, c_cat, d_skip):
    s_len, bsz, _ = u_tm3.shape
    return pl.pallas_call(
        _s5_kernel,
        grid=(s_len // SCAN_CHUNK,),
        in_specs=[
            pl.BlockSpec((SCAN_CHUNK, bsz, SSM_WIDTH), lambda c: (c, 0, 0)),
            pl.BlockSpec((2, bsz, 2 * HALF_STATE), lambda c: (0, 0, 0)),
            pl.BlockSpec((2, HALF_U, 2 * HALF_STATE), lambda c: (0, 0, 0)),
            pl.BlockSpec((2, 2 * HALF_STATE, HALF_U), lambda c: (0, 0, 0)),
            pl.BlockSpec((1, SSM_WIDTH), lambda c: (0, 0)),
        ],
        out_specs=pl.BlockSpec((SCAN_CHUNK, bsz, SSM_WIDTH), lambda c: (c, 0, 0)),
        out_shape=jax.ShapeDtypeStruct((s_len, bsz, SSM_WIDTH), F32),
        scratch_shapes=[
            pltpu.VMEM((2, SCAN_CHUNK, bsz, 2 * HALF_STATE), F32),
            pltpu.VMEM((2, bsz, 2 * HALF_STATE), F32),
        ],
        compiler_params=_cparams(1),
        name="s5",
    )(u_tm3, a_bc, b_cat, c_cat, d_skip)


def _s5_params(a_re, a_im, log_dt, b_re, b_im, c_re, c_im, bsz):
    dt = jnp.exp(log_dt)[:, None]
    mag = jnp.exp(a_re * dt)
    abar_r = mag * jnp.cos(a_im * dt)
    abar_i = mag * jnp.sin(a_im * dt)
    den = a_re * a_re + a_im * a_im
    nr = abar_r - 1.0
    ni = abar_i
    fr = (nr * a_re + ni * a_im) / den
    fi = (ni * a_re - nr * a_im) / den
    bbar_r = fr[..., None] * b_re - fi[..., None] * b_im
    bbar_i = fr[..., None] * b_im + fi[..., None] * b_re
    gh = SSM_GROUPS // 2
    eye = jnp.eye(gh, dtype=F32)

    def a_half(hf):
        sl = slice(hf * gh, (hf + 1) * gh)
        row = jnp.concatenate([abar_r[sl].reshape(-1), abar_i[sl].reshape(-1)])
        return jnp.broadcast_to(row[None], (bsz, 2 * HALF_STATE))

    def b_half(bb, hf):
        blk = bb[hf * gh:(hf + 1) * gh]
        return jnp.einsum('gpc,gk->gckp', blk, eye).reshape(HALF_U, HALF_STATE)

    def c_half(cc, hf):
        blk = cc[hf * gh:(hf + 1) * gh]
        return jnp.einsum('gcp,gk->gpkc', blk, eye).reshape(HALF_STATE, HALF_U)

    a_bc = jnp.stack([a_half(0), a_half(1)])
    b_cat = jnp.stack([jnp.concatenate([b_half(bbar_r, hf), b_half(bbar_i, hf)], axis=1)
                       for hf in range(2)]).astype(BF16)
    c_cat = jnp.stack([jnp.concatenate([c_half(c_re, hf), -c_half(c_im, hf)], axis=0)
                       for hf in range(2)]).astype(BF16)
    return a_bc, b_cat, c_cat


def _merge_kernel(x_ref, o_ref, y_ref, gates_ref, wab_ref, wglu_ref, bglu_ref, wout_ref,
                  gffn_ref, wr2_ref, br_ref,
                  x1_ref, xn_ref, idx_ref, gate_ref, rank_ref, cnt_ref, carry_ref):
    @pl.when(pl.program_id(0) == 0)
    def _():
        carry_ref[...] = jnp.zeros_like(carry_ref)

    subs = [pl.ds(s * TOK_TILE, TOK_TILE) for s in range(MERGE_SUBTILES)]
    nt = (((1,), (1,)), ((), ()))

    attn_branch = [jnp.dot(o_ref[s, :].astype(BF16), wab_ref[...], preferred_element_type=F32)
                   for s in subs]
    zg = [jnp.dot(jax.nn.gelu(y_ref[s, :]).astype(BF16), wglu_ref[...],
                  preferred_element_type=F32) + bglu_ref[...] for s in subs]
    mixed = []
    for i, s in enumerate(subs):
        ssm_branch = zg[i][:, :D_MODEL] * jax.nn.sigmoid(zg[i][:, D_MODEL:])
        mixed.append((jax.nn.sigmoid(gates_ref[s, :D_MODEL]) * attn_branch[i]
                      + jax.nn.sigmoid(gates_ref[s, D_MODEL:]) * ssm_branch).astype(BF16))
    x1 = [x_ref[s, :] + jnp.dot(mixed[i], wout_ref[...], preferred_element_type=F32)
          for i, s in enumerate(subs)]
    xh, xl = [], []
    for i, s in enumerate(subs):
        x1_ref[s, :] = x1[i]
        xn = _rms(x1[i], gffn_ref[...])
        for c in range(ROW_TILES):
            xn_ref[pl.ds(i * TOK_TILE * ROW_TILES + c, TOK_TILE, stride=ROW_TILES), :] = (
                xn[:, c * LANES:(c + 1) * LANES])
        xh.append(xn.astype(BF16))
        xl.append((xn - xh[i].astype(F32)).astype(BF16))
    logits = []
    for i in range(MERGE_SUBTILES):
        both = lax.dot_general(wr2_ref[...], xh[i], nt, preferred_element_type=F32)
        cross = lax.dot_general(wr2_ref[:N_EXPERTS, :], xl[i], nt, preferred_element_type=F32)
        logits.append((both[:N_EXPERTS] + both[N_EXPERTS:] + cross) + br_ref[...])

    e_iota = lax.broadcasted_iota(jnp.int32, (N_EXPERTS, TOK_TILE), 0).astype(F32)
    k_iota = lax.broadcasted_iota(jnp.int32, (SUBLANES, TOK_TILE), 0)
    rr = lax.broadcasted_iota(jnp.int32, (TOK_TILE, TOK_TILE), 0)
    cc = lax.broadcasted_iota(jnp.int32, (TOK_TILE, TOK_TILE), 1)
    earlier = jnp.where(rr < cc, 1.0, 0.0).astype(BF16)

    for i, s in enumerate(subs):
        work = logits[i]
        hits, vals = [], []
        sel = jnp.zeros((N_EXPERTS, TOK_TILE), F32)
        for _ in range(TOP_K):
            m = jnp.max(work, axis=0, keepdims=True)
            pick = jnp.min(jnp.where(work == m, e_iota, float(N_EXPERTS)), axis=0, keepdims=True)
            hit = e_iota == pick
            work = jnp.where(hit, -jnp.inf, work)
            sel = sel + jnp.where(hit, 1.0, 0.0)
            hits.append((hit, pick))
            vals.append(m)
        before = jnp.dot(sel.astype(BF16), earlier, preferred_element_type=F32) + carry_ref[...]
        exps = [jnp.exp(v - vals[0]) for v in vals]
        denom = exps[0] + exps[1] + exps[2] + exps[3]
        idx = jnp.zeros((SUBLANES, TOK_TILE), jnp.int32)
        gate = jnp.zeros((SUBLANES, TOK_TILE), F32)
        rank = jnp.zeros((SUBLANES, TOK_TILE), jnp.int32)
        for k in range(TOP_K):
            hit, pick = hits[k]
            rk = jnp.sum(jnp.where(hit, before, 0.0), axis=0, keepdims=True)
            idx = jnp.where(k_iota == k, pick.astype(jnp.int32), idx)
            gate = jnp.where(k_iota == k, exps[k] / denom, gate)
            rank = jnp.where(k_iota == k, rk.astype(jnp.int32), rank)
        idx_ref[:, s] = idx
        gate_ref[:, s] = gate
        rank_ref[:, s] = rank
        carry_ref[...] += jnp.sum(sel, axis=1, keepdims=True)
    cnt_ref[...] = jnp.broadcast_to(carry_ref[...], cnt_ref.shape)


def _merge(x2d, o2d, y_tm2, gates, wab, wglu, bglu, wout, gffn, wr2_t, br_col, s_len):
    n = x2d.shape[0]
    tile = MERGE_SUBTILES * TOK_TILE
    tiles_per_seq = s_len // tile
    full = lambda shape: pl.BlockSpec(shape, lambda i: (0,) * len(shape))
    row = lambda w: pl.BlockSpec((tile, w), lambda i: (i, 0))
    col = pl.BlockSpec((SUBLANES, tile), lambda i: (0, i))
    return pl.pallas_call(
        _merge_kernel,
        grid=(n // tile,),
        in_specs=[
            row(D_MODEL), row(SB_WIDTH),
            pl.BlockSpec((tile, SSM_WIDTH),
                         lambda i: (i % tiles_per_seq, i // tiles_per_seq)),
            row(2 * D_MODEL),
            full((SB_WIDTH, D_MODEL)), full((SSM_WIDTH, 2 * D_MODEL)), full((1, 2 * D_MODEL)),
            full((D_MODEL, D_MODEL)), full((1, D_MODEL)),
            full((2 * N_EXPERTS, D_MODEL)), full((N_EXPERTS, 1)),
        ],
        out_specs=[row(D_MODEL), pl.BlockSpec((tile * ROW_TILES, LANES), lambda i: (i, 0)),
                   col, col, col, full((N_EXPERTS, LANES))],
        out_shape=[
            jax.ShapeDtypeStruct((n, D_MODEL), F32),
            jax.ShapeDtypeStruct((n * ROW_TILES, LANES), F32),
            jax.ShapeDtypeStruct((SUBLANES, n), jnp.int32),
            jax.ShapeDtypeStruct((SUBLANES, n), F32),
            jax.ShapeDtypeStruct((SUBLANES, n), jnp.int32),
            jax.ShapeDtypeStruct((N_EXPERTS, LANES), F32),
        ],
        scratch_shapes=[pltpu.VMEM((N_EXPERTS, 1), F32)],
        compiler_params=_cparams(1),
        name="merge",
    )(x2d, o2d, y_tm2, gates, wab, wglu, bglu, wout, gffn, wr2_t, br_col)


def _lane_tile(ref2d, c, n_rows):
    return ref2d.at[pl.ds(c, n_rows, stride=ROW_TILES), :]


def _load_row_tiled(ref2d, n_rows):
    return jnp.concatenate([_lane_tile(ref2d, c, n_rows)[...] for c in range(ROW_TILES)], axis=1)


def _dispatch_kernel(slot_ref, xn_ref, xs_ref, sem):
    def issue(g, _):
        for k in range(TOP_K):
            for j in range(SUBLANES):
                t = g * SUBLANES + j
                pltpu.make_async_copy(xn_ref.at[t], xs_ref.at[slot_ref[k * TOK_TILE + t]],
                                      sem).start(priority=j % 2)
        return 0

    lax.fori_loop(0, TILE_GROUPS, issue, 0)
    for _ in range(TOP_K):
        pltpu.make_async_copy(xn_ref, xs_ref.at[pl.ds(0, TOK_TILE)], sem).wait()


def _dispatch(slots_flat, xn3):
    n = xn3.shape[0]
    return pl.pallas_call(
        _dispatch_kernel,
        grid=(n // TOK_TILE,),
        in_specs=[
            pl.BlockSpec((TOP_K * TOK_TILE,), lambda i: (i,), memory_space=pltpu.SMEM),
            pl.BlockSpec((TOK_TILE, ROW_TILES, LANES), lambda i: (i, 0, 0)),
        ],
        out_specs=pl.BlockSpec(memory_space=pl.ANY),
        out_shape=jax.ShapeDtypeStruct((n * TOP_K, ROW_TILES, LANES), F32),
        scratch_shapes=[pltpu.SemaphoreType.DMA],
        compiler_params=_cparams(1),
        name="dispatch",
    )(slots_flat, xn3)


FLAG_VALID, FLAG_FIRST_VISIT, FLAG_NEW_EXPERT = 1, 2, 4


def _experts_kernel(blk_ref, exp_ref, nexp_ref, flag_ref, start_ref,
                    xs_ref, wup_hbm_ref, bup_ref, wdn_hbm_ref, bdn_ref, ys_ref,
                    wup_bf_ref, wdn_bf_ref, wup_f32_ref, wdn_f32_ref, keep_ref, wsem):
    i = pl.program_id(0)
    flags = flag_ref[i]

    def weight_copies(e):
        return (pltpu.make_async_copy(wup_hbm_ref.at[e], wup_f32_ref, wsem.at[0]),
                pltpu.make_async_copy(wdn_hbm_ref.at[e], wdn_f32_ref, wsem.at[1]))

    @pl.when(i == 0)
    def _():
        for cp in weight_copies(exp_ref[0]):
            cp.start(priority=1)

    @pl.when((flags & FLAG_NEW_EXPERT) != 0)
    def _():
        for cp in weight_copies(exp_ref[i]):
            cp.wait()
        wup_bf_ref[...] = wup_f32_ref[...].astype(BF16)
        wdn_bf_ref[...] = wdn_f32_ref[...].astype(BF16)
        nxt = nexp_ref[i]

        @pl.when(nxt >= 0)
        def _():
            for cp in weight_copies(nxt):
                cp.start(priority=1)

    revisit = ((flags & FLAG_VALID) != 0) & ((flags & FLAG_FIRST_VISIT) == 0)

    @pl.when(revisit)
    def _():
        keep_ref[...] = _load_row_tiled(ys_ref, ROW_BLK)

    @pl.when((flags & FLAG_VALID) != 0)
    def _():
        x = _load_row_tiled(xs_ref, ROW_BLK).astype(BF16)
        acts = []
        for c in range(D_FF // FF_CHUNK):
            gcols = pl.ds(c * FF_CHUNK, FF_CHUNK)
            ucols = pl.ds(D_FF + c * FF_CHUNK, FF_CHUNK)
            g = jnp.dot(x, wup_bf_ref[:, gcols], preferred_element_type=F32) + bup_ref[0, :, gcols]
            up = jnp.dot(x, wup_bf_ref[:, ucols], preferred_element_type=F32) + bup_ref[0, :, ucols]
            g = jnp.minimum(g, SWIGLU_LIMIT)
            up = jnp.clip(up, -SWIGLU_LIMIT, SWIGLU_LIMIT)
            acts.append(((up + 1.0) * g * jax.nn.sigmoid(SWIGLU_ALPHA * g)).astype(BF16))
        y = jnp.dot(jnp.concatenate(acts, axis=1), wdn_bf_ref[...],
                    preferred_element_type=F32) + bdn_ref[0]
        for c in range(ROW_TILES):
            _lane_tile(ys_ref, c, ROW_BLK)[...] = y[:, c * LANES:(c + 1) * LANES]

    @pl.when(revisit)
    def _():
        row = blk_ref[i] * ROW_BLK + lax.broadcasted_iota(jnp.int32, (ROW_BLK, 1), 0)
        earlier = jnp.broadcast_to(row < start_ref[exp_ref[i]], (ROW_BLK, LANES))
        for c in range(ROW_TILES):
            tile = _lane_tile(ys_ref, c, ROW_BLK)
            tile[...] = jnp.where(earlier, keep_ref[:, c * LANES:(c + 1) * LANES], tile[...])


def _experts(item_blk, item_exp, item_nexp, item_flag, starts, xs2d, w_up, b_up3, w_down, b_down3):
    n_items = item_blk.shape[0]
    rows_blk = pl.BlockSpec((ROW_BLK * ROW_TILES, LANES), lambda i, b, e, x, f, s: (b[i], 0))
    grid_spec = pltpu.PrefetchScalarGridSpec(
        num_scalar_prefetch=5,
        grid=(n_items,),
        in_specs=[
            rows_blk,
            pl.BlockSpec(memory_space=pl.ANY),
            pl.BlockSpec((1, 1, 2 * D_FF), lambda i, b, e, x, f, s: (e[i], 0, 0)),
            pl.BlockSpec(memory_space=pl.ANY),
            pl.BlockSpec((1, 1, D_MODEL), lambda i, b, e, x, f, s: (e[i], 0, 0)),
        ],
        out_specs=rows_blk,
        scratch_shapes=[
            pltpu.VMEM((D_MODEL, 2 * D_FF), BF16),
            pltpu.VMEM((D_FF, D_MODEL), BF16),
            pltpu.VMEM((D_MODEL, 2 * D_FF), F32),
            pltpu.VMEM((D_FF, D_MODEL), F32),
            pltpu.VMEM((ROW_BLK, D_MODEL), F32),
            pltpu.SemaphoreType.DMA((2,)),
        ],
    )
    return pl.pallas_call(
        _experts_kernel,
        grid_spec=grid_spec,
        out_shape=jax.ShapeDtypeStruct(xs2d.shape, F32),
        compiler_params=_cparams(1),
        name="experts",
    )(item_blk, item_exp, item_nexp, item_flag, starts, xs2d, w_up, b_up3, w_down, b_down3)


def _build_items(counts, n_rows):
    ends = jnp.cumsum(counts)
    starts = ends - counts
    n_blocks = n_rows // ROW_BLK
    n_items = n_blocks + N_ITEMS_EXTRA
    lo = jnp.arange(n_blocks, dtype=jnp.int32)[:, None] * ROW_BLK
    overlap = jnp.minimum(ends[None, :], lo + ROW_BLK) - jnp.maximum(starts[None, :], lo)
    live = (overlap > 0).reshape(-1)
    n_live = jnp.sum(live.astype(jnp.int32))
    (pos,) = jnp.nonzero(live, size=n_items, fill_value=0)
    pos = pos.astype(jnp.int32)
    k = jnp.arange(n_items, dtype=jnp.int32)
    valid = k < n_live
    pos = jnp.where(valid, pos, pos[jnp.maximum(n_live - 1, 0)])
    blk = pos // N_EXPERTS
    exp = pos % N_EXPERTS
    prev_blk = jnp.concatenate([jnp.full((1,), -1, jnp.int32), blk[:-1]])
    prev_exp = jnp.concatenate([jnp.full((1,), -1, jnp.int32), exp[:-1]])
    new_exp = valid & (exp != prev_exp)
    flags = (jnp.where(valid, FLAG_VALID, 0)
             | jnp.where(valid & (blk != prev_blk), FLAG_FIRST_VISIT, 0)
             | jnp.where(new_exp, FLAG_NEW_EXPERT, 0)).astype(jnp.int32)
    first_at = jnp.where(new_exp, k, n_items)
    next_first = jnp.flip(lax.cummin(jnp.flip(first_at)))
    next_first = jnp.concatenate([next_first[1:], jnp.full((1,), n_items, jnp.int32)])
    nexp = jnp.where(next_first < n_items, exp[jnp.minimum(next_first, n_items - 1)], -1)
    starts33 = jnp.concatenate([starts, ends[-1:]]).astype(jnp.int32)
    return blk, exp, nexp.astype(jnp.int32), flags, starts33


def _combine_kernel(slot_cur_ref, slot_next_ref, ys3_ref, ys2d_ref, x1_ref, gate_ref, p_ref,
                    gpg_ref, wpg_ref, wpp_ref, gpp_ref, out_ref, rows_ref, sem):
    i = pl.program_id(0)
    last = pl.num_programs(0) - 1
    cur = i % 2
    nxt = 1 - cur

    def start(slot_ref, buf):
        for k in range(TOP_K):
            for t in range(TOK_TILE):
                pltpu.make_async_copy(ys3_ref.at[slot_ref[k * TOK_TILE + t]],
                                      rows_ref.at[buf, k, pl.ds(t * ROW_TILES, ROW_TILES)],
                                      sem.at[buf]).start(priority=t % 2)

    def wait(buf):
        for k in range(TOP_K):
            pltpu.make_async_copy(ys2d_ref.at[pl.ds(0, TOK_TILE * ROW_TILES)],
                                  rows_ref.at[buf, k], sem.at[buf]).wait()

    @pl.when(i == 0)
    def _():
        start(slot_cur_ref, 0)

    wait(cur)
    gate = jnp.concatenate([gate_ref[...], jnp.zeros((LANES - SUBLANES, TOK_TILE), F32)],
                           axis=0).T
    moe = []
    for c in range(ROW_TILES):
        acc = gate[:, 0:1] * _lane_tile(rows_ref.at[cur, 0], c, TOK_TILE)[...]
        for k in range(1, TOP_K):
            acc = acc + gate[:, k:k + 1] * _lane_tile(rows_ref.at[cur, k], c, TOK_TILE)[...]
        moe.append(acc)
    x2 = x1_ref[...] + jnp.concatenate(moe, axis=1)
    start(slot_next_ref, nxt)
    ple = _rms(jnp.dot(p_ref[...].astype(BF16), wpp_ref[...], preferred_element_type=F32),
               gpp_ref[...])
    pg = jax.nn.sigmoid(jnp.dot(_rms(x2, gpg_ref[...]).astype(BF16), wpg_ref[...],
                                preferred_element_type=F32))
    out_ref[...] = x2 + pg * ple

    @pl.when(i == last)
    def _():
        wait(nxt)


def _combine(slots_flat, ys2d, x1, gate, p2d, gpg, wpg, wpp, gpp):
    n = x1.shape[0]
    n_tiles = n // TOK_TILE
    full = lambda shape: pl.BlockSpec(shape, lambda i: (0,) * len(shape))
    row = lambda w: pl.BlockSpec((TOK_TILE, w), lambda i: (i, 0))
    return pl.pallas_call(
        _combine_kernel,
        grid=(n_tiles,),
        in_specs=[
            pl.BlockSpec((TOP_K * TOK_TILE,), lambda i: (i,), memory_space=pltpu.SMEM),
            pl.BlockSpec((TOP_K * TOK_TILE,), lambda i: (jnp.minimum(i + 1, n_tiles - 1),),
                         memory_space=pltpu.SMEM),
            pl.BlockSpec(memory_space=pl.ANY), pl.BlockSpec(memory_space=pl.ANY),
            row(D_MODEL), pl.BlockSpec((SUBLANES, TOK_TILE), lambda i: (0, i)), row(PLE_DIM),
            full((1, D_MODEL)), full((D_MODEL, D_MODEL)), full((PLE_DIM, D_MODEL)),
            full((1, D_MODEL)),
        ],
        out_specs=row(D_MODEL),
        out_shape=jax.ShapeDtypeStruct((n, D_MODEL), F32),
        scratch_shapes=[pltpu.VMEM((2, TOP_K, TOK_TILE * ROW_TILES, LANES), F32),
                        pltpu.SemaphoreType.DMA((2,))],
        compiler_params=_cparams(1),
        name="combine",
    )(slots_flat, slots_flat, ys2d.reshape(-1, ROW_TILES, LANES), ys2d, x1, gate, p2d,
      gpg, wpg, wpp, gpp)


def kernel(x, p, g_mix, w_in, g_q, g_k, w_attn_branch, a_re, a_im, log_dt, b_re, b_im, c_re, c_im, d_skip, w_glu, b_glu, w_out, g_ffn, w_router, b_router, w_up, b_up, w_down, b_down, g_ple_gate, w_ple_gate, w_ple_proj, g_ple_post):
    bsz, s_len, d = x.shape
    depth = w_in.shape[0]
    n = bsz * s_len
    assert d == D_MODEL and s_len % (MERGE_SUBTILES * TOK_TILE) == 0 and s_len % SCAN_CHUNK == 0
    assert bsz == SUBLANES, "the S5 scan keeps the batch on the sublane axis"

    x2d = x.reshape(n, d)
    for i in range(depth):
        qs, kn, vb, u_tm, gates = _inproj(x2d, g_mix[i][None], w_in[i].astype(BF16),
                                          jnp.tile(g_q[i], 2)[None], jnp.tile(g_k[i], 2)[None],
                                          bsz, s_len)
        o = _attn(qs, kn.reshape(bsz, s_len, SB_WIDTH), vb.reshape(bsz, s_len, SB_WIDTH))
        a_bc, b_cat, c_cat = _s5_params(a_re[i], a_im[i], log_dt[i], b_re[i], b_im[i],
                                        c_re[i], c_im[i], bsz)
        y_tm = _s5(u_tm.reshape(s_len, bsz, SSM_WIDTH), a_bc, b_cat, c_cat, d_skip[i][None])

        wr_t = w_router[i].T
        wr_hi = wr_t.astype(BF16)
        wr_lo = (wr_t - wr_hi.astype(F32)).astype(BF16)
        x1, xn, idx, gate, rank, counts = _merge(
            x2d, o.reshape(n, SB_WIDTH), y_tm.reshape(s_len, bsz * SSM_WIDTH), gates,
            w_attn_branch[i].astype(BF16), w_glu[i].astype(BF16), b_glu[i][None],
            w_out[i].astype(BF16), g_ffn[i][None], jnp.concatenate([wr_hi, wr_lo], axis=0),
            b_router[i][:, None], s_len)

        counts_i = counts[:, 0].astype(jnp.int32)
        blk, exp, nexp, flags, starts33 = _build_items(counts_i, n * TOP_K)
        hot = idx[:TOP_K, :, None] == jnp.arange(N_EXPERTS, dtype=jnp.int32)
        slots = jnp.sum(jnp.where(hot, starts33[:N_EXPERTS], 0), axis=-1) + rank[:TOP_K]
        slots_flat = slots.reshape(TOP_K, n // TOK_TILE, TOK_TILE).transpose(1, 0, 2).reshape(-1)

        xs3 = _dispatch(slots_flat, xn.reshape(n, ROW_TILES, LANES))
        ys2d = _experts(blk, exp, nexp, flags, starts33, xs3.reshape(n * TOP_K * ROW_TILES, LANES),
                        w_up[i], b_up[i][:, None, :], w_down[i], b_down[i][:, None, :])
        x2d = _combine(slots_flat, ys2d, x1, gate, p[i].reshape(n, PLE_DIM),
                       g_ple_gate[i][None], w_ple_gate[i].astype(BF16),
                       w_ple_proj[i].astype(BF16), g_ple_post[i][None])
    return x2d.reshape(bsz, s_len, d)
```

```python
import jax
import jax.numpy as jnp
from jax import lax
from jax.experimental import pallas as pl
from jax.experimental.pallas import tpu as pltpu

F32 = jnp.float32
BF16 = jnp.bfloat16

D_MODEL = 1024
SB_HEADS = 8
SB_HEAD_DIM = 64
SB_WIDTH = SB_HEADS * SB_HEAD_DIM
SSM_GROUP = 16
SSM_WIDTH = 512
SSM_GROUPS = SSM_WIDTH // SSM_GROUP
SSM_STATE = 64
PLE_DIM = 256
N_EXPERTS = 32
TOP_K = 4
D_FF = D_MODEL
SWIGLU_LIMIT = 7.0
SWIGLU_ALPHA = 1.702
EPS = 1e-6

LANES = 128
SUBLANES = 8
VMEM_LIMIT = 56 * 1024 * 1024

TOK_TILE = 256
IN_TILE = 512
MOE_TILE = 512
ROW_TILES = D_MODEL // LANES
assert ROW_TILES == SUBLANES
MERGE_SUBTILES = 2
ATT_BLK = 128
DEAD_LOG_WEIGHT = -104.0
HEAD_PAIRS = SB_WIDTH // LANES
SCAN_CHUNK = 64
SCAN_LANES = 512
HALF_U = SSM_WIDTH // 2
HALF_STATE = SSM_GROUPS // 2 * SSM_STATE
ROW_BLK = 256
FF_CHUNK = 256
N_ITEMS_EXTRA = N_EXPERTS - 1


def _cparams(n_axes):
    return pltpu.CompilerParams(
        dimension_semantics=("arbitrary",) * n_axes,
        vmem_limit_bytes=VMEM_LIMIT)


def _rms(x, g):
    ms = jnp.mean(x * x, axis=-1, keepdims=True)
    return x * lax.rsqrt(ms + EPS) * g


def _inproj_kernel(x_ref, g_ref, w_ref, gq_ref, gk_ref, qs_ref, k_ref, v_ref, u_ref, gates_ref):
    h = _rms(x_ref[...], g_ref[...]).astype(BF16)
    n_qkv = 3 * SB_WIDTH
    n_u = n_qkv + SSM_WIDTH
    qkv = jnp.dot(h, w_ref[:, :n_qkv], preferred_element_type=F32)
    u_ref[...] = jnp.dot(h, w_ref[:, n_qkv:n_u], preferred_element_type=F32)
    gates_ref[...] = jnp.dot(h, w_ref[:, n_u:], preferred_element_type=F32)

    head0 = lax.broadcasted_iota(jnp.int32, (1, LANES), 1) < SB_HEAD_DIM

    def head_rms(t, g):
        sq = t * t
        s0 = jnp.sum(jnp.where(head0, sq, 0.0), axis=-1, keepdims=True)
        s1 = jnp.sum(jnp.where(head0, 0.0, sq), axis=-1, keepdims=True)
        ms = jnp.where(head0, s0, s1) * (1.0 / SB_HEAD_DIM)
        return t * lax.rsqrt(ms + EPS) * g

    scale = SB_HEAD_DIM ** -0.5
    for p in range(HEAD_PAIRS):
        cols = slice(p * LANES, (p + 1) * LANES)
        qn = head_rms(qkv[:, cols], gq_ref[...]) * scale
        q0 = jnp.where(head0, qn, 0.0).astype(BF16)
        q1 = jnp.where(head0, 0.0, qn).astype(BF16)
        for t in range(IN_TILE // ATT_BLK):
            rows = slice(t * ATT_BLK, (t + 1) * ATT_BLK)
            qs_ref[0, p, t, :ATT_BLK, :] = q0[rows]
            qs_ref[0, p, t, ATT_BLK:, :] = q1[rows]
        kcols = slice(SB_WIDTH + p * LANES, SB_WIDTH + (p + 1) * LANES)
        k_ref[:, cols] = head_rms(qkv[:, kcols], gk_ref[...]).astype(BF16)
    v_ref[...] = qkv[:, 2 * SB_WIDTH:].astype(BF16)


def _inproj(x2d, g_mix, w_in_bf, gq2, gk2, bsz, s_len):
    n = x2d.shape[0]
    tiles_per_seq = s_len // IN_TILE
    blks_per_tile = IN_TILE // ATT_BLK
    in_cols = w_in_bf.shape[1]
    return pl.pallas_call(
        _inproj_kernel,
        grid=(n // IN_TILE,),
        in_specs=[
            pl.BlockSpec((IN_TILE, D_MODEL), lambda i: (i, 0)),
            pl.BlockSpec((1, D_MODEL), lambda i: (0, 0)),
            pl.BlockSpec((D_MODEL, in_cols), lambda i: (0, 0)),
            pl.BlockSpec((1, LANES), lambda i: (0, 0)),
            pl.BlockSpec((1, LANES), lambda i: (0, 0)),
        ],
        out_specs=[
            pl.BlockSpec((1, HEAD_PAIRS, blks_per_tile, 2 * ATT_BLK, LANES),
                         lambda i: (i // tiles_per_seq, 0, i % tiles_per_seq, 0, 0)),
            pl.BlockSpec((IN_TILE, SB_WIDTH), lambda i: (i, 0)),
            pl.BlockSpec((IN_TILE, SB_WIDTH), lambda i: (i, 0)),
            pl.BlockSpec((IN_TILE, SSM_WIDTH),
                         lambda i: (i % tiles_per_seq, i // tiles_per_seq)),
            pl.BlockSpec((IN_TILE, 2 * D_MODEL), lambda i: (i, 0)),
        ],
        out_shape=[
            jax.ShapeDtypeStruct((bsz, HEAD_PAIRS, s_len // ATT_BLK, 2 * ATT_BLK, LANES), BF16),
            jax.ShapeDtypeStruct((n, SB_WIDTH), BF16),
            jax.ShapeDtypeStruct((n, SB_WIDTH), BF16),
            jax.ShapeDtypeStruct((s_len, bsz * SSM_WIDTH), F32),
            jax.ShapeDtypeStruct((n, 2 * D_MODEL), F32),
        ],
        compiler_params=_cparams(1),
        name="inproj",
    )(x2d, g_mix, w_in_bf, gq2, gk2)


def _attn_kernel(qs_ref, ks_ref, vs_ref, o_ref, w2_ref, c_ref, acc_ref):
    s_len = ks_ref.shape[1]
    n_blk = s_len // ATT_BLK
    n_pairs = HEAD_PAIRS
    head0 = lax.broadcasted_iota(jnp.int32, (1, LANES), 1) < SB_HEAD_DIM

    r = lax.broadcasted_iota(jnp.int32, (2 * ATT_BLK, 2 * ATT_BLK), 0)
    c = lax.broadcasted_iota(jnp.int32, (2 * ATT_BLK, 2 * ATT_BLK), 1)
    r = jnp.where(r >= ATT_BLK, r - ATT_BLK, r)
    w2_ref[...] = jnp.where((c >= ATT_BLK) | (r > c), 1.0, 0.0).astype(BF16)

    ti = lax.broadcasted_iota(jnp.int32, (2 * ATT_BLK, ATT_BLK), 0)
    si = lax.broadcasted_iota(jnp.int32, (2 * ATT_BLK, ATT_BLK), 1)
    causal = si < jnp.where(ti >= ATT_BLK, ti - ATT_BLK, ti)

    def rows_of(blk):
        if isinstance(blk, int):
            return pl.ds(blk * ATT_BLK, ATT_BLK)
        return pl.ds(pl.multiple_of(blk * ATT_BLK, ATT_BLK), ATT_BLK)

    def tiles(sweeps, diag):
        chains = [(w, qi, rows_of(kv), p) for (w, qi, kv) in sweeps for p in range(n_pairs)]
        z = [lax.dot_general(qs_ref[0, p, qi], ks_ref[0, kv_rows, p * LANES:(p + 1) * LANES],
                             (((1,), (1,)), ((), ())),
                             preferred_element_type=F32)
             for (w, qi, kv_rows, p) in chains]
        log_beta, stacked = [], []
        for zi in z:
            sp = jnp.maximum(zi, 0.0) + jnp.log(1.0 + jnp.exp(-jnp.abs(zi)))
            log_keep = -sp
            log_beta.append(zi - sp)
            if diag:
                log_keep = jnp.where(causal, log_keep, 0.0)
            hi = log_keep.astype(BF16)
            lo = (log_keep - hi.astype(F32)).astype(BF16)
            stacked.append(jnp.concatenate([hi, lo], axis=1))
        sums = [jnp.dot(st, w2_ref[...], preferred_element_type=F32) for st in stacked]
        weights = []
        for i, (w, qi, kv_rows, p) in enumerate(chains):
            wp = jnp.exp(log_beta[i] + sums[i][:, :ATT_BLK] + c_ref[w, p])
            if diag:
                wp = jnp.where(causal, wp, 0.0)
            weights.append(wp.astype(BF16))
            c_ref[w, p] += sums[i][:, ATT_BLK:]
        pv = [jnp.dot(weights[i], vs_ref[0, kv_rows, p * LANES:(p + 1) * LANES],
                      preferred_element_type=F32)
              for i, (w, qi, kv_rows, p) in enumerate(chains)]
        for i, (w, qi, kv_rows, p) in enumerate(chains):
            acc_ref[w, p] += pv[i]

    def qblocks(qb, _):
        qa = 2 * qb
        qc = qa + 1
        c_ref[...] = jnp.zeros_like(c_ref)
        acc_ref[...] = jnp.zeros_like(acc_ref)
        tiles([(0, qa, qa), (1, qc, qc)], True)

        def live(carry):
            jj, c_max = carry
            return (jj < qa) & (c_max > DEAD_LOG_WEIGHT)

        def kvblock(carry):
            jj, _ = carry
            tiles([(0, qa, qa - 1 - jj), (1, qc, qa - jj)], False)
            return jj + 1, jnp.max(c_ref[...])

        jj, _ = lax.while_loop(live, kvblock, (jnp.int32(0), jnp.max(c_ref[...])))

        @pl.when((jj == qa) & (jnp.max(c_ref[1]) > DEAD_LOG_WEIGHT))
        def _():
            tiles([(1, qc, 0)], False)

        for w, qi in ((0, qa), (1, qc)):
            for p in range(n_pairs):
                o_ref[0, rows_of(qi), p * LANES:(p + 1) * LANES] = jnp.where(
                    head0, acc_ref[w, p, :ATT_BLK, :], acc_ref[w, p, ATT_BLK:, :])
        return 0

    lax.fori_loop(0, n_blk // 2, qblocks, 0)


def _attn(qs, k3, v3):
    bsz, s_len, _ = k3.shape
    kv_blk = pl.BlockSpec((1, s_len, SB_WIDTH), lambda b: (b, 0, 0))
    return pl.pallas_call(
        _attn_kernel,
        grid=(bsz,),
        in_specs=[
            pl.BlockSpec((1,) + qs.shape[1:], lambda b: (b, 0, 0, 0, 0)),
            kv_blk, kv_blk,
        ],
        out_specs=pl.BlockSpec((1, s_len, SB_WIDTH), lambda b: (b, 0, 0)),
        out_shape=jax.ShapeDtypeStruct((bsz, s_len, SB_WIDTH), F32),
        scratch_shapes=[
            pltpu.VMEM((2 * ATT_BLK, 2 * ATT_BLK), BF16),
            pltpu.VMEM((2, HEAD_PAIRS, 2 * ATT_BLK, ATT_BLK), F32),
            pltpu.VMEM((2, HEAD_PAIRS, 2 * ATT_BLK, LANES), F32),
        ],
        compiler_params=_cparams(1),
        name="attn",
    )(qs, k3, v3)


def _s5_kernel(u_ref, a_ref, b_ref, c_ref, d_ref, y_ref, hbuf_ref, state_ref):
    tc, bsz, _ = u_ref.shape
    rows = tc * bsz

    @pl.when(pl.program_id(0) == 0)
    def _():
        state_ref[...] = jnp.zeros_like(state_ref)

    u2 = u_ref[...].reshape(rows, SSM_WIDTH)
    ub = u2.astype(BF16)
    halves = range(2)
    for hf in halves:
        uh = ub[:, hf * HALF_U:(hf + 1) * HALF_U]
        xh = jnp.dot(uh, b_ref[hf], preferred_element_type=F32)
        hbuf_ref[hf] = xh.reshape(tc, bsz, 2 * HALF_STATE)
    for hf in halves:
        for lc in range(HALF_STATE // SCAN_LANES):
            re = pl.ds(lc * SCAN_LANES, SCAN_LANES)
            im = pl.ds(HALF_STATE + lc * SCAN_LANES, SCAN_LANES)
            ar = a_ref[hf, :, re]
            ai = a_ref[hf, :, im]
            hr = state_ref[hf, :, re]
            hi = state_ref[hf, :, im]
            for t in range(tc):
                hr, hi = (ar * hr - ai * hi + hbuf_ref[hf, t, :, re],
                          ar * hi + ai * hr + hbuf_ref[hf, t, :, im])
                hbuf_ref[hf, t, :, re] = hr
                hbuf_ref[hf, t, :, im] = hi
            state_ref[hf, :, re] = hr
            state_ref[hf, :, im] = hi
        hb = hbuf_ref[hf].reshape(rows, 2 * HALF_STATE).astype(BF16)
        yh = jnp.dot(hb, c_ref[hf], preferred_element_type=F32)
        cols = slice(hf * HALF_U, (hf + 1) * HALF_U)
        yh = yh + d_ref[:, cols] * u2[:, cols]
        y_ref[:, :, cols] = yh.reshape(tc, bsz, HALF_U)


def _s5(u_tm3, a_bc, b_cat, c_cat, d_skip):
    s_len, bsz, _ = u_tm3.shape
    return pl.pallas_call(
        _s5_kernel,
        grid=(s_len // SCAN_CHUNK,),
        in_specs=[
            pl.BlockSpec((SCAN_CHUNK, bsz, SSM_WIDTH), lambda c: (c, 0, 0)),
            pl.BlockSpec((2, bsz, 2 * HALF_STATE), lambda c: (0, 0, 0)),
            pl.BlockSpec((2, HALF_U, 2 * HALF_STATE), lambda c: (0, 0, 0)),
            pl.BlockSpec((2, 2 * HALF_STATE, HALF_U), lambda c: (0, 0, 0)),
            pl.BlockSpec((1, SSM_WIDTH), lambda c: (0, 0)),
        ],
        out_specs=pl.BlockSpec((SCAN_CHUNK, bsz, SSM_WIDTH), lambda c: (c, 0, 0)),
        out_shape=jax.ShapeDtypeStruct((s_len, bsz, SSM_WIDTH), F32),
        scratch_shapes=[
            pltpu.VMEM((2, SCAN_CHUNK, bsz, 2 * HALF_STATE), F32),
            pltpu.VMEM((2, bsz, 2 * HALF_STATE), F32),
        ],
        compiler_params=_cparams(1),
        name="s5",
    )(u_tm3, a_bc, b_cat, c_cat, d_skip)


def _s5_params(a_re, a_im, log_dt, b_re, b_im, c_re, c_im, bsz):
    dt = jnp.exp(log_dt)[:, None]
    mag = jnp.exp(a_re * dt)
    abar_r = mag * jnp.cos(a_im * dt)
    abar_i = mag * jnp.sin(a_im * dt)
    den = a_re * a_re + a_im * a_im
    nr = abar_r - 1.0
    ni = abar_i
    fr = (nr * a_re + ni * a_im) / den
    fi = (ni * a_re - nr * a_im) / den
    bbar_r = fr[..., None] * b_re - fi[..., None] * b_im
    bbar_i = fr[..., None] * b_im + fi[..., None] * b_re
    gh = SSM_GROUPS // 2
    eye = jnp.eye(gh, dtype=F32)

    def a_half(hf):
        sl = slice(hf * gh, (hf + 1) * gh)
        row = jnp.concatenate([abar_r[sl].reshape(-1), abar_i[sl].reshape(-1)])
        return jnp.broadcast_to(row[None], (bsz, 2 * HALF_STATE))

    def b_half(bb, hf):
        blk = bb[hf * gh:(hf + 1) * gh]
        return jnp.einsum('gpc,gk->gckp', blk, eye).reshape(HALF_U, HALF_STATE)

    def c_half(cc, hf):
        blk = cc[hf * gh:(hf + 1) * gh]
        return jnp.einsum('gcp,gk->gpkc', blk, eye).reshape(HALF_STATE, HALF_U)

    a_bc = jnp.stack([a_half(0), a_half(1)])
    b_cat = jnp.stack([jnp.concatenate([b_half(bbar_r, hf), b_half(bbar_i, hf)], axis=1)
                       for hf in range(2)]).astype(BF16)
    c_cat = jnp.stack([jnp.concatenate([c_half(c_re, hf), -c_half(c_im, hf)], axis=0)
                       for hf in range(2)]).astype(BF16)
    return a_bc, b_cat, c_cat


def _merge_kernel(x_ref, o_ref, y_ref, gates_ref, wab_ref, wglu_ref, bglu_ref, wout_ref,
                  gffn_ref, wr2_ref, br_ref,
                  x1_ref, xn_ref, idx_ref, gate_ref, rank_ref, cnt_ref, carry_ref):
    @pl.when(pl.program_id(0) == 0)
    def _():
        carry_ref[...] = jnp.zeros_like(carry_ref)

    subs = [pl.ds(s * TOK_TILE, TOK_TILE) for s in range(MERGE_SUBTILES)]
    nt = (((1,), (1,)), ((), ()))

    attn_branch = [jnp.dot(o_ref[s, :].astype(BF16), wab_ref[...], preferred_element_type=F32)
                   for s in subs]
    zg = [jnp.dot(jax.nn.gelu(y_ref[s, :]).astype(BF16), wglu_ref[...],
                  preferred_element_type=F32) + bglu_ref[...] for s in subs]
    mixed = []
    for i, s in enumerate(subs):
        ssm_branch = zg[i][:, :D_MODEL] * jax.nn.sigmoid(zg[i][:, D_MODEL:])
        mixed.append((jax.nn.sigmoid(gates_ref[s, :D_MODEL]) * attn_branch[i]
                      + jax.nn.sigmoid(gates_ref[s, D_MODEL:]) * ssm_branch).astype(BF16))
    x1 = [x_ref[s, :] + jnp.dot(mixed[i], wout_ref[...], preferred_element_type=F32)
          for i, s in enumerate(subs)]
    xh, xl = [], []
    for i, s in enumerate(subs):
        x1_ref[s, :] = x1[i]
        xn = _rms(x1[i], gffn_ref[...])
        for c in range(ROW_TILES):
            xn_ref[pl.ds(i * TOK_TILE * ROW_TILES + c, TOK_TILE, stride=ROW_TILES), :] = (
                xn[:, c * LANES:(c + 1) * LANES])
        xh.append(xn.astype(BF16))
        xl.append((xn - xh[i].astype(F32)).astype(BF16))
    logits = []
    for i in range(MERGE_SUBTILES):
        both = lax.dot_general(wr2_ref[...], xh[i], nt, preferred_element_type=F32)
        cross = lax.dot_general(wr2_ref[:N_EXPERTS, :], xl[i], nt, preferred_element_type=F32)
        logits.append((both[:N_EXPERTS] + both[N_EXPERTS:] + cross) + br_ref[...])

    e_iota = lax.broadcasted_iota(jnp.int32, (N_EXPERTS, TOK_TILE), 0).astype(F32)
    k_iota = lax.broadcasted_iota(jnp.int32, (SUBLANES, TOK_TILE), 0)
    rr = lax.broadcasted_iota(jnp.int32, (TOK_TILE, TOK_TILE), 0)
    cc = lax.broadcasted_iota(jnp.int32, (TOK_TILE, TOK_TILE), 1)
    earlier = jnp.where(rr < cc, 1.0, 0.0).astype(BF16)

    for i, s in enumerate(subs):
        work = logits[i]
        hits, vals = [], []
        sel = jnp.zeros((N_EXPERTS, TOK_TILE), F32)
        for _ in range(TOP_K):
            m = jnp.max(work, axis=0, keepdims=True)
            pick = jnp.min(jnp.where(work == m, e_iota, float(N_EXPERTS)), axis=0, keepdims=True)
            hit = e_iota == pick
            work = jnp.where(hit, -jnp.inf, work)
            sel = sel + jnp.where(hit, 1.0, 0.0)
            hits.append((hit, pick))
            vals.append(m)
        before = jnp.dot(sel.astype(BF16), earlier, preferred_element_type=F32) + carry_ref[...]
        exps = [jnp.exp(v - vals[0]) for v in vals]
        denom = exps[0] + exps[1] + exps[2] + exps[3]
        idx = jnp.zeros((SUBLANES, TOK_TILE), jnp.int32)
        gate = jnp.zeros((SUBLANES, TOK_TILE), F32)
        rank = jnp.zeros((SUBLANES, TOK_TILE), jnp.int32)
        for k in range(TOP_K):
            hit, pick = hits[k]
            rk = jnp.sum(jnp.where(hit, before, 0.0), axis=0, keepdims=True)
            idx = jnp.where(k_iota == k, pick.astype(jnp.int32), idx)
            gate = jnp.where(k_iota == k, exps[k] / denom, gate)
            rank = jnp.where(k_iota == k, rk.astype(jnp.int32), rank)
        idx_ref[:, s] = idx
        gate_ref[:, s] = gate
        rank_ref[:, s] = rank
        carry_ref[...] += jnp.sum(sel, axis=1, keepdims=True)
    cnt_ref[...] = jnp.broadcast_to(carry_ref[...], cnt_ref.shape)


def _merge(x2d, o2d, y_tm2, gates, wab, wglu, bglu, wout, gffn, wr2_t, br_col, s_len):
    n = x2d.shape[0]
    tile = MERGE_SUBTILES * TOK_TILE
    tiles_per_seq = s_len // tile
    full = lambda shape: pl.BlockSpec(shape, lambda i: (0,) * len(shape))
    row = lambda w: pl.BlockSpec((tile, w), lambda i: (i, 0))
    col = pl.BlockSpec((SUBLANES, tile), lambda i: (0, i))
    return pl.pallas_call(
        _merge_kernel,
        grid=(n // tile,),
        in_specs=[
            row(D_MODEL), row(SB_WIDTH),
            pl.BlockSpec((tile, SSM_WIDTH),
                         lambda i: (i % tiles_per_seq, i // tiles_per_seq)),
            row(2 * D_MODEL),
            full((SB_WIDTH, D_MODEL)), full((SSM_WIDTH, 2 * D_MODEL)), full((1, 2 * D_MODEL)),
            full((D_MODEL, D_MODEL)), full((1, D_MODEL)),
            full((2 * N_EXPERTS, D_MODEL)), full((N_EXPERTS, 1)),
        ],
        out_specs=[row(D_MODEL), pl.BlockSpec((tile * ROW_TILES, LANES), lambda i: (i, 0)),
                   col, col, col, full((N_EXPERTS, LANES))],
        out_shape=[
            jax.ShapeDtypeStruct((n, D_MODEL), F32),
            jax.ShapeDtypeStruct((n * ROW_TILES, LANES), F32),
            jax.ShapeDtypeStruct((SUBLANES, n), jnp.int32),
            jax.ShapeDtypeStruct((SUBLANES, n), F32),
            jax.ShapeDtypeStruct((SUBLANES, n), jnp.int32),
            jax.ShapeDtypeStruct((N_EXPERTS, LANES), F32),
        ],
        scratch_shapes=[pltpu.VMEM((N_EXPERTS, 1), F32)],
        compiler_params=_cparams(1),
        name="merge",
    )(x2d, o2d, y_tm2, gates, wab, wglu, bglu, wout, gffn, wr2_t, br_col)


def _lane_tile(ref2d, c, n_rows):
    return ref2d.at[pl.ds(c, n_rows, stride=ROW_TILES), :]


def _load_row_tiled(ref2d, n_rows):
    return jnp.concatenate([_lane_tile(ref2d, c, n_rows)[...] for c in range(ROW_TILES)], axis=1)


def _dispatch_kernel(slot_ref, xn_ref, xs_ref, sem):
    def issue(g, _):
        for k in range(TOP_K):
            for j in range(SUBLANES):
                t = g * SUBLANES + j
                pltpu.make_async_copy(xn_ref.at[t], xs_ref.at[slot_ref[k * MOE_TILE + t]],
                                      sem).start(priority=j % 2)
        return 0

    lax.fori_loop(0, MOE_TILE // SUBLANES, issue, 0)
    for _ in range(TOP_K):
        pltpu.make_async_copy(xn_ref, xs_ref.at[pl.ds(0, MOE_TILE)], sem).wait()


def _dispatch(slots_flat, xn3):
    n = xn3.shape[0]
    return pl.pallas_call(
        _dispatch_kernel,
        grid=(n // MOE_TILE,),
        in_specs=[
            pl.BlockSpec((TOP_K * MOE_TILE,), lambda i: (i,), memory_space=pltpu.SMEM),
            pl.BlockSpec((MOE_TILE, ROW_TILES, LANES), lambda i: (i, 0, 0)),
        ],
        out_specs=pl.BlockSpec(memory_space=pl.ANY),
        out_shape=jax.ShapeDtypeStruct((n * TOP_K, ROW_TILES, LANES), F32),
        scratch_shapes=[pltpu.SemaphoreType.DMA],
        compiler_params=_cparams(1),
        name="dispatch",
    )(slots_flat, xn3)


FLAG_VALID, FLAG_FIRST_VISIT, FLAG_NEW_EXPERT = 1, 2, 4


def _experts_kernel(blk_ref, exp_ref, nexp_ref, flag_ref, start_ref,
                    xs_ref, wup_hbm_ref, bup_ref, wdn_hbm_ref, bdn_ref, ys_ref,
                    wup_bf_ref, wdn_bf_ref, wup_f32_ref, wdn_f32_ref, keep_ref, wsem):
    i = pl.program_id(0)
    flags = flag_ref[i]

    def weight_copies(e):
        return (pltpu.make_async_copy(wup_hbm_ref.at[e], wup_f32_ref, wsem.at[0]),
                pltpu.make_async_copy(wdn_hbm_ref.at[e], wdn_f32_ref, wsem.at[1]))

    @pl.when(i == 0)
    def _():
        for cp in weight_copies(exp_ref[0]):
            cp.start(priority=1)

    @pl.when((flags & FLAG_NEW_EXPERT) != 0)
    def _():
        for cp in weight_copies(exp_ref[i]):
            cp.wait()
        wup_bf_ref[...] = wup_f32_ref[...].astype(BF16)
        wdn_bf_ref[...] = wdn_f32_ref[...].astype(BF16)
        nxt = nexp_ref[i]

        @pl.when(nxt >= 0)
        def _():
            for cp in weight_copies(nxt):
                cp.start(priority=1)

    revisit = ((flags & FLAG_VALID) != 0) & ((flags & FLAG_FIRST_VISIT) == 0)

    @pl.when(revisit)
    def _():
        keep_ref[...] = _load_row_tiled(ys_ref, ROW_BLK)

    @pl.when((flags & FLAG_VALID) != 0)
    def _():
        x = _load_row_tiled(xs_ref, ROW_BLK).astype(BF16)
        acts = []
        for c in range(D_FF // FF_CHUNK):
            gcols = pl.ds(c * FF_CHUNK, FF_CHUNK)
            ucols = pl.ds(D_FF + c * FF_CHUNK, FF_CHUNK)
            g = jnp.dot(x, wup_bf_ref[:, gcols], preferred_element_type=F32) + bup_ref[0, :, gcols]
            up = jnp.dot(x, wup_bf_ref[:, ucols], preferred_element_type=F32) + bup_ref[0, :, ucols]
            g = jnp.minimum(g, SWIGLU_LIMIT)
            up = jnp.clip(up, -SWIGLU_LIMIT, SWIGLU_LIMIT)
            acts.append(((up + 1.0) * g * jax.nn.sigmoid(SWIGLU_ALPHA * g)).astype(BF16))
        y = jnp.dot(jnp.concatenate(acts, axis=1), wdn_bf_ref[...],
                    preferred_element_type=F32) + bdn_ref[0]
        for c in range(ROW_TILES):
            _lane_tile(ys_ref, c, ROW_BLK)[...] = y[:, c * LANES:(c + 1) * LANES]

    @pl.when(revisit)
    def _():
        row = blk_ref[i] * ROW_BLK + lax.broadcasted_iota(jnp.int32, (ROW_BLK, 1), 0)
        earlier = jnp.broadcast_to(row < start_ref[exp_ref[i]], (ROW_BLK, LANES))
        for c in range(ROW_TILES):
            tile = _lane_tile(ys_ref, c, ROW_BLK)
            tile[...] = jnp.where(earlier, keep_ref[:, c * LANES:(c + 1) * LANES], tile[...])


def _experts(item_blk, item_exp, item_nexp, item_flag, starts, xs2d, w_up, b_up3, w_down, b_down3):
    n_items = item_blk.shape[0]
    rows_blk = pl.BlockSpec((ROW_BLK * ROW_TILES, LANES), lambda i, b, e, x, f, s: (b[i], 0))
    grid_spec = pltpu.PrefetchScalarGridSpec(
        num_scalar_prefetch=5,
        grid=(n_items,),
        in_specs=[
            rows_blk,
            pl.BlockSpec(memory_space=pl.ANY),
            pl.BlockSpec((1, 1, 2 * D_FF), lambda i, b, e, x, f, s: (e[i], 0, 0)),
            pl.BlockSpec(memory_space=pl.ANY),
            pl.BlockSpec((1, 1, D_MODEL), lambda i, b, e, x, f, s: (e[i], 0, 0)),
        ],
        out_specs=rows_blk,
        scratch_shapes=[
            pltpu.VMEM((D_MODEL, 2 * D_FF), BF16),
            pltpu.VMEM((D_FF, D_MODEL), BF16),
            pltpu.VMEM((D_MODEL, 2 * D_FF), F32),
            pltpu.VMEM((D_FF, D_MODEL), F32),
            pltpu.VMEM((ROW_BLK, D_MODEL), F32),
            pltpu.SemaphoreType.DMA((2,)),
        ],
    )
    return pl.pallas_call(
        _experts_kernel,
        grid_spec=grid_spec,
        out_shape=jax.ShapeDtypeStruct(xs2d.shape, F32),
        compiler_params=_cparams(1),
        name="experts",
    )(item_blk, item_exp, item_nexp, item_flag, starts, xs2d, w_up, b_up3, w_down, b_down3)


def _build_items(counts, n_rows):
    ends = jnp.cumsum(counts)
    starts = ends - counts
    n_blocks = n_rows // ROW_BLK
    n_items = n_blocks + N_ITEMS_EXTRA
    lo = jnp.arange(n_blocks, dtype=jnp.int32)[:, None] * ROW_BLK
    overlap = jnp.minimum(ends[None, :], lo + ROW_BLK) - jnp.maximum(starts[None, :], lo)
    live = (overlap > 0).reshape(-1)
    n_live = jnp.sum(live.astype(jnp.int32))
    (pos,) = jnp.nonzero(live, size=n_items, fill_value=0)
    pos = pos.astype(jnp.int32)
    k = jnp.arange(n_items, dtype=jnp.int32)
    valid = k < n_live
    pos = jnp.where(valid, pos, pos[jnp.maximum(n_live - 1, 0)])
    blk = pos // N_EXPERTS
    exp = pos % N_EXPERTS
    prev_blk = jnp.concatenate([jnp.full((1,), -1, jnp.int32), blk[:-1]])
    prev_exp = jnp.concatenate([jnp.full((1,), -1, jnp.int32), exp[:-1]])
    new_exp = valid & (exp != prev_exp)
    flags = (jnp.where(valid, FLAG_VALID, 0)
             | jnp.where(valid & (blk != prev_blk), FLAG_FIRST_VISIT, 0)
             | jnp.where(new_exp, FLAG_NEW_EXPERT, 0)).astype(jnp.int32)
    first_at = jnp.where(new_exp, k, n_items)
    next_first = jnp.flip(lax.cummin(jnp.flip(first_at)))
    next_first = jnp.concatenate([next_first[1:], jnp.full((1,), n_items, jnp.int32)])
    nexp = jnp.where(next_first < n_items, exp[jnp.minimum(next_first, n_items - 1)], -1)
    starts33 = jnp.concatenate([starts, ends[-1:]]).astype(jnp.int32)
    return blk, exp, nexp.astype(jnp.int32), flags, starts33


def _combine_kernel(slot_cur_ref, slot_next_ref, ys3_ref, ys2d_ref, x1_ref, gate_ref, p_ref,
                    gpg_ref, wpg_ref, wpp_ref, gpp_ref, out_ref, rows_ref, sem):
    i = pl.program_id(0)
    last = pl.num_programs(0) - 1
    cur = i % 2
    nxt = 1 - cur

    def start(slot_ref, buf):
        for k in range(TOP_K):
            for t in range(MOE_TILE):
                pltpu.make_async_copy(ys3_ref.at[slot_ref[k * MOE_TILE + t]],
                                      rows_ref.at[buf, k, pl.ds(t * ROW_TILES, ROW_TILES)],
                                      sem.at[buf]).start(priority=t % 2)

    def wait(buf):
        for k in range(TOP_K):
            pltpu.make_async_copy(ys2d_ref.at[pl.ds(0, MOE_TILE * ROW_TILES)],
                                  rows_ref.at[buf, k], sem.at[buf]).wait()

    @pl.when(i == 0)
    def _():
        start(slot_cur_ref, 0)

    wait(cur)
    gate = jnp.concatenate([gate_ref[...], jnp.zeros((LANES - SUBLANES, MOE_TILE), F32)],
                           axis=0).T
    moe = []
    for c in range(ROW_TILES):
        acc = gate[:, 0:1] * _lane_tile(rows_ref.at[cur, 0], c, MOE_TILE)[...]
        for k in range(1, TOP_K):
            acc = acc + gate[:, k:k + 1] * _lane_tile(rows_ref.at[cur, k], c, MOE_TILE)[...]
        moe.append(acc)
    x2 = x1_ref[...] + jnp.concatenate(moe, axis=1)
    start(slot_next_ref, nxt)
    ple = _rms(jnp.dot(p_ref[...].astype(BF16), wpp_ref[...], preferred_element_type=F32),
               gpp_ref[...])
    pg = jax.nn.sigmoid(jnp.dot(_rms(x2, gpg_ref[...]).astype(BF16), wpg_ref[...],
                                preferred_element_type=F32))
    out_ref[...] = x2 + pg * ple

    @pl.when(i == last)
    def _():
        wait(nxt)


def _combine(slots_flat, ys2d, x1, gate, p2d, gpg, wpg, wpp, gpp):
    n = x1.shape[0]
    n_tiles = n // MOE_TILE
    full = lambda shape: pl.BlockSpec(shape, lambda i: (0,) * len(shape))
    row = lambda w: pl.BlockSpec((MOE_TILE, w), lambda i: (i, 0))
    return pl.pallas_call(
        _combine_kernel,
        grid=(n_tiles,),
        in_specs=[
            pl.BlockSpec((TOP_K * MOE_TILE,), lambda i: (i,), memory_space=pltpu.SMEM),
            pl.BlockSpec((TOP_K * MOE_TILE,), lambda i: (jnp.minimum(i + 1, n_tiles - 1),),
                         memory_space=pltpu.SMEM),
            pl.BlockSpec(memory_space=pl.ANY), pl.BlockSpec(memory_space=pl.ANY),
            row(D_MODEL), pl.BlockSpec((SUBLANES, MOE_TILE), lambda i: (0, i)), row(PLE_DIM),
            full((1, D_MODEL)), full((D_MODEL, D_MODEL)), full((PLE_DIM, D_MODEL)),
            full((1, D_MODEL)),
        ],
        out_specs=row(D_MODEL),
        out_shape=jax.ShapeDtypeStruct((n, D_MODEL), F32),
        scratch_shapes=[pltpu.VMEM((2, TOP_K, MOE_TILE * ROW_TILES, LANES), F32),
                        pltpu.SemaphoreType.DMA((2,))],
        compiler_params=_cparams(1),
        name="combine",
    )(slots_flat, slots_flat, ys2d.reshape(-1, ROW_TILES, LANES), ys2d, x1, gate, p2d,
      gpg, wpg, wpp, gpp)


def kernel(x, p, g_mix, w_in, g_q, g_k, w_attn_branch, a_re, a_im, log_dt, b_re, b_im, c_re, c_im, d_skip, w_glu, b_glu, w_out, g_ffn, w_router, b_router, w_up, b_up, w_down, b_down, g_ple_gate, w_ple_gate, w_ple_proj, g_ple_post):
    bsz, s_len, d = x.shape
    depth = w_in.shape[0]
    n = bsz * s_len
    assert d == D_MODEL and s_len % SCAN_CHUNK == 0
    assert all(s_len % t == 0 for t in (MERGE_SUBTILES * TOK_TILE, IN_TILE, MOE_TILE))
    assert bsz == SUBLANES, "the S5 scan keeps the batch on the sublane axis"

    x2d = x.reshape(n, d)
    for i in range(depth):
        qs, kn, vb, u_tm, gates = _inproj(x2d, g_mix[i][None], w_in[i].astype(BF16),
                                          jnp.tile(g_q[i], 2)[None], jnp.tile(g_k[i], 2)[None],
                                          bsz, s_len)
        o = _attn(qs, kn.reshape(bsz, s_len, SB_WIDTH), vb.reshape(bsz, s_len, SB_WIDTH))
        a_bc, b_cat, c_cat = _s5_params(a_re[i], a_im[i], log_dt[i], b_re[i], b_im[i],
                                        c_re[i], c_im[i], bsz)
        y_tm = _s5(u_tm.reshape(s_len, bsz, SSM_WIDTH), a_bc, b_cat, c_cat, d_skip[i][None])

        wr_t = w_router[i].T
        wr_hi = wr_t.astype(BF16)
        wr_lo = (wr_t - wr_hi.astype(F32)).astype(BF16)
        x1, xn, idx, gate, rank, counts = _merge(
            x2d, o.reshape(n, SB_WIDTH), y_tm.reshape(s_len, bsz * SSM_WIDTH), gates,
            w_attn_branch[i].astype(BF16), w_glu[i].astype(BF16), b_glu[i][None],
            w_out[i].astype(BF16), g_ffn[i][None], jnp.concatenate([wr_hi, wr_lo], axis=0),
            b_router[i][:, None], s_len)

        counts_i = counts[:, 0].astype(jnp.int32)
        blk, exp, nexp, flags, starts33 = _build_items(counts_i, n * TOP_K)
        hot = idx[:TOP_K, :, None] == jnp.arange(N_EXPERTS, dtype=jnp.int32)
        slots = jnp.sum(jnp.where(hot, starts33[:N_EXPERTS], 0), axis=-1) + rank[:TOP_K]
        slots_flat = slots.reshape(TOP_K, n // MOE_TILE, MOE_TILE).transpose(1, 0, 2).reshape(-1)

        xs3 = _dispatch(slots_flat, xn.reshape(n, ROW_TILES, LANES))
        ys2d = _experts(blk, exp, nexp, flags, starts33, xs3.reshape(n * TOP_K * ROW_TILES, LANES),
                        w_up[i], b_up[i][:, None, :], w_down[i], b_down[i][:, None, :])
        x2d = _combine(slots_flat, ys2d, x1, gate, p[i].reshape(n, PLE_DIM),
                       g_ple_gate[i][None], w_ple_gate[i].astype(BF16),
                       w_ple_proj[i].astype(BF16), g_ple_post[i][None])
    return x2d.reshape(bsz, s_len, d)
```

```python
import jax
import jax.numpy as jnp
from jax import lax
from jax.experimental import pallas as pl
from jax.experimental.pallas import tpu as pltpu

F32 = jnp.float32
BF16 = jnp.bfloat16

D_MODEL = 1024
SB_HEADS = 8
SB_HEAD_DIM = 64
SB_WIDTH = SB_HEADS * SB_HEAD_DIM
SSM_GROUP = 16
SSM_WIDTH = 512
SSM_GROUPS = SSM_WIDTH // SSM_GROUP
SSM_STATE = 64
PLE_DIM = 256
N_EXPERTS = 32
TOP_K = 4
D_FF = D_MODEL
SWIGLU_LIMIT = 7.0
SWIGLU_ALPHA = 1.702
EPS = 1e-6

LANES = 128
SUBLANES = 8
VMEM_LIMIT = 56 * 1024 * 1024

TOK_TILE = 256
IN_TILE = 512
MOE_TILE = 512
DISPATCH_TILE = 1024
ROW_TILES = D_MODEL // LANES
assert ROW_TILES == SUBLANES
MERGE_SUBTILES = 2
ATT_BLK = 128
DEAD_LOG_WEIGHT = -104.0
HEAD_PAIRS = SB_WIDTH // LANES
SCAN_CHUNK = 64
SCAN_LANES = 512
HALF_U = SSM_WIDTH // 2
HALF_STATE = SSM_GROUPS // 2 * SSM_STATE
ROW_BLK = 256
FF_CHUNK = 256
N_ITEMS_EXTRA = N_EXPERTS - 1


def _cparams(n_axes):
    return pltpu.CompilerParams(
        dimension_semantics=("arbitrary",) * n_axes,
        vmem_limit_bytes=VMEM_LIMIT)


def _rms(x, g):
    ms = jnp.mean(x * x, axis=-1, keepdims=True)
    return x * lax.rsqrt(ms + EPS) * g


def _inproj_kernel(x_ref, g_ref, w_ref, gq_ref, gk_ref, qs_ref, k_ref, v_ref, u_ref, gates_ref):
    h = _rms(x_ref[...], g_ref[...]).astype(BF16)
    n_qkv = 3 * SB_WIDTH
    n_u = n_qkv + SSM_WIDTH
    qkv = jnp.dot(h, w_ref[:, :n_qkv], preferred_element_type=F32)
    u_ref[...] = jnp.dot(h, w_ref[:, n_qkv:n_u], preferred_element_type=F32)
    gates_ref[...] = jnp.dot(h, w_ref[:, n_u:], preferred_element_type=F32)

    head0 = lax.broadcasted_iota(jnp.int32, (1, LANES), 1) < SB_HEAD_DIM

    def head_rms(t, g):
        sq = t * t
        s0 = jnp.sum(jnp.where(head0, sq, 0.0), axis=-1, keepdims=True)
        s1 = jnp.sum(jnp.where(head0, 0.0, sq), axis=-1, keepdims=True)
        ms = jnp.where(head0, s0, s1) * (1.0 / SB_HEAD_DIM)
        return t * lax.rsqrt(ms + EPS) * g

    scale = SB_HEAD_DIM ** -0.5
    for p in range(HEAD_PAIRS):
        cols = slice(p * LANES, (p + 1) * LANES)
        qn = head_rms(qkv[:, cols], gq_ref[...]) * scale
        q0 = jnp.where(head0, qn, 0.0).astype(BF16)
        q1 = jnp.where(head0, 0.0, qn).astype(BF16)
        for t in range(IN_TILE // ATT_BLK):
            rows = slice(t * ATT_BLK, (t + 1) * ATT_BLK)
            qs_ref[0, p, t, :ATT_BLK, :] = q0[rows]
            qs_ref[0, p, t, ATT_BLK:, :] = q1[rows]
        kcols = slice(SB_WIDTH + p * LANES, SB_WIDTH + (p + 1) * LANES)
        k_ref[:, cols] = head_rms(qkv[:, kcols], gk_ref[...]).astype(BF16)
    v_ref[...] = qkv[:, 2 * SB_WIDTH:].astype(BF16)


def _inproj(x2d, g_mix, w_in_bf, gq2, gk2, bsz, s_len):
    n = x2d.shape[0]
    tiles_per_seq = s_len // IN_TILE
    blks_per_tile = IN_TILE // ATT_BLK
    in_cols = w_in_bf.shape[1]
    return pl.pallas_call(
        _inproj_kernel,
        grid=(n // IN_TILE,),
        in_specs=[
            pl.BlockSpec((IN_TILE, D_MODEL), lambda i: (i, 0)),
            pl.BlockSpec((1, D_MODEL), lambda i: (0, 0)),
            pl.BlockSpec((D_MODEL, in_cols), lambda i: (0, 0)),
            pl.BlockSpec((1, LANES), lambda i: (0, 0)),
            pl.BlockSpec((1, LANES), lambda i: (0, 0)),
        ],
        out_specs=[
            pl.BlockSpec((1, HEAD_PAIRS, blks_per_tile, 2 * ATT_BLK, LANES),
                         lambda i: (i // tiles_per_seq, 0, i % tiles_per_seq, 0, 0)),
            pl.BlockSpec((IN_TILE, SB_WIDTH), lambda i: (i, 0)),
            pl.BlockSpec((IN_TILE, SB_WIDTH), lambda i: (i, 0)),
            pl.BlockSpec((IN_TILE, SSM_WIDTH),
                         lambda i: (i % tiles_per_seq, i // tiles_per_seq)),
            pl.BlockSpec((IN_TILE, 2 * D_MODEL), lambda i: (i, 0)),
        ],
        out_shape=[
            jax.ShapeDtypeStruct((bsz, HEAD_PAIRS, s_len // ATT_BLK, 2 * ATT_BLK, LANES), BF16),
            jax.ShapeDtypeStruct((n, SB_WIDTH), BF16),
            jax.ShapeDtypeStruct((n, SB_WIDTH), BF16),
            jax.ShapeDtypeStruct((s_len, bsz * SSM_WIDTH), F32),
            jax.ShapeDtypeStruct((n, 2 * D_MODEL), F32),
        ],
        compiler_params=_cparams(1),
        name="inproj",
    )(x2d, g_mix, w_in_bf, gq2, gk2)


def _attn_kernel(qs_ref, ks_ref, vs_ref, o_ref, w2_ref, c_ref, acc_ref):
    s_len = ks_ref.shape[1]
    n_blk = s_len // ATT_BLK
    n_pairs = HEAD_PAIRS
    head0 = lax.broadcasted_iota(jnp.int32, (1, LANES), 1) < SB_HEAD_DIM

    r = lax.broadcasted_iota(jnp.int32, (2 * ATT_BLK, 2 * ATT_BLK), 0)
    c = lax.broadcasted_iota(jnp.int32, (2 * ATT_BLK, 2 * ATT_BLK), 1)
    r = jnp.where(r >= ATT_BLK, r - ATT_BLK, r)
    w2_ref[...] = jnp.where((c >= ATT_BLK) | (r > c), 1.0, 0.0).astype(BF16)

    ti = lax.broadcasted_iota(jnp.int32, (2 * ATT_BLK, ATT_BLK), 0)
    si = lax.broadcasted_iota(jnp.int32, (2 * ATT_BLK, ATT_BLK), 1)
    causal = si < jnp.where(ti >= ATT_BLK, ti - ATT_BLK, ti)

    def rows_of(blk):
        if isinstance(blk, int):
            return pl.ds(blk * ATT_BLK, ATT_BLK)
        return pl.ds(pl.multiple_of(blk * ATT_BLK, ATT_BLK), ATT_BLK)

    def tiles(sweeps, diag):
        chains = [(w, qi, rows_of(kv), p) for (w, qi, kv) in sweeps for p in range(n_pairs)]
        z = [lax.dot_general(qs_ref[0, p, qi], ks_ref[0, kv_rows, p * LANES:(p + 1) * LANES],
                             (((1,), (1,)), ((), ())),
                             preferred_element_type=F32)
             for (w, qi, kv_rows, p) in chains]
        log_beta, stacked = [], []
        for zi in z:
            sp = jnp.maximum(zi, 0.0) + jnp.log(1.0 + jnp.exp(-jnp.abs(zi)))
            log_keep = -sp
            log_beta.append(zi - sp)
            if diag:
                log_keep = jnp.where(causal, log_keep, 0.0)
            hi = log_keep.astype(BF16)
            lo = (log_keep - hi.astype(F32)).astype(BF16)
            stacked.append(jnp.concatenate([hi, lo], axis=1))
        sums = [jnp.dot(st, w2_ref[...], preferred_element_type=F32) for st in stacked]
        weights = []
        for i, (w, qi, kv_rows, p) in enumerate(chains):
            wp = jnp.exp(log_beta[i] + sums[i][:, :ATT_BLK] + c_ref[w, p])
            if diag:
                wp = jnp.where(causal, wp, 0.0)
            weights.append(wp.astype(BF16))
            c_ref[w, p] += sums[i][:, ATT_BLK:]
        pv = [jnp.dot(weights[i], vs_ref[0, kv_rows, p * LANES:(p + 1) * LANES],
                      preferred_element_type=F32)
              for i, (w, qi, kv_rows, p) in enumerate(chains)]
        for i, (w, qi, kv_rows, p) in enumerate(chains):
            acc_ref[w, p] += pv[i]

    def qblocks(qb, _):
        qa = 2 * qb
        qc = qa + 1
        c_ref[...] = jnp.zeros_like(c_ref)
        acc_ref[...] = jnp.zeros_like(acc_ref)
        tiles([(0, qa, qa), (1, qc, qc)], True)

        def live(carry):
            jj, c_max = carry
            return (jj < qa) & (c_max > DEAD_LOG_WEIGHT)

        def kvblock(carry):
            jj, _ = carry
            tiles([(0, qa, qa - 1 - jj), (1, qc, qa - jj)], False)
            return jj + 1, jnp.max(c_ref[...])

        jj, _ = lax.while_loop(live, kvblock, (jnp.int32(0), jnp.max(c_ref[...])))

        @pl.when((jj == qa) & (jnp.max(c_ref[1]) > DEAD_LOG_WEIGHT))
        def _():
            tiles([(1, qc, 0)], False)

        for w, qi in ((0, qa), (1, qc)):
            for p in range(n_pairs):
                o_ref[0, rows_of(qi), p * LANES:(p + 1) * LANES] = jnp.where(
                    head0, acc_ref[w, p, :ATT_BLK, :], acc_ref[w, p, ATT_BLK:, :])
        return 0

    lax.fori_loop(0, n_blk // 2, qblocks, 0)


def _attn(qs, k3, v3):
    bsz, s_len, _ = k3.shape
    kv_blk = pl.BlockSpec((1, s_len, SB_WIDTH), lambda b: (b, 0, 0))
    return pl.pallas_call(
        _attn_kernel,
        grid=(bsz,),
        in_specs=[
            pl.BlockSpec((1,) + qs.shape[1:], lambda b: (b, 0, 0, 0, 0)),
            kv_blk, kv_blk,
        ],
        out_specs=pl.BlockSpec((1, s_len, SB_WIDTH), lambda b: (b, 0, 0)),
        out_shape=jax.ShapeDtypeStruct((bsz, s_len, SB_WIDTH), F32),
        scratch_shapes=[
            pltpu.VMEM((2 * ATT_BLK, 2 * ATT_BLK), BF16),
            pltpu.VMEM((2, HEAD_PAIRS, 2 * ATT_BLK, ATT_BLK), F32),
            pltpu.VMEM((2, HEAD_PAIRS, 2 * ATT_BLK, LANES), F32),
        ],
        compiler_params=_cparams(1),
        name="attn",
    )(qs, k3, v3)


def _s5_kernel(u_ref, a_ref, b_ref, c_ref, d_ref, y_ref, hbuf_ref, state_ref):
    tc, bsz, _ = u_ref.shape
    rows = tc * bsz

    @pl.when(pl.program_id(0) == 0)
    def _():
        state_ref[...] = jnp.zeros_like(state_ref)

    u2 = u_ref[...].reshape(rows, SSM_WIDTH)
    ub = u2.astype(BF16)
    halves = range(2)
    for hf in halves:
        uh = ub[:, hf * HALF_U:(hf + 1) * HALF_U]
        xh = jnp.dot(uh, b_ref[hf], preferred_element_type=F32)
        hbuf_ref[hf] = xh.reshape(tc, bsz, 2 * HALF_STATE)
    for hf in halves:
        for lc in range(HALF_STATE // SCAN_LANES):
            re = pl.ds(lc * SCAN_LANES, SCAN_LANES)
            im = pl.ds(HALF_STATE + lc * SCAN_LANES, SCAN_LANES)
            ar = a_ref[hf, :, re]
            ai = a_ref[hf, :, im]
            hr = state_ref[hf, :, re]
            hi = state_ref[hf, :, im]
            for t in range(tc):
                hr, hi = (ar * hr - ai * hi + hbuf_ref[hf, t, :, re],
                          ar * hi + ai * hr + hbuf_ref[hf, t, :, im])
                hbuf_ref[hf, t, :, re] = hr
                hbuf_ref[hf, t, :, im] = hi
            state_ref[hf, :, re] = hr
            state_ref[hf, :, im] = hi
        hb = hbuf_ref[hf].reshape(rows, 2 * HALF_STATE).astype(BF16)
        yh = jnp.dot(hb, c_ref[hf], preferred_element_type=F32)
        cols = slice(hf * HALF_U, (hf + 1) * HALF_U)
        yh = yh + d_ref[:, cols] * u2[:, cols]
        y_ref[:, :, cols] = yh.reshape(tc, bsz, HALF_U)


def _s5(u_tm3, a_bc, b_cat, c_cat, d_skip):
    s_len, bsz, _ = u_tm3.shape
    return pl.pallas_call(
        _s5_kernel,
        grid=(s_len // SCAN_CHUNK,),
        in_specs=[
            pl.BlockSpec((SCAN_CHUNK, bsz, SSM_WIDTH), lambda c: (c, 0, 0)),
            pl.BlockSpec((2, bsz, 2 * HALF_STATE), lambda c: (0, 0, 0)),
            pl.BlockSpec((2, HALF_U, 2 * HALF_STATE), lambda c: (0, 0, 0)),
            pl.BlockSpec((2, 2 * HALF_STATE, HALF_U), lambda c: (0, 0, 0)),
            pl.BlockSpec((1, SSM_WIDTH), lambda c: (0, 0)),
        ],
        out_specs=pl.BlockSpec((SCAN_CHUNK, bsz, SSM_WIDTH), lambda c: (c, 0, 0)),
        out_shape=jax.ShapeDtypeStruct((s_len, bsz, SSM_WIDTH), F32),
        scratch_shapes=[
            pltpu.VMEM((2, SCAN_CHUNK, bsz, 2 * HALF_STATE), F32),
            pltpu.VMEM((2, bsz, 2 * HALF_STATE), F32),
        ],
        compiler_params=_cparams(1),
        name="s5",
    )(u_tm3, a_bc, b_cat, c_cat, d_skip)


def _s5_params(a_re, a_im, log_dt, b_re, b_im, c_re, c_im, bsz):
    dt = jnp.exp(log_dt)[:, None]
    mag = jnp.exp(a_re * dt)
    abar_r = mag * jnp.cos(a_im * dt)
    abar_i = mag * jnp.sin(a_im * dt)
    den = a_re * a_re + a_im * a_im
    nr = abar_r - 1.0
    ni = abar_i
    fr = (nr * a_re + ni * a_im) / den
    fi = (ni * a_re - nr * a_im) / den
    bbar_r = fr[..., None] * b_re - fi[..., None] * b_im
    bbar_i = fr[..., None] * b_im + fi[..., None] * b_re
    gh = SSM_GROUPS // 2
    eye = jnp.eye(gh, dtype=F32)

    def a_half(hf):
        sl = slice(hf * gh, (hf + 1) * gh)
        row = jnp.concatenate([abar_r[sl].reshape(-1), abar_i[sl].reshape(-1)])
        return jnp.broadcast_to(row[None], (bsz, 2 * HALF_STATE))

    def b_half(bb, hf):
        blk = bb[hf * gh:(hf + 1) * gh]
        return jnp.einsum('gpc,gk->gckp', blk, eye).reshape(HALF_U, HALF_STATE)

    def c_half(cc, hf):
        blk = cc[hf * gh:(hf + 1) * gh]
        return jnp.einsum('gcp,gk->gpkc', blk, eye).reshape(HALF_STATE, HALF_U)

    a_bc = jnp.stack([a_half(0), a_half(1)])
    b_cat = jnp.stack([jnp.concatenate([b_half(bbar_r, hf), b_half(bbar_i, hf)], axis=1)
                       for hf in range(2)]).astype(BF16)
    c_cat = jnp.stack([jnp.concatenate([c_half(c_re, hf), -c_half(c_im, hf)], axis=0)
                       for hf in range(2)]).astype(BF16)
    return a_bc, b_cat, c_cat


def _merge_kernel(x_ref, o_ref, y_ref, gates_ref, wab_ref, wglu_ref, bglu_ref, wout_ref,
                  gffn_ref, wr2_ref, br_ref,
                  x1_ref, xn_ref, idx_ref, gate_ref, rank_ref, cnt_ref, carry_ref):
    @pl.when(pl.program_id(0) == 0)
    def _():
        carry_ref[...] = jnp.zeros_like(carry_ref)

    subs = [pl.ds(s * TOK_TILE, TOK_TILE) for s in range(MERGE_SUBTILES)]
    nt = (((1,), (1,)), ((), ()))

    attn_branch = [jnp.dot(o_ref[s, :].astype(BF16), wab_ref[...], preferred_element_type=F32)
                   for s in subs]
    zg = [jnp.dot(jax.nn.gelu(y_ref[s, :]).astype(BF16), wglu_ref[...],
                  preferred_element_type=F32) + bglu_ref[...] for s in subs]
    mixed = []
    for i, s in enumerate(subs):
        ssm_branch = zg[i][:, :D_MODEL] * jax.nn.sigmoid(zg[i][:, D_MODEL:])
        mixed.append((jax.nn.sigmoid(gates_ref[s, :D_MODEL]) * attn_branch[i]
                      + jax.nn.sigmoid(gates_ref[s, D_MODEL:]) * ssm_branch).astype(BF16))
    x1 = [x_ref[s, :] + jnp.dot(mixed[i], wout_ref[...], preferred_element_type=F32)
          for i, s in enumerate(subs)]
    xh, xl = [], []
    for i, s in enumerate(subs):
        x1_ref[s, :] = x1[i]
        xn = _rms(x1[i], gffn_ref[...])
        for c in range(ROW_TILES):
            xn_ref[pl.ds(i * TOK_TILE * ROW_TILES + c, TOK_TILE, stride=ROW_TILES), :] = (
                xn[:, c * LANES:(c + 1) * LANES])
        xh.append(xn.astype(BF16))
        xl.append((xn - xh[i].astype(F32)).astype(BF16))
    logits = []
    for i in range(MERGE_SUBTILES):
        both = lax.dot_general(wr2_ref[...], xh[i], nt, preferred_element_type=F32)
        cross = lax.dot_general(wr2_ref[:N_EXPERTS, :], xl[i], nt, preferred_element_type=F32)
        logits.append((both[:N_EXPERTS] + both[N_EXPERTS:] + cross) + br_ref[...])

    e_iota = lax.broadcasted_iota(jnp.int32, (N_EXPERTS, TOK_TILE), 0).astype(F32)
    k_iota = lax.broadcasted_iota(jnp.int32, (SUBLANES, TOK_TILE), 0)
    rr = lax.broadcasted_iota(jnp.int32, (TOK_TILE, TOK_TILE), 0)
    cc = lax.broadcasted_iota(jnp.int32, (TOK_TILE, TOK_TILE), 1)
    earlier = jnp.where(rr < cc, 1.0, 0.0).astype(BF16)

    for i, s in enumerate(subs):
        work = logits[i]
        hits, vals = [], []
        sel = jnp.zeros((N_EXPERTS, TOK_TILE), F32)
        for _ in range(TOP_K):
            m = jnp.max(work, axis=0, keepdims=True)
            pick = jnp.min(jnp.where(work == m, e_iota, float(N_EXPERTS)), axis=0, keepdims=True)
            hit = e_iota == pick
            work = jnp.where(hit, -jnp.inf, work)
            sel = sel + jnp.where(hit, 1.0, 0.0)
            hits.append((hit, pick))
            vals.append(m)
        before = jnp.dot(sel.astype(BF16), earlier, preferred_element_type=F32) + carry_ref[...]
        exps = [jnp.exp(v - vals[0]) for v in vals]
        denom = exps[0] + exps[1] + exps[2] + exps[3]
        idx = jnp.zeros((SUBLANES, TOK_TILE), jnp.int32)
        gate = jnp.zeros((SUBLANES, TOK_TILE), F32)
        rank = jnp.zeros((SUBLANES, TOK_TILE), jnp.int32)
        for k in range(TOP_K):
            hit, pick = hits[k]
            rk = jnp.sum(jnp.where(hit, before, 0.0), axis=0, keepdims=True)
            idx = jnp.where(k_iota == k, pick.astype(jnp.int32), idx)
            gate = jnp.where(k_iota == k, exps[k] / denom, gate)
            rank = jnp.where(k_iota == k, rk.astype(jnp.int32), rank)
        idx_ref[:, s] = idx
        gate_ref[:, s] = gate
        rank_ref[:, s] = rank
        carry_ref[...] += jnp.sum(sel, axis=1, keepdims=True)
    cnt_ref[...] = jnp.broadcast_to(carry_ref[...], cnt_ref.shape)


def _merge(x2d, o2d, y_tm2, gates, wab, wglu, bglu, wout, gffn, wr2_t, br_col, s_len):
    n = x2d.shape[0]
    tile = MERGE_SUBTILES * TOK_TILE
    tiles_per_seq = s_len // tile
    full = lambda shape: pl.BlockSpec(shape, lambda i: (0,) * len(shape))
    row = lambda w: pl.BlockSpec((tile, w), lambda i: (i, 0))
    col = pl.BlockSpec((SUBLANES, tile), lambda i: (0, i))
    return pl.pallas_call(
        _merge_kernel,
        grid=(n // tile,),
        in_specs=[
            row(D_MODEL), row(SB_WIDTH),
            pl.BlockSpec((tile, SSM_WIDTH),
                         lambda i: (i % tiles_per_seq, i // tiles_per_seq)),
            row(2 * D_MODEL),
            full((SB_WIDTH, D_MODEL)), full((SSM_WIDTH, 2 * D_MODEL)), full((1, 2 * D_MODEL)),
            full((D_MODEL, D_MODEL)), full((1, D_MODEL)),
            full((2 * N_EXPERTS, D_MODEL)), full((N_EXPERTS, 1)),
        ],
        out_specs=[row(D_MODEL), pl.BlockSpec((tile * ROW_TILES, LANES), lambda i: (i, 0)),
                   col, col, col, full((N_EXPERTS, LANES))],
        out_shape=[
            jax.ShapeDtypeStruct((n, D_MODEL), F32),
            jax.ShapeDtypeStruct((n * ROW_TILES, LANES), F32),
            jax.ShapeDtypeStruct((SUBLANES, n), jnp.int32),
            jax.ShapeDtypeStruct((SUBLANES, n), F32),
            jax.ShapeDtypeStruct((SUBLANES, n), jnp.int32),
            jax.ShapeDtypeStruct((N_EXPERTS, LANES), F32),
        ],
        scratch_shapes=[pltpu.VMEM((N_EXPERTS, 1), F32)],
        compiler_params=_cparams(1),
        name="merge",
    )(x2d, o2d, y_tm2, gates, wab, wglu, bglu, wout, gffn, wr2_t, br_col)


def _lane_tile(ref2d, c, n_rows):
    return ref2d.at[pl.ds(c, n_rows, stride=ROW_TILES), :]


def _load_row_tiled(ref2d, n_rows):
    return jnp.concatenate([_lane_tile(ref2d, c, n_rows)[...] for c in range(ROW_TILES)], axis=1)


def _dispatch_kernel(slot_ref, xn_ref, xs_ref, sem):
    def issue(g, _):
        for k in range(TOP_K):
            for j in range(SUBLANES):
                t = g * SUBLANES + j
                pltpu.make_async_copy(xn_ref.at[t], xs_ref.at[slot_ref[k * DISPATCH_TILE + t]],
                                      sem).start(priority=j % 2)
        return 0

    lax.fori_loop(0, DISPATCH_TILE // SUBLANES, issue, 0)
    for _ in range(TOP_K):
        pltpu.make_async_copy(xn_ref, xs_ref.at[pl.ds(0, DISPATCH_TILE)], sem).wait()


def _dispatch(slots_flat, xn3):
    n = xn3.shape[0]
    return pl.pallas_call(
        _dispatch_kernel,
        grid=(n // DISPATCH_TILE,),
        in_specs=[
            pl.BlockSpec((TOP_K * DISPATCH_TILE,), lambda i: (i,), memory_space=pltpu.SMEM),
            pl.BlockSpec((DISPATCH_TILE, ROW_TILES, LANES), lambda i: (i, 0, 0)),
        ],
        out_specs=pl.BlockSpec(memory_space=pl.ANY),
        out_shape=jax.ShapeDtypeStruct((n * TOP_K, ROW_TILES, LANES), F32),
        scratch_shapes=[pltpu.SemaphoreType.DMA],
        compiler_params=_cparams(1),
        name="dispatch",
    )(slots_flat, xn3)


FLAG_VALID, FLAG_FIRST_VISIT, FLAG_NEW_EXPERT = 1, 2, 4


def _experts_kernel(blk_ref, exp_ref, nexp_ref, flag_ref, start_ref,
                    xs_ref, wup_hbm_ref, bup_ref, wdn_hbm_ref, bdn_ref, ys_ref,
                    wup_bf_ref, wdn_bf_ref, wup_f32_ref, wdn_f32_ref, keep_ref, wsem):
    i = pl.program_id(0)
    flags = flag_ref[i]

    def weight_copies(e):
        return (pltpu.make_async_copy(wup_hbm_ref.at[e], wup_f32_ref, wsem.at[0]),
                pltpu.make_async_copy(wdn_hbm_ref.at[e], wdn_f32_ref, wsem.at[1]))

    @pl.when(i == 0)
    def _():
        for cp in weight_copies(exp_ref[0]):
            cp.start(priority=1)

    @pl.when((flags & FLAG_NEW_EXPERT) != 0)
    def _():
        for cp in weight_copies(exp_ref[i]):
            cp.wait()
        wup_bf_ref[...] = wup_f32_ref[...].astype(BF16)
        wdn_bf_ref[...] = wdn_f32_ref[...].astype(BF16)
        nxt = nexp_ref[i]

        @pl.when(nxt >= 0)
        def _():
            for cp in weight_copies(nxt):
                cp.start(priority=1)

    revisit = ((flags & FLAG_VALID) != 0) & ((flags & FLAG_FIRST_VISIT) == 0)

    @pl.when(revisit)
    def _():
        keep_ref[...] = _load_row_tiled(ys_ref, ROW_BLK)

    @pl.when((flags & FLAG_VALID) != 0)
    def _():
        x = _load_row_tiled(xs_ref, ROW_BLK).astype(BF16)
        acts = []
        for c in range(D_FF // FF_CHUNK):
            gcols = pl.ds(c * FF_CHUNK, FF_CHUNK)
            ucols = pl.ds(D_FF + c * FF_CHUNK, FF_CHUNK)
            g = jnp.dot(x, wup_bf_ref[:, gcols], preferred_element_type=F32) + bup_ref[0, :, gcols]
            up = jnp.dot(x, wup_bf_ref[:, ucols], preferred_element_type=F32) + bup_ref[0, :, ucols]
            g = jnp.minimum(g, SWIGLU_LIMIT)
            up = jnp.clip(up, -SWIGLU_LIMIT, SWIGLU_LIMIT)
            acts.append(((up + 1.0) * g * jax.nn.sigmoid(SWIGLU_ALPHA * g)).astype(BF16))
        y = jnp.dot(jnp.concatenate(acts, axis=1), wdn_bf_ref[...],
                    preferred_element_type=F32) + bdn_ref[0]
        for c in range(ROW_TILES):
            _lane_tile(ys_ref, c, ROW_BLK)[...] = y[:, c * LANES:(c + 1) * LANES]

    @pl.when(revisit)
    def _():
        row = blk_ref[i] * ROW_BLK + lax.broadcasted_iota(jnp.int32, (ROW_BLK, 1), 0)
        earlier = jnp.broadcast_to(row < start_ref[exp_ref[i]], (ROW_BLK, LANES))
        for c in range(ROW_TILES):
            tile = _lane_tile(ys_ref, c, ROW_BLK)
            tile[...] = jnp.where(earlier, keep_ref[:, c * LANES:(c + 1) * LANES], tile[...])


def _experts(item_blk, item_exp, item_nexp, item_flag, starts, xs2d, w_up, b_up3, w_down, b_down3):
    n_items = item_blk.shape[0]
    rows_blk = pl.BlockSpec((ROW_BLK * ROW_TILES, LANES), lambda i, b, e, x, f, s: (b[i], 0))
    grid_spec = pltpu.PrefetchScalarGridSpec(
        num_scalar_prefetch=5,
        grid=(n_items,),
        in_specs=[
            rows_blk,
            pl.BlockSpec(memory_space=pl.ANY),
            pl.BlockSpec((1, 1, 2 * D_FF), lambda i, b, e, x, f, s: (e[i], 0, 0)),
            pl.BlockSpec(memory_space=pl.ANY),
            pl.BlockSpec((1, 1, D_MODEL), lambda i, b, e, x, f, s: (e[i], 0, 0)),
        ],
        out_specs=rows_blk,
        scratch_shapes=[
            pltpu.VMEM((D_MODEL, 2 * D_FF), BF16),
            pltpu.VMEM((D_FF, D_MODEL), BF16),
            pltpu.VMEM((D_MODEL, 2 * D_FF), F32),
            pltpu.VMEM((D_FF, D_MODEL), F32),
            pltpu.VMEM((ROW_BLK, D_MODEL), F32),
            pltpu.SemaphoreType.DMA((2,)),
        ],
    )
    return pl.pallas_call(
        _experts_kernel,
        grid_spec=grid_spec,
        out_shape=jax.ShapeDtypeStruct(xs2d.shape, F32),
        compiler_params=_cparams(1),
        name="experts",
    )(item_blk, item_exp, item_nexp, item_flag, starts, xs2d, w_up, b_up3, w_down, b_down3)


def _build_items(counts, n_rows):
    ends = jnp.cumsum(counts)
    starts = ends - counts
    n_items = n_rows // ROW_BLK + N_ITEMS_EXTRA
    first_blk = starts // ROW_BLK
    per_exp = jnp.where(counts > 0, (ends - 1) // ROW_BLK - first_blk + 1, 0)
    item_end = jnp.cumsum(per_exp)
    n_live = item_end[-1]
    k = jnp.arange(n_items, dtype=jnp.int32)
    valid = k < n_live
    kk = jnp.minimum(k, n_live - 1)
    exp = jnp.sum((kk[:, None] >= item_end[None, :]).astype(jnp.int32), axis=1)
    mine = exp[:, None] == jnp.arange(N_EXPERTS, dtype=jnp.int32)[None, :]
    blk = kk + jnp.sum(jnp.where(mine, (first_blk - (item_end - per_exp))[None, :], 0), axis=1)
    prev_blk = jnp.concatenate([jnp.full((1,), -1, jnp.int32), blk[:-1]])
    prev_exp = jnp.concatenate([jnp.full((1,), -1, jnp.int32), exp[:-1]])
    new_exp = valid & (exp != prev_exp)
    flags = (jnp.where(valid, FLAG_VALID, 0)
             | jnp.where(valid & (blk != prev_blk), FLAG_FIRST_VISIT, 0)
             | jnp.where(new_exp, FLAG_NEW_EXPERT, 0)).astype(jnp.int32)
    first_at = jnp.where(new_exp, k, n_items)
    next_first = jnp.flip(lax.cummin(jnp.flip(first_at)))
    next_first = jnp.concatenate([next_first[1:], jnp.full((1,), n_items, jnp.int32)])
    nexp = jnp.where(next_first < n_items, exp[jnp.minimum(next_first, n_items - 1)], -1)
    starts33 = jnp.concatenate([starts, ends[-1:]]).astype(jnp.int32)
    return blk, exp, nexp.astype(jnp.int32), flags, starts33


def _combine_kernel(slot_cur_ref, slot_next_ref, ys3_ref, ys2d_ref, x1_ref, gate_ref, p_ref,
                    gpg_ref, wpg_ref, wpp_ref, gpp_ref, out_ref, rows_ref, sem):
    i = pl.program_id(0)
    last = pl.num_programs(0) - 1
    cur = i % 2
    nxt = 1 - cur

    def start(slot_ref, buf):
        for k in range(TOP_K):
            for t in range(MOE_TILE):
                pltpu.make_async_copy(ys3_ref.at[slot_ref[k * MOE_TILE + t]],
                                      rows_ref.at[buf, k, pl.ds(t * ROW_TILES, ROW_TILES)],
                                      sem.at[buf]).start(priority=t % 2)

    def wait(buf):
        for k in range(TOP_K):
            pltpu.make_async_copy(ys2d_ref.at[pl.ds(0, MOE_TILE * ROW_TILES)],
                                  rows_ref.at[buf, k], sem.at[buf]).wait()

    @pl.when(i == 0)
    def _():
        start(slot_cur_ref, 0)

    wait(cur)
    gate = jnp.concatenate([gate_ref[...], jnp.zeros((LANES - SUBLANES, MOE_TILE), F32)],
                           axis=0).T
    moe = []
    for c in range(ROW_TILES):
        acc = gate[:, 0:1] * _lane_tile(rows_ref.at[cur, 0], c, MOE_TILE)[...]
        for k in range(1, TOP_K):
            acc = acc + gate[:, k:k + 1] * _lane_tile(rows_ref.at[cur, k], c, MOE_TILE)[...]
        moe.append(acc)
    x2 = x1_ref[...] + jnp.concatenate(moe, axis=1)
    start(slot_next_ref, nxt)
    ple = _rms(jnp.dot(p_ref[...].astype(BF16), wpp_ref[...], preferred_element_type=F32),
               gpp_ref[...])
    pg = jax.nn.sigmoid(jnp.dot(_rms(x2, gpg_ref[...]).astype(BF16), wpg_ref[...],
                                preferred_element_type=F32))
    out_ref[...] = x2 + pg * ple

    @pl.when(i == last)
    def _():
        wait(nxt)


def _combine(slots_flat, ys2d, x1, gate, p2d, gpg, wpg, wpp, gpp):
    n = x1.shape[0]
    n_tiles = n // MOE_TILE
    full = lambda shape: pl.BlockSpec(shape, lambda i: (0,) * len(shape))
    row = lambda w: pl.BlockSpec((MOE_TILE, w), lambda i: (i, 0))
    return pl.pallas_call(
        _combine_kernel,
        grid=(n_tiles,),
        in_specs=[
            pl.BlockSpec((TOP_K * MOE_TILE,), lambda i: (i,), memory_space=pltpu.SMEM),
            pl.BlockSpec((TOP_K * MOE_TILE,), lambda i: (jnp.minimum(i + 1, n_tiles - 1),),
                         memory_space=pltpu.SMEM),
            pl.BlockSpec(memory_space=pl.ANY), pl.BlockSpec(memory_space=pl.ANY),
            row(D_MODEL), pl.BlockSpec((SUBLANES, MOE_TILE), lambda i: (0, i)), row(PLE_DIM),
            full((1, D_MODEL)), full((D_MODEL, D_MODEL)), full((PLE_DIM, D_MODEL)),
            full((1, D_MODEL)),
        ],
        out_specs=row(D_MODEL),
        out_shape=jax.ShapeDtypeStruct((n, D_MODEL), F32),
        scratch_shapes=[pltpu.VMEM((2, TOP_K, MOE_TILE * ROW_TILES, LANES), F32),
                        pltpu.SemaphoreType.DMA((2,))],
        compiler_params=_cparams(1),
        name="combine",
    )(slots_flat, slots_flat, ys2d.reshape(-1, ROW_TILES, LANES), ys2d, x1, gate, p2d,
      gpg, wpg, wpp, gpp)


def kernel(x, p, g_mix, w_in, g_q, g_k, w_attn_branch, a_re, a_im, log_dt, b_re, b_im, c_re, c_im, d_skip, w_glu, b_glu, w_out, g_ffn, w_router, b_router, w_up, b_up, w_down, b_down, g_ple_gate, w_ple_gate, w_ple_proj, g_ple_post):
    bsz, s_len, d = x.shape
    depth = w_in.shape[0]
    n = bsz * s_len
    assert d == D_MODEL and s_len % SCAN_CHUNK == 0
    assert all(s_len % t == 0
               for t in (MERGE_SUBTILES * TOK_TILE, IN_TILE, MOE_TILE, DISPATCH_TILE))
    assert bsz == SUBLANES, "the S5 scan keeps the batch on the sublane axis"

    x2d = x.reshape(n, d)
    for i in range(depth):
        qs, kn, vb, u_tm, gates = _inproj(x2d, g_mix[i][None], w_in[i].astype(BF16),
                                          jnp.tile(g_q[i], 2)[None], jnp.tile(g_k[i], 2)[None],
                                          bsz, s_len)
        o = _attn(qs, kn.reshape(bsz, s_len, SB_WIDTH), vb.reshape(bsz, s_len, SB_WIDTH))
        a_bc, b_cat, c_cat = _s5_params(a_re[i], a_im[i], log_dt[i], b_re[i], b_im[i],
                                        c_re[i], c_im[i], bsz)
        y_tm = _s5(u_tm.reshape(s_len, bsz, SSM_WIDTH), a_bc, b_cat, c_cat, d_skip[i][None])

        wr_t = w_router[i].T
        wr_hi = wr_t.astype(BF16)
        wr_lo = (wr_t - wr_hi.astype(F32)).astype(BF16)
        x1, xn, idx, gate, rank, counts = _merge(
            x2d, o.reshape(n, SB_WIDTH), y_tm.reshape(s_len, bsz * SSM_WIDTH), gates,
            w_attn_branch[i].astype(BF16), w_glu[i].astype(BF16), b_glu[i][None],
            w_out[i].astype(BF16), g_ffn[i][None], jnp.concatenate([wr_hi, wr_lo], axis=0),
            b_router[i][:, None], s_len)

        counts_i = counts[:, 0].astype(jnp.int32)
        blk, exp, nexp, flags, starts33 = _build_items(counts_i, n * TOP_K)
        hot = idx[:TOP_K, :, None] == jnp.arange(N_EXPERTS, dtype=jnp.int32)
        slots = jnp.sum(jnp.where(hot, starts33[:N_EXPERTS], 0), axis=-1) + rank[:TOP_K]
        def per_tile(tile):
            return slots.reshape(TOP_K, n // tile, tile).transpose(1, 0, 2).reshape(-1)

        slots_flat = per_tile(MOE_TILE)

        xs3 = _dispatch(per_tile(DISPATCH_TILE), xn.reshape(n, ROW_TILES, LANES))
        ys2d = _experts(blk, exp, nexp, flags, starts33, xs3.reshape(n * TOP_K * ROW_TILES, LANES),
                        w_up[i], b_up[i][:, None, :], w_down[i], b_down[i][:, None, :])
        x2d = _combine(slots_flat, ys2d, x1, gate, p[i].reshape(n, PLE_DIM),
                       g_ple_gate[i][None], w_ple_gate[i].astype(BF16),
                       w_ple_proj[i].astype(BF16), g_ple_post[i][None])
    return x2d.reshape(bsz, s_len, d)
```

```python
import jax
import jax.numpy as jnp
from jax import lax
from jax.experimental import pallas as pl
from jax.experimental.pallas import tpu as pltpu

F32 = jnp.float32
BF16 = jnp.bfloat16

D_MODEL = 1024
SB_HEADS = 8
SB_HEAD_DIM = 64
SB_WIDTH = SB_HEADS * SB_HEAD_DIM
SSM_GROUP = 16
SSM_WIDTH = 512
SSM_GROUPS = SSM_WIDTH // SSM_GROUP
SSM_STATE = 64
PLE_DIM = 256
N_EXPERTS = 32
TOP_K = 4
D_FF = D_MODEL
SWIGLU_LIMIT = 7.0
SWIGLU_ALPHA = 1.702
EPS = 1e-6

LANES = 128
SUBLANES = 8
VMEM_LIMIT = 56 * 1024 * 1024

TOK_TILE = 256
IN_TILE = 512
MOE_TILE = 512
DISPATCH_TILE = 2048
ROW_TILES = D_MODEL // LANES
assert ROW_TILES == SUBLANES
MERGE_SUBTILES = 2
ATT_BLK = 128
DEAD_LOG_WEIGHT = -104.0
HEAD_PAIRS = SB_WIDTH // LANES
SCAN_CHUNK = 64
SCAN_LANES = 512
HALF_U = SSM_WIDTH // 2
HALF_STATE = SSM_GROUPS // 2 * SSM_STATE
ROW_BLK = 256
FF_CHUNK = 256
N_ITEMS_EXTRA = N_EXPERTS - 1


def _cparams(n_axes):
    return pltpu.CompilerParams(
        dimension_semantics=("arbitrary",) * n_axes,
        vmem_limit_bytes=VMEM_LIMIT)


def _rms(x, g):
    ms = jnp.mean(x * x, axis=-1, keepdims=True)
    return x * lax.rsqrt(ms + EPS) * g


def _inproj_kernel(x_ref, g_ref, w_ref, gq_ref, gk_ref, qs_ref, k_ref, v_ref, u_ref, gates_ref):
    h = _rms(x_ref[...], g_ref[...]).astype(BF16)
    n_qkv = 3 * SB_WIDTH
    n_u = n_qkv + SSM_WIDTH
    qkv = jnp.dot(h, w_ref[:, :n_qkv], preferred_element_type=F32)
    u_ref[...] = jnp.dot(h, w_ref[:, n_qkv:n_u], preferred_element_type=F32)
    gates_ref[...] = jnp.dot(h, w_ref[:, n_u:], preferred_element_type=F32)

    head0 = lax.broadcasted_iota(jnp.int32, (1, LANES), 1) < SB_HEAD_DIM

    def head_rms(t, g):
        sq = t * t
        s0 = jnp.sum(jnp.where(head0, sq, 0.0), axis=-1, keepdims=True)
        s1 = jnp.sum(jnp.where(head0, 0.0, sq), axis=-1, keepdims=True)
        ms = jnp.where(head0, s0, s1) * (1.0 / SB_HEAD_DIM)
        return t * lax.rsqrt(ms + EPS) * g

    scale = SB_HEAD_DIM ** -0.5
    for p in range(HEAD_PAIRS):
        cols = slice(p * LANES, (p + 1) * LANES)
        qn = head_rms(qkv[:, cols], gq_ref[...]) * scale
        q0 = jnp.where(head0, qn, 0.0).astype(BF16)
        q1 = jnp.where(head0, 0.0, qn).astype(BF16)
        for t in range(IN_TILE // ATT_BLK):
            rows = slice(t * ATT_BLK, (t + 1) * ATT_BLK)
            qs_ref[0, p, t, :ATT_BLK, :] = q0[rows]
            qs_ref[0, p, t, ATT_BLK:, :] = q1[rows]
        kcols = slice(SB_WIDTH + p * LANES, SB_WIDTH + (p + 1) * LANES)
        k_ref[:, cols] = head_rms(qkv[:, kcols], gk_ref[...]).astype(BF16)
    v_ref[...] = qkv[:, 2 * SB_WIDTH:].astype(BF16)


def _inproj(x2d, g_mix, w_in_bf, gq2, gk2, bsz, s_len):
    n = x2d.shape[0]
    tiles_per_seq = s_len // IN_TILE
    blks_per_tile = IN_TILE // ATT_BLK
    in_cols = w_in_bf.shape[1]
    return pl.pallas_call(
        _inproj_kernel,
        grid=(n // IN_TILE,),
        in_specs=[
            pl.BlockSpec((IN_TILE, D_MODEL), lambda i: (i, 0)),
            pl.BlockSpec((1, D_MODEL), lambda i: (0, 0)),
            pl.BlockSpec((D_MODEL, in_cols), lambda i: (0, 0)),
            pl.BlockSpec((1, LANES), lambda i: (0, 0)),
            pl.BlockSpec((1, LANES), lambda i: (0, 0)),
        ],
        out_specs=[
            pl.BlockSpec((1, HEAD_PAIRS, blks_per_tile, 2 * ATT_BLK, LANES),
                         lambda i: (i // tiles_per_seq, 0, i % tiles_per_seq, 0, 0)),
            pl.BlockSpec((IN_TILE, SB_WIDTH), lambda i: (i, 0)),
            pl.BlockSpec((IN_TILE, SB_WIDTH), lambda i: (i, 0)),
            pl.BlockSpec((IN_TILE, SSM_WIDTH),
                         lambda i: (i % tiles_per_seq, i // tiles_per_seq)),
            pl.BlockSpec((IN_TILE, 2 * D_MODEL), lambda i: (i, 0)),
        ],
        out_shape=[
            jax.ShapeDtypeStruct((bsz, HEAD_PAIRS, s_len // ATT_BLK, 2 * ATT_BLK, LANES), BF16),
            jax.ShapeDtypeStruct((n, SB_WIDTH), BF16),
            jax.ShapeDtypeStruct((n, SB_WIDTH), BF16),
            jax.ShapeDtypeStruct((s_len, bsz * SSM_WIDTH), F32),
            jax.ShapeDtypeStruct((n, 2 * D_MODEL), F32),
        ],
        compiler_params=_cparams(1),
        name="inproj",
    )(x2d, g_mix, w_in_bf, gq2, gk2)


def _attn_kernel(qs_ref, ks_ref, vs_ref, o_ref, w2_ref, c_ref, acc_ref):
    s_len = ks_ref.shape[1]
    n_blk = s_len // ATT_BLK
    n_pairs = HEAD_PAIRS
    head0 = lax.broadcasted_iota(jnp.int32, (1, LANES), 1) < SB_HEAD_DIM

    r = lax.broadcasted_iota(jnp.int32, (2 * ATT_BLK, 2 * ATT_BLK), 0)
    c = lax.broadcasted_iota(jnp.int32, (2 * ATT_BLK, 2 * ATT_BLK), 1)
    r = jnp.where(r >= ATT_BLK, r - ATT_BLK, r)
    w2_ref[...] = jnp.where((c >= ATT_BLK) | (r > c), 1.0, 0.0).astype(BF16)

    ti = lax.broadcasted_iota(jnp.int32, (2 * ATT_BLK, ATT_BLK), 0)
    si = lax.broadcasted_iota(jnp.int32, (2 * ATT_BLK, ATT_BLK), 1)
    causal = si < jnp.where(ti >= ATT_BLK, ti - ATT_BLK, ti)

    def rows_of(blk):
        if isinstance(blk, int):
            return pl.ds(blk * ATT_BLK, ATT_BLK)
        return pl.ds(pl.multiple_of(blk * ATT_BLK, ATT_BLK), ATT_BLK)

    def tiles(sweeps, diag):
        chains = [(w, qi, rows_of(kv), p) for (w, qi, kv) in sweeps for p in range(n_pairs)]
        z = [lax.dot_general(qs_ref[0, p, qi], ks_ref[0, kv_rows, p * LANES:(p + 1) * LANES],
                             (((1,), (1,)), ((), ())),
                             preferred_element_type=F32)
             for (w, qi, kv_rows, p) in chains]
        log_beta, stacked = [], []
        for zi in z:
            sp = jnp.maximum(zi, 0.0) + jnp.log(1.0 + jnp.exp(-jnp.abs(zi)))
            log_keep = -sp
            log_beta.append(zi - sp)
            if diag:
                log_keep = jnp.where(causal, log_keep, 0.0)
            hi = log_keep.astype(BF16)
            lo = (log_keep - hi.astype(F32)).astype(BF16)
            stacked.append(jnp.concatenate([hi, lo], axis=1))
        sums = [jnp.dot(st, w2_ref[...], preferred_element_type=F32) for st in stacked]
        weights = []
        for i, (w, qi, kv_rows, p) in enumerate(chains):
            wp = jnp.exp(log_beta[i] + sums[i][:, :ATT_BLK] + c_ref[w, p])
            if diag:
                wp = jnp.where(causal, wp, 0.0)
            weights.append(wp.astype(BF16))
            c_ref[w, p] += sums[i][:, ATT_BLK:]
        pv = [jnp.dot(weights[i], vs_ref[0, kv_rows, p * LANES:(p + 1) * LANES],
                      preferred_element_type=F32)
              for i, (w, qi, kv_rows, p) in enumerate(chains)]
        for i, (w, qi, kv_rows, p) in enumerate(chains):
            acc_ref[w, p] += pv[i]

    def qblocks(qb, _):
        qa = 2 * qb
        qc = qa + 1
        c_ref[...] = jnp.zeros_like(c_ref)
        acc_ref[...] = jnp.zeros_like(acc_ref)
        tiles([(0, qa, qa), (1, qc, qc)], True)

        def live(carry):
            jj, c_max = carry
            return (jj < qa) & (c_max > DEAD_LOG_WEIGHT)

        def kvblock(carry):
            jj, _ = carry
            tiles([(0, qa, qa - 1 - jj), (1, qc, qa - jj)], False)
            return jj + 1, jnp.max(c_ref[...])

        jj, _ = lax.while_loop(live, kvblock, (jnp.int32(0), jnp.max(c_ref[...])))

        @pl.when((jj == qa) & (jnp.max(c_ref[1]) > DEAD_LOG_WEIGHT))
        def _():
            tiles([(1, qc, 0)], False)

        for w, qi in ((0, qa), (1, qc)):
            for p in range(n_pairs):
                o_ref[0, rows_of(qi), p * LANES:(p + 1) * LANES] = jnp.where(
                    head0, acc_ref[w, p, :ATT_BLK, :], acc_ref[w, p, ATT_BLK:, :])
        return 0

    lax.fori_loop(0, n_blk // 2, qblocks, 0)


def _attn(qs, k3, v3):
    bsz, s_len, _ = k3.shape
    kv_blk = pl.BlockSpec((1, s_len, SB_WIDTH), lambda b: (b, 0, 0))
    return pl.pallas_call(
        _attn_kernel,
        grid=(bsz,),
        in_specs=[
            pl.BlockSpec((1,) + qs.shape[1:], lambda b: (b, 0, 0, 0, 0)),
            kv_blk, kv_blk,
        ],
        out_specs=pl.BlockSpec((1, s_len, SB_WIDTH), lambda b: (b, 0, 0)),
        out_shape=jax.ShapeDtypeStruct((bsz, s_len, SB_WIDTH), F32),
        scratch_shapes=[
            pltpu.VMEM((2 * ATT_BLK, 2 * ATT_BLK), BF16),
            pltpu.VMEM((2, HEAD_PAIRS, 2 * ATT_BLK, ATT_BLK), F32),
            pltpu.VMEM((2, HEAD_PAIRS, 2 * ATT_BLK, LANES), F32),
        ],
        compiler_params=_cparams(1),
        name="attn",
    )(qs, k3, v3)


def _s5_kernel(u_ref, a_ref, b_ref, c_ref, d_ref, y_ref, hbuf_ref, state_ref):
    tc, bsz, _ = u_ref.shape
    rows = tc * bsz

    @pl.when(pl.program_id(0) == 0)
    def _():
        state_ref[...] = jnp.zeros_like(state_ref)

    u2 = u_ref[...].reshape(rows, SSM_WIDTH)
    ub = u2.astype(BF16)
    halves = range(2)
    for hf in halves:
        uh = ub[:, hf * HALF_U:(hf + 1) * HALF_U]
        xh = jnp.dot(uh, b_ref[hf], preferred_element_type=F32)
        hbuf_ref[hf] = xh.reshape(tc, bsz, 2 * HALF_STATE)
    for hf in halves:
        for lc in range(HALF_STATE // SCAN_LANES):
            re = pl.ds(lc * SCAN_LANES, SCAN_LANES)
            im = pl.ds(HALF_STATE + lc * SCAN_LANES, SCAN_LANES)
            ar = a_ref[hf, :, re]
            ai = a_ref[hf, :, im]
            hr = state_ref[hf, :, re]
            hi = state_ref[hf, :, im]
            for t in range(tc):
                hr, hi = (ar * hr - ai * hi + hbuf_ref[hf, t, :, re],
                          ar * hi + ai * hr + hbuf_ref[hf, t, :, im])
                hbuf_ref[hf, t, :, re] = hr
                hbuf_ref[hf, t, :, im] = hi
            state_ref[hf, :, re] = hr
            state_ref[hf, :, im] = hi
        hb = hbuf_ref[hf].reshape(rows, 2 * HALF_STATE).astype(BF16)
        yh = jnp.dot(hb, c_ref[hf], preferred_element_type=F32)
        cols = slice(hf * HALF_U, (hf + 1) * HALF_U)
        yh = yh + d_ref[:, cols] * u2[:, cols]
        y_ref[:, :, cols] = yh.reshape(tc, bsz, HALF_U)


def _s5(u_tm3, a_bc, b_cat, c_cat, d_skip):
    s_len, bsz, _ = u_tm3.shape
    return pl.pallas_call(
        _s5_kernel,
        grid=(s_len // SCAN_CHUNK,),
        in_specs=[
            pl.BlockSpec((SCAN_CHUNK, bsz, SSM_WIDTH), lambda c: (c, 0, 0)),
            pl.BlockSpec((2, bsz, 2 * HALF_STATE), lambda c: (0, 0, 0)),
            pl.BlockSpec((2, HALF_U, 2 * HALF_STATE), lambda c: (0, 0, 0)),
            pl.BlockSpec((2, 2 * HALF_STATE, HALF_U), lambda c: (0, 0, 0)),
            pl.BlockSpec((1, SSM_WIDTH), lambda c: (0, 0)),
        ],
        out_specs=pl.BlockSpec((SCAN_CHUNK, bsz, SSM_WIDTH), lambda c: (c, 0, 0)),
        out_shape=jax.ShapeDtypeStruct((s_len, bsz, SSM_WIDTH), F32),
        scratch_shapes=[
            pltpu.VMEM((2, SCAN_CHUNK, bsz, 2 * HALF_STATE), F32),
            pltpu.VMEM((2, bsz, 2 * HALF_STATE), F32),
        ],
        compiler_params=_cparams(1),
        name="s5",
    )(u_tm3, a_bc, b_cat, c_cat, d_skip)


def _s5_params(a_re, a_im, log_dt, b_re, b_im, c_re, c_im, bsz):
    dt = jnp.exp(log_dt)[:, None]
    mag = jnp.exp(a_re * dt)
    abar_r = mag * jnp.cos(a_im * dt)
    abar_i = mag * jnp.sin(a_im * dt)
    den = a_re * a_re + a_im * a_im
    nr = abar_r - 1.0
    ni = abar_i
    fr = (nr * a_re + ni * a_im) / den
    fi = (ni * a_re - nr * a_im) / den
    bbar_r = fr[..., None] * b_re - fi[..., None] * b_im
    bbar_i = fr[..., None] * b_im + fi[..., None] * b_re
    gh = SSM_GROUPS // 2
    eye = jnp.eye(gh, dtype=F32)

    def a_half(hf):
        sl = slice(hf * gh, (hf + 1) * gh)
        row = jnp.concatenate([abar_r[sl].reshape(-1), abar_i[sl].reshape(-1)])
        return jnp.broadcast_to(row[None], (bsz, 2 * HALF_STATE))

    def b_half(bb, hf):
        blk = bb[hf * gh:(hf + 1) * gh]
        return jnp.einsum('gpc,gk->gckp', blk, eye).reshape(HALF_U, HALF_STATE)

    def c_half(cc, hf):
        blk = cc[hf * gh:(hf + 1) * gh]
        return jnp.einsum('gcp,gk->gpkc', blk, eye).reshape(HALF_STATE, HALF_U)

    a_bc = jnp.stack([a_half(0), a_half(1)])
    b_cat = jnp.stack([jnp.concatenate([b_half(bbar_r, hf), b_half(bbar_i, hf)], axis=1)
                       for hf in range(2)]).astype(BF16)
    c_cat = jnp.stack([jnp.concatenate([c_half(c_re, hf), -c_half(c_im, hf)], axis=0)
                       for hf in range(2)]).astype(BF16)
    return a_bc, b_cat, c_cat


def _merge_kernel(x_ref, o_ref, y_ref, gates_ref, wab_ref, wglu_ref, bglu_ref, wout_ref,
                  gffn_ref, wr2_ref, br_ref,
                  x1_ref, xn_ref, idx_ref, gate_ref, rank_ref, cnt_ref, carry_ref):
    @pl.when(pl.program_id(0) == 0)
    def _():
        carry_ref[...] = jnp.zeros_like(carry_ref)

    subs = [pl.ds(s * TOK_TILE, TOK_TILE) for s in range(MERGE_SUBTILES)]
    nt = (((1,), (1,)), ((), ()))

    attn_branch = [jnp.dot(o_ref[s, :].astype(BF16), wab_ref[...], preferred_element_type=F32)
                   for s in subs]
    zg = [jnp.dot(jax.nn.gelu(y_ref[s, :]).astype(BF16), wglu_ref[...],
                  preferred_element_type=F32) + bglu_ref[...] for s in subs]
    mixed = []
    for i, s in enumerate(subs):
        ssm_branch = zg[i][:, :D_MODEL] * jax.nn.sigmoid(zg[i][:, D_MODEL:])
        mixed.append((jax.nn.sigmoid(gates_ref[s, :D_MODEL]) * attn_branch[i]
                      + jax.nn.sigmoid(gates_ref[s, D_MODEL:]) * ssm_branch).astype(BF16))
    x1 = [x_ref[s, :] + jnp.dot(mixed[i], wout_ref[...], preferred_element_type=F32)
          for i, s in enumerate(subs)]
    xh, xl = [], []
    for i, s in enumerate(subs):
        x1_ref[s, :] = x1[i]
        xn = _rms(x1[i], gffn_ref[...])
        for c in range(ROW_TILES):
            xn_ref[pl.ds(i * TOK_TILE * ROW_TILES + c, TOK_TILE, stride=ROW_TILES), :] = (
                xn[:, c * LANES:(c + 1) * LANES])
        xh.append(xn.astype(BF16))
        xl.append((xn - xh[i].astype(F32)).astype(BF16))
    logits = []
    for i in range(MERGE_SUBTILES):
        both = lax.dot_general(wr2_ref[...], xh[i], nt, preferred_element_type=F32)
        cross = lax.dot_general(wr2_ref[:N_EXPERTS, :], xl[i], nt, preferred_element_type=F32)
        logits.append((both[:N_EXPERTS] + both[N_EXPERTS:] + cross) + br_ref[...])

    e_iota = lax.broadcasted_iota(jnp.int32, (N_EXPERTS, TOK_TILE), 0).astype(F32)
    k_iota = lax.broadcasted_iota(jnp.int32, (SUBLANES, TOK_TILE), 0)
    rr = lax.broadcasted_iota(jnp.int32, (TOK_TILE, TOK_TILE), 0)
    cc = lax.broadcasted_iota(jnp.int32, (TOK_TILE, TOK_TILE), 1)
    earlier = jnp.where(rr < cc, 1.0, 0.0).astype(BF16)

    for i, s in enumerate(subs):
        work = logits[i]
        hits, vals = [], []
        sel = jnp.zeros((N_EXPERTS, TOK_TILE), F32)
        for _ in range(TOP_K):
            m = jnp.max(work, axis=0, keepdims=True)
            pick = jnp.min(jnp.where(work == m, e_iota, float(N_EXPERTS)), axis=0, keepdims=True)
            hit = e_iota == pick
            work = jnp.where(hit, -jnp.inf, work)
            sel = sel + jnp.where(hit, 1.0, 0.0)
            hits.append((hit, pick))
            vals.append(m)
        before = jnp.dot(sel.astype(BF16), earlier, preferred_element_type=F32) + carry_ref[...]
        exps = [jnp.exp(v - vals[0]) for v in vals]
        denom = exps[0] + exps[1] + exps[2] + exps[3]
        idx = jnp.zeros((SUBLANES, TOK_TILE), jnp.int32)
        gate = jnp.zeros((SUBLANES, TOK_TILE), F32)
        rank = jnp.zeros((SUBLANES, TOK_TILE), jnp.int32)
        for k in range(TOP_K):
            hit, pick = hits[k]
            rk = jnp.sum(jnp.where(hit, before, 0.0), axis=0, keepdims=True)
            idx = jnp.where(k_iota == k, pick.astype(jnp.int32), idx)
            gate = jnp.where(k_iota == k, exps[k] / denom, gate)
            rank = jnp.where(k_iota == k, rk.astype(jnp.int32), rank)
        idx_ref[:, s] = idx
        gate_ref[:, s] = gate
        rank_ref[:, s] = rank
        carry_ref[...] += jnp.sum(sel, axis=1, keepdims=True)
    cnt_ref[...] = jnp.broadcast_to(carry_ref[...], cnt_ref.shape)


def _merge(x2d, o2d, y_tm2, gates, wab, wglu, bglu, wout, gffn, wr2_t, br_col, s_len):
    n = x2d.shape[0]
    tile = MERGE_SUBTILES * TOK_TILE
    tiles_per_seq = s_len // tile
    full = lambda shape: pl.BlockSpec(shape, lambda i: (0,) * len(shape))
    row = lambda w: pl.BlockSpec((tile, w), lambda i: (i, 0))
    col = pl.BlockSpec((SUBLANES, tile), lambda i: (0, i))
    return pl.pallas_call(
        _merge_kernel,
        grid=(n // tile,),
        in_specs=[
            row(D_MODEL), row(SB_WIDTH),
            pl.BlockSpec((tile, SSM_WIDTH),
                         lambda i: (i % tiles_per_seq, i // tiles_per_seq)),
            row(2 * D_MODEL),
            full((SB_WIDTH, D_MODEL)), full((SSM_WIDTH, 2 * D_MODEL)), full((1, 2 * D_MODEL)),
            full((D_MODEL, D_MODEL)), full((1, D_MODEL)),
            full((2 * N_EXPERTS, D_MODEL)), full((N_EXPERTS, 1)),
        ],
        out_specs=[row(D_MODEL), pl.BlockSpec((tile * ROW_TILES, LANES), lambda i: (i, 0)),
                   col, col, col, full((N_EXPERTS, LANES))],
        out_shape=[
            jax.ShapeDtypeStruct((n, D_MODEL), F32),
            jax.ShapeDtypeStruct((n * ROW_TILES, LANES), F32),
            jax.ShapeDtypeStruct((SUBLANES, n), jnp.int32),
            jax.ShapeDtypeStruct((SUBLANES, n), F32),
            jax.ShapeDtypeStruct((SUBLANES, n), jnp.int32),
            jax.ShapeDtypeStruct((N_EXPERTS, LANES), F32),
        ],
        scratch_shapes=[pltpu.VMEM((N_EXPERTS, 1), F32)],
        compiler_params=_cparams(1),
        name="merge",
    )(x2d, o2d, y_tm2, gates, wab, wglu, bglu, wout, gffn, wr2_t, br_col)


def _lane_tile(ref2d, c, n_rows):
    return ref2d.at[pl.ds(c, n_rows, stride=ROW_TILES), :]


def _load_row_tiled(ref2d, n_rows):
    return jnp.concatenate([_lane_tile(ref2d, c, n_rows)[...] for c in range(ROW_TILES)], axis=1)


def _dispatch_kernel(slot_ref, xn_ref, xs_ref, sem):
    def issue(g, _):
        for k in range(TOP_K):
            for j in range(SUBLANES):
                t = g * SUBLANES + j
                pltpu.make_async_copy(xn_ref.at[t], xs_ref.at[slot_ref[k * DISPATCH_TILE + t]],
                                      sem).start(priority=j % 2)
        return 0

    lax.fori_loop(0, DISPATCH_TILE // SUBLANES, issue, 0)
    for _ in range(TOP_K):
        pltpu.make_async_copy(xn_ref, xs_ref.at[pl.ds(0, DISPATCH_TILE)], sem).wait()


def _dispatch(slots_flat, xn3):
    n = xn3.shape[0]
    return pl.pallas_call(
        _dispatch_kernel,
        grid=(n // DISPATCH_TILE,),
        in_specs=[
            pl.BlockSpec((TOP_K * DISPATCH_TILE,), lambda i: (i,), memory_space=pltpu.SMEM),
            pl.BlockSpec((DISPATCH_TILE, ROW_TILES, LANES), lambda i: (i, 0, 0)),
        ],
        out_specs=pl.BlockSpec(memory_space=pl.ANY),
        out_shape=jax.ShapeDtypeStruct((n * TOP_K, ROW_TILES, LANES), F32),
        scratch_shapes=[pltpu.SemaphoreType.DMA],
        compiler_params=_cparams(1),
        name="dispatch",
    )(slots_flat, xn3)


FLAG_VALID, FLAG_FIRST_VISIT, FLAG_NEW_EXPERT = 1, 2, 4


def _experts_kernel(blk_ref, exp_ref, nexp_ref, flag_ref, start_ref,
                    xs_ref, wup_hbm_ref, bup_ref, wdn_hbm_ref, bdn_ref, ys_ref,
                    wup_bf_ref, wdn_bf_ref, wup_f32_ref, wdn_f32_ref, keep_ref, wsem):
    i = pl.program_id(0)
    flags = flag_ref[i]

    def weight_copies(e):
        return (pltpu.make_async_copy(wup_hbm_ref.at[e], wup_f32_ref, wsem.at[0]),
                pltpu.make_async_copy(wdn_hbm_ref.at[e], wdn_f32_ref, wsem.at[1]))

    @pl.when(i == 0)
    def _():
        for cp in weight_copies(exp_ref[0]):
            cp.start(priority=1)

    @pl.when((flags & FLAG_NEW_EXPERT) != 0)
    def _():
        for cp in weight_copies(exp_ref[i]):
            cp.wait()
        wup_bf_ref[...] = wup_f32_ref[...].astype(BF16)
        wdn_bf_ref[...] = wdn_f32_ref[...].astype(BF16)
        nxt = nexp_ref[i]

        @pl.when(nxt >= 0)
        def _():
            for cp in weight_copies(nxt):
                cp.start(priority=1)

    revisit = ((flags & FLAG_VALID) != 0) & ((flags & FLAG_FIRST_VISIT) == 0)

    @pl.when(revisit)
    def _():
        keep_ref[...] = _load_row_tiled(ys_ref, ROW_BLK)

    @pl.when((flags & FLAG_VALID) != 0)
    def _():
        x = _load_row_tiled(xs_ref, ROW_BLK).astype(BF16)
        acts = []
        for c in range(D_FF // FF_CHUNK):
            gcols = pl.ds(c * FF_CHUNK, FF_CHUNK)
            ucols = pl.ds(D_FF + c * FF_CHUNK, FF_CHUNK)
            g = jnp.dot(x, wup_bf_ref[:, gcols], preferred_element_type=F32) + bup_ref[0, :, gcols]
            up = jnp.dot(x, wup_bf_ref[:, ucols], preferred_element_type=F32) + bup_ref[0, :, ucols]
            g = jnp.minimum(g, SWIGLU_LIMIT)
            up = jnp.clip(up, -SWIGLU_LIMIT, SWIGLU_LIMIT)
            acts.append(((up + 1.0) * g * jax.nn.sigmoid(SWIGLU_ALPHA * g)).astype(BF16))
        y = jnp.dot(jnp.concatenate(acts, axis=1), wdn_bf_ref[...],
                    preferred_element_type=F32) + bdn_ref[0]
        for c in range(ROW_TILES):
            _lane_tile(ys_ref, c, ROW_BLK)[...] = y[:, c * LANES:(c + 1) * LANES]

    @pl.when(revisit)
    def _():
        row = blk_ref[i] * ROW_BLK + lax.broadcasted_iota(jnp.int32, (ROW_BLK, 1), 0)
        earlier = jnp.broadcast_to(row < start_ref[exp_ref[i]], (ROW_BLK, LANES))
        for c in range(ROW_TILES):
            tile = _lane_tile(ys_ref, c, ROW_BLK)
            tile[...] = jnp.where(earlier, keep_ref[:, c * LANES:(c + 1) * LANES], tile[...])


def _experts(item_blk, item_exp, item_nexp, item_flag, starts, xs2d, w_up, b_up3, w_down, b_down3):
    n_items = item_blk.shape[0]
    rows_blk = pl.BlockSpec((ROW_BLK * ROW_TILES, LANES), lambda i, b, e, x, f, s: (b[i], 0))
    grid_spec = pltpu.PrefetchScalarGridSpec(
        num_scalar_prefetch=5,
        grid=(n_items,),
        in_specs=[
            rows_blk,
            pl.BlockSpec(memory_space=pl.ANY),
            pl.BlockSpec((1, 1, 2 * D_FF), lambda i, b, e, x, f, s: (e[i], 0, 0)),
            pl.BlockSpec(memory_space=pl.ANY),
            pl.BlockSpec((1, 1, D_MODEL), lambda i, b, e, x, f, s: (e[i], 0, 0)),
        ],
        out_specs=rows_blk,
        scratch_shapes=[
            pltpu.VMEM((D_MODEL, 2 * D_FF), BF16),
            pltpu.VMEM((D_FF, D_MODEL), BF16),
            pltpu.VMEM((D_MODEL, 2 * D_FF), F32),
            pltpu.VMEM((D_FF, D_MODEL), F32),
            pltpu.VMEM((ROW_BLK, D_MODEL), F32),
            pltpu.SemaphoreType.DMA((2,)),
        ],
    )
    return pl.pallas_call(
        _experts_kernel,
        grid_spec=grid_spec,
        out_shape=jax.ShapeDtypeStruct(xs2d.shape, F32),
        compiler_params=_cparams(1),
        name="experts",
    )(item_blk, item_exp, item_nexp, item_flag, starts, xs2d, w_up, b_up3, w_down, b_down3)


def _build_items(counts, n_rows):
    ends = jnp.cumsum(counts)
    starts = ends - counts
    n_items = n_rows // ROW_BLK + N_ITEMS_EXTRA
    first_blk = starts // ROW_BLK
    per_exp = jnp.where(counts > 0, (ends - 1) // ROW_BLK - first_blk + 1, 0)
    item_end = jnp.cumsum(per_exp)
    n_live = item_end[-1]
    k = jnp.arange(n_items, dtype=jnp.int32)
    valid = k < n_live
    kk = jnp.minimum(k, n_live - 1)
    exp = jnp.sum((kk[:, None] >= item_end[None, :]).astype(jnp.int32), axis=1)
    mine = exp[:, None] == jnp.arange(N_EXPERTS, dtype=jnp.int32)[None, :]
    blk = kk + jnp.sum(jnp.where(mine, (first_blk - (item_end - per_exp))[None, :], 0), axis=1)
    prev_blk = jnp.concatenate([jnp.full((1,), -1, jnp.int32), blk[:-1]])
    prev_exp = jnp.concatenate([jnp.full((1,), -1, jnp.int32), exp[:-1]])
    new_exp = valid & (exp != prev_exp)
    flags = (jnp.where(valid, FLAG_VALID, 0)
             | jnp.where(valid & (blk != prev_blk), FLAG_FIRST_VISIT, 0)
             | jnp.where(new_exp, FLAG_NEW_EXPERT, 0)).astype(jnp.int32)
    first_at = jnp.where(new_exp, k, n_items)
    next_first = jnp.flip(lax.cummin(jnp.flip(first_at)))
    next_first = jnp.concatenate([next_first[1:], jnp.full((1,), n_items, jnp.int32)])
    nexp = jnp.where(next_first < n_items, exp[jnp.minimum(next_first, n_items - 1)], -1)
    starts33 = jnp.concatenate([starts, ends[-1:]]).astype(jnp.int32)
    return blk, exp, nexp.astype(jnp.int32), flags, starts33


def _combine_kernel(slot_cur_ref, slot_next_ref, ys3_ref, ys2d_ref, x1_ref, gate_ref, p_ref,
                    gpg_ref, wpg_ref, wpp_ref, gpp_ref, out_ref, rows_ref, sem):
    i = pl.program_id(0)
    last = pl.num_programs(0) - 1
    cur = i % 2
    nxt = 1 - cur

    def start(slot_ref, buf):
        for k in range(TOP_K):
            for t in range(MOE_TILE):
                pltpu.make_async_copy(ys3_ref.at[slot_ref[k * MOE_TILE + t]],
                                      rows_ref.at[buf, k, pl.ds(t * ROW_TILES, ROW_TILES)],
                                      sem.at[buf]).start(priority=t % 2)

    def wait(buf):
        for k in range(TOP_K):
            pltpu.make_async_copy(ys2d_ref.at[pl.ds(0, MOE_TILE * ROW_TILES)],
                                  rows_ref.at[buf, k], sem.at[buf]).wait()

    @pl.when(i == 0)
    def _():
        start(slot_cur_ref, 0)

    wait(cur)
    gate = jnp.concatenate([gate_ref[...], jnp.zeros((LANES - SUBLANES, MOE_TILE), F32)],
                           axis=0).T
    moe = []
    for c in range(ROW_TILES):
        acc = gate[:, 0:1] * _lane_tile(rows_ref.at[cur, 0], c, MOE_TILE)[...]
        for k in range(1, TOP_K):
            acc = acc + gate[:, k:k + 1] * _lane_tile(rows_ref.at[cur, k], c, MOE_TILE)[...]
        moe.append(acc)
    x2 = x1_ref[...] + jnp.concatenate(moe, axis=1)
    start(slot_next_ref, nxt)
    ple = _rms(jnp.dot(p_ref[...].astype(BF16), wpp_ref[...], preferred_element_type=F32),
               gpp_ref[...])
    pg = jax.nn.sigmoid(jnp.dot(_rms(x2, gpg_ref[...]).astype(BF16), wpg_ref[...],
                                preferred_element_type=F32))
    out_ref[...] = x2 + pg * ple

    @pl.when(i == last)
    def _():
        wait(nxt)


def _combine(slots_flat, ys2d, x1, gate, p2d, gpg, wpg, wpp, gpp):
    n = x1.shape[0]
    n_tiles = n // MOE_TILE
    full = lambda shape: pl.BlockSpec(shape, lambda i: (0,) * len(shape))
    row = lambda w: pl.BlockSpec((MOE_TILE, w), lambda i: (i, 0))
    return pl.pallas_call(
        _combine_kernel,
        grid=(n_tiles,),
        in_specs=[
            pl.BlockSpec((TOP_K * MOE_TILE,), lambda i: (i,), memory_space=pltpu.SMEM),
            pl.BlockSpec((TOP_K * MOE_TILE,), lambda i: (jnp.minimum(i + 1, n_tiles - 1),),
                         memory_space=pltpu.SMEM),
            pl.BlockSpec(memory_space=pl.ANY), pl.BlockSpec(memory_space=pl.ANY),
            row(D_MODEL), pl.BlockSpec((SUBLANES, MOE_TILE), lambda i: (0, i)), row(PLE_DIM),
            full((1, D_MODEL)), full((D_MODEL, D_MODEL)), full((PLE_DIM, D_MODEL)),
            full((1, D_MODEL)),
        ],
        out_specs=row(D_MODEL),
        out_shape=jax.ShapeDtypeStruct((n, D_MODEL), F32),
        scratch_shapes=[pltpu.VMEM((2, TOP_K, MOE_TILE * ROW_TILES, LANES), F32),
                        pltpu.SemaphoreType.DMA((2,))],
        compiler_params=_cparams(1),
        name="combine",
    )(slots_flat, slots_flat, ys2d.reshape(-1, ROW_TILES, LANES), ys2d, x1, gate, p2d,
      gpg, wpg, wpp, gpp)


def kernel(x, p, g_mix, w_in, g_q, g_k, w_attn_branch, a_re, a_im, log_dt, b_re, b_im, c_re, c_im, d_skip, w_glu, b_glu, w_out, g_ffn, w_router, b_router, w_up, b_up, w_down, b_down, g_ple_gate, w_ple_gate, w_ple_proj, g_ple_post):
    bsz, s_len, d = x.shape
    depth = w_in.shape[0]
    n = bsz * s_len
    assert d == D_MODEL and s_len % SCAN_CHUNK == 0
    assert all(s_len % t == 0
               for t in (MERGE_SUBTILES * TOK_TILE, IN_TILE, MOE_TILE, DISPATCH_TILE))
    assert bsz == SUBLANES, "the S5 scan keeps the batch on the sublane axis"

    x2d = x.reshape(n, d)
    for i in range(depth):
        qs, kn, vb, u_tm, gates = _inproj(x2d, g_mix[i][None], w_in[i].astype(BF16),
                                          jnp.tile(g_q[i], 2)[None], jnp.tile(g_k[i], 2)[None],
                                          bsz, s_len)
        o = _attn(qs, kn.reshape(bsz, s_len, SB_WIDTH), vb.reshape(bsz, s_len, SB_WIDTH))
        a_bc, b_cat, c_cat = _s5_params(a_re[i], a_im[i], log_dt[i], b_re[i], b_im[i],
                                        c_re[i], c_im[i], bsz)
        y_tm = _s5(u_tm.reshape(s_len, bsz, SSM_WIDTH), a_bc, b_cat, c_cat, d_skip[i][None])

        wr_t = w_router[i].T
        wr_hi = wr_t.astype(BF16)
        wr_lo = (wr_t - wr_hi.astype(F32)).astype(BF16)
        x1, xn, idx, gate, rank, counts = _merge(
            x2d, o.reshape(n, SB_WIDTH), y_tm.reshape(s_len, bsz * SSM_WIDTH), gates,
            w_attn_branch[i].astype(BF16), w_glu[i].astype(BF16), b_glu[i][None],
            w_out[i].astype(BF16), g_ffn[i][None], jnp.concatenate([wr_hi, wr_lo], axis=0),
            b_router[i][:, None], s_len)

        counts_i = counts[:, 0].astype(jnp.int32)
        blk, exp, nexp, flags, starts33 = _build_items(counts_i, n * TOP_K)
        hot = idx[:TOP_K, :, None] == jnp.arange(N_EXPERTS, dtype=jnp.int32)
        slots = jnp.sum(jnp.where(hot, starts33[:N_EXPERTS], 0), axis=-1) + rank[:TOP_K]
        def per_tile(tile):
            return slots.reshape(TOP_K, n // tile, tile).transpose(1, 0, 2).reshape(-1)

        slots_flat = per_tile(MOE_TILE)

        xs3 = _dispatch(per_tile(DISPATCH_TILE), xn.reshape(n, ROW_TILES, LANES))
        ys2d = _experts(blk, exp, nexp, flags, starts33, xs3.reshape(n * TOP_K * ROW_TILES, LANES),
                        w_up[i], b_up[i][:, None, :], w_down[i], b_down[i][:, None, :])
        x2d = _combine(slots_flat, ys2d, x1, gate, p[i].reshape(n, PLE_DIM),
                       g_ple_gate[i][None], w_ple_gate[i].astype(BF16),
                       w_ple_proj[i].astype(BF16), g_ple_post[i][None])
    return x2d.reshape(bsz, s_len, d)
```

```python
import jax
import jax.numpy as jnp
from jax import lax
from jax.experimental import pallas as pl
from jax.experimental.pallas import tpu as pltpu

F32 = jnp.float32
BF16 = jnp.bfloat16

D_MODEL = 1024
SB_HEADS = 8
SB_HEAD_DIM = 64
SB_WIDTH = SB_HEADS * SB_HEAD_DIM
SSM_GROUP = 16
SSM_WIDTH = 512
SSM_GROUPS = SSM_WIDTH // SSM_GROUP
SSM_STATE = 64
PLE_DIM = 256
N_EXPERTS = 32
TOP_K = 4
D_FF = D_MODEL
SWIGLU_LIMIT = 7.0
SWIGLU_ALPHA = 1.702
EPS = 1e-6

LANES = 128
SUBLANES = 8
VMEM_LIMIT = 56 * 1024 * 1024

TOK_TILE = 256
IN_TILE = 512
MOE_TILE = 512
DISPATCH_TILE = 2048
ROW_TILES = D_MODEL // LANES
assert ROW_TILES == SUBLANES
MERGE_SUBTILES = 2
ATT_BLK = 128
DEAD_LOG_WEIGHT = -104.0
HEAD_PAIRS = SB_WIDTH // LANES
SCAN_CHUNK = 64
SCAN_LANES = 512
HALF_U = SSM_WIDTH // 2
HALF_STATE = SSM_GROUPS // 2 * SSM_STATE
ROW_BLK = 256
FF_CHUNK = 256
N_ITEMS_EXTRA = N_EXPERTS - 1


def _cparams(n_axes, flags=None):
    return pltpu.CompilerParams(
        dimension_semantics=("arbitrary",) * n_axes,
        vmem_limit_bytes=VMEM_LIMIT,
        flags=flags)


def _rms(x, g):
    ms = jnp.mean(x * x, axis=-1, keepdims=True)
    return x * lax.rsqrt(ms + EPS) * g


def _inproj_kernel(x_ref, g_ref, w_ref, gq_ref, gk_ref, qs_ref, k_ref, v_ref, u_ref, gates_ref):
    h = _rms(x_ref[...], g_ref[...]).astype(BF16)
    n_qkv = 3 * SB_WIDTH
    n_u = n_qkv + SSM_WIDTH
    qkv = jnp.dot(h, w_ref[:, :n_qkv], preferred_element_type=F32)
    u_ref[...] = jnp.dot(h, w_ref[:, n_qkv:n_u], preferred_element_type=F32)
    gates_ref[...] = jnp.dot(h, w_ref[:, n_u:], preferred_element_type=F32).astype(BF16)

    head0 = lax.broadcasted_iota(jnp.int32, (1, LANES), 1) < SB_HEAD_DIM

    def head_rms(t, g):
        sq = t * t
        s0 = jnp.sum(jnp.where(head0, sq, 0.0), axis=-1, keepdims=True)
        s1 = jnp.sum(jnp.where(head0, 0.0, sq), axis=-1, keepdims=True)
        ms = jnp.where(head0, s0, s1) * (1.0 / SB_HEAD_DIM)
        return t * lax.rsqrt(ms + EPS) * g

    scale = SB_HEAD_DIM ** -0.5
    for p in range(HEAD_PAIRS):
        cols = slice(p * LANES, (p + 1) * LANES)
        qn = head_rms(qkv[:, cols], gq_ref[...]) * scale
        q0 = jnp.where(head0, qn, 0.0).astype(BF16)
        q1 = jnp.where(head0, 0.0, qn).astype(BF16)
        for t in range(IN_TILE // ATT_BLK):
            rows = slice(t * ATT_BLK, (t + 1) * ATT_BLK)
            qs_ref[0, p, t, :ATT_BLK, :] = q0[rows]
            qs_ref[0, p, t, ATT_BLK:, :] = q1[rows]
        kcols = slice(SB_WIDTH + p * LANES, SB_WIDTH + (p + 1) * LANES)
        k_ref[:, cols] = head_rms(qkv[:, kcols], gk_ref[...]).astype(BF16)
    v_ref[...] = qkv[:, 2 * SB_WIDTH:].astype(BF16)


def _inproj(x2d, g_mix, w_in_bf, gq2, gk2, bsz, s_len):
    n = x2d.shape[0]
    tiles_per_seq = s_len // IN_TILE
    blks_per_tile = IN_TILE // ATT_BLK
    in_cols = w_in_bf.shape[1]
    return pl.pallas_call(
        _inproj_kernel,
        grid=(n // IN_TILE,),
        in_specs=[
            pl.BlockSpec((IN_TILE, D_MODEL), lambda i: (i, 0)),
            pl.BlockSpec((1, D_MODEL), lambda i: (0, 0)),
            pl.BlockSpec((D_MODEL, in_cols), lambda i: (0, 0)),
            pl.BlockSpec((1, LANES), lambda i: (0, 0)),
            pl.BlockSpec((1, LANES), lambda i: (0, 0)),
        ],
        out_specs=[
            pl.BlockSpec((1, HEAD_PAIRS, blks_per_tile, 2 * ATT_BLK, LANES),
                         lambda i: (i // tiles_per_seq, 0, i % tiles_per_seq, 0, 0)),
            pl.BlockSpec((IN_TILE, SB_WIDTH), lambda i: (i, 0)),
            pl.BlockSpec((IN_TILE, SB_WIDTH), lambda i: (i, 0)),
            pl.BlockSpec((IN_TILE, SSM_WIDTH),
                         lambda i: (i % tiles_per_seq, i // tiles_per_seq)),
            pl.BlockSpec((IN_TILE, 2 * D_MODEL), lambda i: (i, 0)),
        ],
        out_shape=[
            jax.ShapeDtypeStruct((bsz, HEAD_PAIRS, s_len // ATT_BLK, 2 * ATT_BLK, LANES), BF16),
            jax.ShapeDtypeStruct((n, SB_WIDTH), BF16),
            jax.ShapeDtypeStruct((n, SB_WIDTH), BF16),
            jax.ShapeDtypeStruct((s_len, bsz * SSM_WIDTH), F32),
            jax.ShapeDtypeStruct((n, 2 * D_MODEL), BF16),
        ],
        compiler_params=_cparams(1),
        name="inproj",
    )(x2d, g_mix, w_in_bf, gq2, gk2)


def _attn_kernel(qs_ref, ks_ref, vs_ref, o_ref, w2_ref, c_ref, acc_ref):
    s_len = ks_ref.shape[1]
    n_blk = s_len // ATT_BLK
    n_pairs = HEAD_PAIRS
    head0 = lax.broadcasted_iota(jnp.int32, (1, LANES), 1) < SB_HEAD_DIM

    r = lax.broadcasted_iota(jnp.int32, (2 * ATT_BLK, 2 * ATT_BLK), 0)
    c = lax.broadcasted_iota(jnp.int32, (2 * ATT_BLK, 2 * ATT_BLK), 1)
    r = jnp.where(r >= ATT_BLK, r - ATT_BLK, r)
    w2_ref[...] = jnp.where((c >= ATT_BLK) | (r > c), 1.0, 0.0).astype(BF16)

    ti = lax.broadcasted_iota(jnp.int32, (2 * ATT_BLK, ATT_BLK), 0)
    si = lax.broadcasted_iota(jnp.int32, (2 * ATT_BLK, ATT_BLK), 1)
    causal = si < jnp.where(ti >= ATT_BLK, ti - ATT_BLK, ti)

    def rows_of(blk):
        if isinstance(blk, int):
            return pl.ds(blk * ATT_BLK, ATT_BLK)
        return pl.ds(pl.multiple_of(blk * ATT_BLK, ATT_BLK), ATT_BLK)

    def tiles(sweeps, diag):
        chains = [(w, qi, rows_of(kv), p) for (w, qi, kv) in sweeps for p in range(n_pairs)]
        z = [lax.dot_general(qs_ref[0, p, qi], ks_ref[0, kv_rows, p * LANES:(p + 1) * LANES],
                             (((1,), (1,)), ((), ())),
                             preferred_element_type=F32)
             for (w, qi, kv_rows, p) in chains]
        log_beta, stacked = [], []
        for zi in z:
            sp = jnp.maximum(zi, 0.0) + jnp.log(1.0 + jnp.exp(-jnp.abs(zi)))
            log_keep = -sp
            log_beta.append(zi - sp)
            if diag:
                log_keep = jnp.where(causal, log_keep, 0.0)
            hi = log_keep.astype(BF16)
            lo = (log_keep - hi.astype(F32)).astype(BF16)
            stacked.append(jnp.concatenate([hi, lo], axis=1))
        sums = [jnp.dot(st, w2_ref[...], preferred_element_type=F32) for st in stacked]
        weights = []
        for i, (w, qi, kv_rows, p) in enumerate(chains):
            wp = jnp.exp(log_beta[i] + sums[i][:, :ATT_BLK] + c_ref[w, p])
            if diag:
                wp = jnp.where(causal, wp, 0.0)
            weights.append(wp.astype(BF16))
            c_ref[w, p] += sums[i][:, ATT_BLK:]
        pv = [jnp.dot(weights[i], vs_ref[0, kv_rows, p * LANES:(p + 1) * LANES],
                      preferred_element_type=F32)
              for i, (w, qi, kv_rows, p) in enumerate(chains)]
        for i, (w, qi, kv_rows, p) in enumerate(chains):
            acc_ref[w, p] += pv[i]

    def qblocks(qb, _):
        qa = 2 * qb
        qc = qa + 1
        c_ref[...] = jnp.zeros_like(c_ref)
        acc_ref[...] = jnp.zeros_like(acc_ref)
        tiles([(0, qa, qa), (1, qc, qc)], True)

        def live(carry):
            jj, c_max = carry
            return (jj < qa) & (c_max > DEAD_LOG_WEIGHT)

        def kvblock(carry):
            jj, _ = carry
            tiles([(0, qa, qa - 1 - jj), (1, qc, qa - jj)], False)
            return jj + 1, jnp.max(c_ref[...])

        jj, _ = lax.while_loop(live, kvblock, (jnp.int32(0), jnp.max(c_ref[...])))

        @pl.when((jj == qa) & (jnp.max(c_ref[1]) > DEAD_LOG_WEIGHT))
        def _():
            tiles([(1, qc, 0)], False)

        for w, qi in ((0, qa), (1, qc)):
            for p in range(n_pairs):
                o_ref[0, rows_of(qi), p * LANES:(p + 1) * LANES] = jnp.where(
                    head0, acc_ref[w, p, :ATT_BLK, :], acc_ref[w, p, ATT_BLK:, :])
        return 0

    lax.fori_loop(0, n_blk // 2, qblocks, 0)


def _attn(qs, k3, v3):
    bsz, s_len, _ = k3.shape
    kv_blk = pl.BlockSpec((1, s_len, SB_WIDTH), lambda b: (b, 0, 0))
    return pl.pallas_call(
        _attn_kernel,
        grid=(bsz,),
        in_specs=[
            pl.BlockSpec((1,) + qs.shape[1:], lambda b: (b, 0, 0, 0, 0)),
            kv_blk, kv_blk,
        ],
        out_specs=pl.BlockSpec((1, s_len, SB_WIDTH), lambda b: (b, 0, 0)),
        out_shape=jax.ShapeDtypeStruct((bsz, s_len, SB_WIDTH), F32),
        scratch_shapes=[
            pltpu.VMEM((2 * ATT_BLK, 2 * ATT_BLK), BF16),
            pltpu.VMEM((2, HEAD_PAIRS, 2 * ATT_BLK, ATT_BLK), F32),
            pltpu.VMEM((2, HEAD_PAIRS, 2 * ATT_BLK, LANES), F32),
        ],
        compiler_params=_cparams(1),
        name="attn",
    )(qs, k3, v3)


def _s5_kernel(u_ref, a_ref, b_ref, c_ref, d_ref, y_ref, hbuf_ref, state_ref):
    tc, bsz, _ = u_ref.shape
    rows = tc * bsz

    @pl.when(pl.program_id(0) == 0)
    def _():
        state_ref[...] = jnp.zeros_like(state_ref)

    u2 = u_ref[...].reshape(rows, SSM_WIDTH)
    ub = u2.astype(BF16)
    halves = range(2)
    for hf in halves:
        uh = ub[:, hf * HALF_U:(hf + 1) * HALF_U]
        xh = jnp.dot(uh, b_ref[hf], preferred_element_type=F32)
        hbuf_ref[hf] = xh.reshape(tc, bsz, 2 * HALF_STATE)
    for hf in halves:
        for lc in range(HALF_STATE // SCAN_LANES):
            re = pl.ds(lc * SCAN_LANES, SCAN_LANES)
            im = pl.ds(HALF_STATE + lc * SCAN_LANES, SCAN_LANES)
            ar = a_ref[hf, :, re]
            ai = a_ref[hf, :, im]
            hr = state_ref[hf, :, re]
            hi = state_ref[hf, :, im]
            for t in range(tc):
                hr, hi = (ar * hr - ai * hi + hbuf_ref[hf, t, :, re],
                          ar * hi + ai * hr + hbuf_ref[hf, t, :, im])
                hbuf_ref[hf, t, :, re] = hr
                hbuf_ref[hf, t, :, im] = hi
            state_ref[hf, :, re] = hr
            state_ref[hf, :, im] = hi
        hb = hbuf_ref[hf].reshape(rows, 2 * HALF_STATE).astype(BF16)
        yh = jnp.dot(hb, c_ref[hf], preferred_element_type=F32)
        cols = slice(hf * HALF_U, (hf + 1) * HALF_U)
        yh = yh + d_ref[:, cols] * u2[:, cols]
        y_ref[:, :, cols] = yh.reshape(tc, bsz, HALF_U)


def _s5(u_tm3, a_bc, b_cat, c_cat, d_skip):
    s_len, bsz, _ = u_tm3.shape
    return pl.pallas_call(
        _s5_kernel,
        grid=(s_len // SCAN_CHUNK,),
        in_specs=[
            pl.BlockSpec((SCAN_CHUNK, bsz, SSM_WIDTH), lambda c: (c, 0, 0)),
            pl.BlockSpec((2, bsz, 2 * HALF_STATE), lambda c: (0, 0, 0)),
            pl.BlockSpec((2, HALF_U, 2 * HALF_STATE), lambda c: (0, 0, 0)),
            pl.BlockSpec((2, 2 * HALF_STATE, HALF_U), lambda c: (0, 0, 0)),
            pl.BlockSpec((1, SSM_WIDTH), lambda c: (0, 0)),
        ],
        out_specs=pl.BlockSpec((SCAN_CHUNK, bsz, SSM_WIDTH), lambda c: (c, 0, 0)),
        out_shape=jax.ShapeDtypeStruct((s_len, bsz, SSM_WIDTH), F32),
        scratch_shapes=[
            pltpu.VMEM((2, SCAN_CHUNK, bsz, 2 * HALF_STATE), F32),
            pltpu.VMEM((2, bsz, 2 * HALF_STATE), F32),
        ],
        compiler_params=_cparams(1),
        name="s5",
    )(u_tm3, a_bc, b_cat, c_cat, d_skip)


def _s5_params(a_re, a_im, log_dt, b_re, b_im, c_re, c_im, bsz):
    dt = jnp.exp(log_dt)[:, None]
    mag = jnp.exp(a_re * dt)
    abar_r = mag * jnp.cos(a_im * dt)
    abar_i = mag * jnp.sin(a_im * dt)
    den = a_re * a_re + a_im * a_im
    nr = abar_r - 1.0
    ni = abar_i
    fr = (nr * a_re + ni * a_im) / den
    fi = (ni * a_re - nr * a_im) / den
    bbar_r = fr[..., None] * b_re - fi[..., None] * b_im
    bbar_i = fr[..., None] * b_im + fi[..., None] * b_re
    gh = SSM_GROUPS // 2
    eye = jnp.eye(gh, dtype=F32)

    def a_half(hf):
        sl = slice(hf * gh, (hf + 1) * gh)
        row = jnp.concatenate([abar_r[sl].reshape(-1), abar_i[sl].reshape(-1)])
        return jnp.broadcast_to(row[None], (bsz, 2 * HALF_STATE))

    def b_half(bb, hf):
        blk = bb[hf * gh:(hf + 1) * gh]
        return jnp.einsum('gpc,gk->gckp', blk, eye).reshape(HALF_U, HALF_STATE)

    def c_half(cc, hf):
        blk = cc[hf * gh:(hf + 1) * gh]
        return jnp.einsum('gcp,gk->gpkc', blk, eye).reshape(HALF_STATE, HALF_U)

    a_bc = jnp.stack([a_half(0), a_half(1)])
    b_cat = jnp.stack([jnp.concatenate([b_half(bbar_r, hf), b_half(bbar_i, hf)], axis=1)
                       for hf in range(2)]).astype(BF16)
    c_cat = jnp.stack([jnp.concatenate([c_half(c_re, hf), -c_half(c_im, hf)], axis=0)
                       for hf in range(2)]).astype(BF16)
    return a_bc, b_cat, c_cat


def _merge_kernel(x_ref, o_ref, y_ref, gates_ref, wab_ref, wglu_ref, bglu_ref, wout_ref,
                  gffn_ref, wr2_ref, br_ref,
                  x1_ref, xn_ref, idx_ref, gate_ref, rank_ref, cnt_ref, carry_ref):
    @pl.when(pl.program_id(0) == 0)
    def _():
        carry_ref[...] = jnp.zeros_like(carry_ref)

    subs = [pl.ds(s * TOK_TILE, TOK_TILE) for s in range(MERGE_SUBTILES)]
    nt = (((1,), (1,)), ((), ()))

    attn_branch = [jnp.dot(o_ref[s, :].astype(BF16), wab_ref[...], preferred_element_type=F32)
                   for s in subs]
    zg = [jnp.dot(jax.nn.gelu(y_ref[s, :]).astype(BF16), wglu_ref[...],
                  preferred_element_type=F32) + bglu_ref[...] for s in subs]
    mixed = []
    for i, s in enumerate(subs):
        ssm_branch = zg[i][:, :D_MODEL] * jax.nn.sigmoid(zg[i][:, D_MODEL:])
        mixed.append((jax.nn.sigmoid(gates_ref[s, :D_MODEL].astype(F32)) * attn_branch[i]
                      + jax.nn.sigmoid(gates_ref[s, D_MODEL:].astype(F32)) * ssm_branch
                      ).astype(BF16))
    x1 = [x_ref[s, :] + jnp.dot(mixed[i], wout_ref[...], preferred_element_type=F32)
          for i, s in enumerate(subs)]
    xh, xl = [], []
    for i, s in enumerate(subs):
        x1_ref[s, :] = x1[i]
        xn = _rms(x1[i], gffn_ref[...])
        for c in range(ROW_TILES):
            xn_ref[pl.ds(i * TOK_TILE * ROW_TILES + c, TOK_TILE, stride=ROW_TILES), :] = (
                xn[:, c * LANES:(c + 1) * LANES])
        xh.append(xn.astype(BF16))
        xl.append((xn - xh[i].astype(F32)).astype(BF16))
    logits = []
    for i in range(MERGE_SUBTILES):
        both = lax.dot_general(wr2_ref[...], xh[i], nt, preferred_element_type=F32)
        cross = lax.dot_general(wr2_ref[:N_EXPERTS, :], xl[i], nt, preferred_element_type=F32)
        logits.append((both[:N_EXPERTS] + both[N_EXPERTS:] + cross) + br_ref[...])

    e_iota = lax.broadcasted_iota(jnp.int32, (N_EXPERTS, TOK_TILE), 0).astype(F32)
    k_iota = lax.broadcasted_iota(jnp.int32, (SUBLANES, TOK_TILE), 0)
    rr = lax.broadcasted_iota(jnp.int32, (TOK_TILE, TOK_TILE), 0)
    cc = lax.broadcasted_iota(jnp.int32, (TOK_TILE, TOK_TILE), 1)
    earlier = jnp.where(rr < cc, 1.0, 0.0).astype(BF16)

    for i, s in enumerate(subs):
        work = logits[i]
        hits, vals = [], []
        sel = jnp.zeros((N_EXPERTS, TOK_TILE), F32)
        for _ in range(TOP_K):
            m = jnp.max(work, axis=0, keepdims=True)
            pick = jnp.min(jnp.where(work == m, e_iota, float(N_EXPERTS)), axis=0, keepdims=True)
            hit = e_iota == pick
            work = jnp.where(hit, -jnp.inf, work)
            sel = sel + jnp.where(hit, 1.0, 0.0)
            hits.append((hit, pick))
            vals.append(m)
        before = jnp.dot(sel.astype(BF16), earlier, preferred_element_type=F32) + carry_ref[...]
        exps = [jnp.exp(v - vals[0]) for v in vals]
        denom = exps[0] + exps[1] + exps[2] + exps[3]
        idx = jnp.zeros((SUBLANES, TOK_TILE), jnp.int32)
        gate = jnp.zeros((SUBLANES, TOK_TILE), F32)
        rank = jnp.zeros((SUBLANES, TOK_TILE), jnp.int32)
        for k in range(TOP_K):
            hit, pick = hits[k]
            rk = jnp.sum(jnp.where(hit, before, 0.0), axis=0, keepdims=True)
            idx = jnp.where(k_iota == k, pick.astype(jnp.int32), idx)
            gate = jnp.where(k_iota == k, exps[k] / denom, gate)
            rank = jnp.where(k_iota == k, rk.astype(jnp.int32), rank)
        idx_ref[:, s] = idx
        gate_ref[:, s] = gate
        rank_ref[:, s] = rank
        carry_ref[...] += jnp.sum(sel, axis=1, keepdims=True)
    cnt_ref[...] = jnp.broadcast_to(carry_ref[...], cnt_ref.shape)


def _merge(x2d, o2d, y_tm2, gates, wab, wglu, bglu, wout, gffn, wr2_t, br_col, s_len):
    n = x2d.shape[0]
    tile = MERGE_SUBTILES * TOK_TILE
    tiles_per_seq = s_len // tile
    full = lambda shape: pl.BlockSpec(shape, lambda i: (0,) * len(shape))
    row = lambda w: pl.BlockSpec((tile, w), lambda i: (i, 0))
    col = pl.BlockSpec((SUBLANES, tile), lambda i: (0, i))
    return pl.pallas_call(
        _merge_kernel,
        grid=(n // tile,),
        in_specs=[
            row(D_MODEL), row(SB_WIDTH),
            pl.BlockSpec((tile, SSM_WIDTH),
                         lambda i: (i % tiles_per_seq, i // tiles_per_seq)),
            row(2 * D_MODEL),
            full((SB_WIDTH, D_MODEL)), full((SSM_WIDTH, 2 * D_MODEL)), full((1, 2 * D_MODEL)),
            full((D_MODEL, D_MODEL)), full((1, D_MODEL)),
            full((2 * N_EXPERTS, D_MODEL)), full((N_EXPERTS, 1)),
        ],
        out_specs=[row(D_MODEL), pl.BlockSpec((tile * ROW_TILES, LANES), lambda i: (i, 0)),
                   col, col, col, full((N_EXPERTS, LANES))],
        out_shape=[
            jax.ShapeDtypeStruct((n, D_MODEL), F32),
            jax.ShapeDtypeStruct((n * ROW_TILES, LANES), F32),
            jax.ShapeDtypeStruct((SUBLANES, n), jnp.int32),
            jax.ShapeDtypeStruct((SUBLANES, n), F32),
            jax.ShapeDtypeStruct((SUBLANES, n), jnp.int32),
            jax.ShapeDtypeStruct((N_EXPERTS, LANES), F32),
        ],
        scratch_shapes=[pltpu.VMEM((N_EXPERTS, 1), F32)],
        compiler_params=_cparams(1),
        name="merge",
    )(x2d, o2d, y_tm2, gates, wab, wglu, bglu, wout, gffn, wr2_t, br_col)


def _lane_tile(ref2d, c, n_rows):
    return ref2d.at[pl.ds(c, n_rows, stride=ROW_TILES), :]


def _load_row_tiled(ref2d, n_rows):
    return jnp.concatenate([_lane_tile(ref2d, c, n_rows)[...] for c in range(ROW_TILES)], axis=1)


def _dispatch_kernel(slot_ref, xn_ref, xs_ref, sem):
    def issue(g, _):
        for k in range(TOP_K):
            for j in range(SUBLANES):
                t = g * SUBLANES + j
                pltpu.make_async_copy(xn_ref.at[t], xs_ref.at[slot_ref[k * DISPATCH_TILE + t]],
                                      sem).start(priority=j % 2)
        return 0

    lax.fori_loop(0, DISPATCH_TILE // SUBLANES, issue, 0)
    for _ in range(TOP_K):
        pltpu.make_async_copy(xn_ref, xs_ref.at[pl.ds(0, DISPATCH_TILE)], sem).wait()


def _dispatch(slots_flat, xn3):
    n = xn3.shape[0]
    return pl.pallas_call(
        _dispatch_kernel,
        grid=(n // DISPATCH_TILE,),
        in_specs=[
            pl.BlockSpec((TOP_K * DISPATCH_TILE,), lambda i: (i,), memory_space=pltpu.SMEM),
            pl.BlockSpec((DISPATCH_TILE, ROW_TILES, LANES), lambda i: (i, 0, 0)),
        ],
        out_specs=pl.BlockSpec(memory_space=pl.ANY),
        out_shape=jax.ShapeDtypeStruct((n * TOP_K, ROW_TILES, LANES), F32),
        scratch_shapes=[pltpu.SemaphoreType.DMA],
        compiler_params=_cparams(1),
        name="dispatch",
    )(slots_flat, xn3)


FLAG_VALID, FLAG_FIRST_VISIT, FLAG_NEW_EXPERT = 1, 2, 4


def _experts_kernel(blk_ref, exp_ref, nexp_ref, flag_ref, start_ref,
                    xs_ref, wup_hbm_ref, bup_ref, wdn_hbm_ref, bdn_ref, ys_ref,
                    wup_bf_ref, wdn_bf_ref, wup_f32_ref, wdn_f32_ref, keep_ref, wsem):
    i = pl.program_id(0)
    flags = flag_ref[i]

    def weight_copies(e):
        return (pltpu.make_async_copy(wup_hbm_ref.at[e], wup_f32_ref, wsem.at[0]),
                pltpu.make_async_copy(wdn_hbm_ref.at[e], wdn_f32_ref, wsem.at[1]))

    @pl.when(i == 0)
    def _():
        for cp in weight_copies(exp_ref[0]):
            cp.start(priority=1)

    @pl.when((flags & FLAG_NEW_EXPERT) != 0)
    def _():
        for cp in weight_copies(exp_ref[i]):
            cp.wait()
        wup_bf_ref[...] = wup_f32_ref[...].astype(BF16)
        wdn_bf_ref[...] = wdn_f32_ref[...].astype(BF16)
        nxt = nexp_ref[i]

        @pl.when(nxt >= 0)
        def _():
            for cp in weight_copies(nxt):
                cp.start(priority=1)

    revisit = ((flags & FLAG_VALID) != 0) & ((flags & FLAG_FIRST_VISIT) == 0)

    @pl.when(revisit)
    def _():
        keep_ref[...] = _load_row_tiled(ys_ref, ROW_BLK)

    @pl.when((flags & FLAG_VALID) != 0)
    def _():
        x = _load_row_tiled(xs_ref, ROW_BLK).astype(BF16)
        acts = []
        for c in range(D_FF // FF_CHUNK):
            gcols = pl.ds(c * FF_CHUNK, FF_CHUNK)
            ucols = pl.ds(D_FF + c * FF_CHUNK, FF_CHUNK)
            g = jnp.dot(x, wup_bf_ref[:, gcols], preferred_element_type=F32) + bup_ref[0, :, gcols]
            up = jnp.dot(x, wup_bf_ref[:, ucols], preferred_element_type=F32) + bup_ref[0, :, ucols]
            g = jnp.minimum(g, SWIGLU_LIMIT)
            up = jnp.clip(up, -SWIGLU_LIMIT, SWIGLU_LIMIT)
            acts.append(((up + 1.0) * g * jax.nn.sigmoid(SWIGLU_ALPHA * g)).astype(BF16))
        y = jnp.dot(jnp.concatenate(acts, axis=1), wdn_bf_ref[...],
                    preferred_element_type=F32) + bdn_ref[0]
        for c in range(ROW_TILES):
            _lane_tile(ys_ref, c, ROW_BLK)[...] = y[:, c * LANES:(c + 1) * LANES]

    @pl.when(revisit)
    def _():
        row = blk_ref[i] * ROW_BLK + lax.broadcasted_iota(jnp.int32, (ROW_BLK, 1), 0)
        earlier = jnp.broadcast_to(row < start_ref[exp_ref[i]], (ROW_BLK, LANES))
        for c in range(ROW_TILES):
            tile = _lane_tile(ys_ref, c, ROW_BLK)
            tile[...] = jnp.where(earlier, keep_ref[:, c * LANES:(c + 1) * LANES], tile[...])


def _experts(item_blk, item_exp, item_nexp, item_flag, starts, xs2d, w_up, b_up3, w_down, b_down3):
    n_items = item_blk.shape[0]
    rows_blk = pl.BlockSpec((ROW_BLK * ROW_TILES, LANES), lambda i, b, e, x, f, s: (b[i], 0))
    grid_spec = pltpu.PrefetchScalarGridSpec(
        num_scalar_prefetch=5,
        grid=(n_items,),
        in_specs=[
            rows_blk,
            pl.BlockSpec(memory_space=pl.ANY),
            pl.BlockSpec((1, 1, 2 * D_FF), lambda i, b, e, x, f, s: (e[i], 0, 0)),
            pl.BlockSpec(memory_space=pl.ANY),
            pl.BlockSpec((1, 1, D_MODEL), lambda i, b, e, x, f, s: (e[i], 0, 0)),
        ],
        out_specs=rows_blk,
        scratch_shapes=[
            pltpu.VMEM((D_MODEL, 2 * D_FF), BF16),
            pltpu.VMEM((D_FF, D_MODEL), BF16),
            pltpu.VMEM((D_MODEL, 2 * D_FF), F32),
            pltpu.VMEM((D_FF, D_MODEL), F32),
            pltpu.VMEM((ROW_BLK, D_MODEL), F32),
            pltpu.SemaphoreType.DMA((2,)),
        ],
    )
    return pl.pallas_call(
        _experts_kernel,
        grid_spec=grid_spec,
        out_shape=jax.ShapeDtypeStruct(xs2d.shape, F32),
        compiler_params=_cparams(1),
        name="experts",
    )(item_blk, item_exp, item_nexp, item_flag, starts, xs2d, w_up, b_up3, w_down, b_down3)


def _build_items(counts, n_rows):
    ends = jnp.cumsum(counts)
    starts = ends - counts
    n_items = n_rows // ROW_BLK + N_ITEMS_EXTRA
    first_blk = starts // ROW_BLK
    per_exp = jnp.where(counts > 0, (ends - 1) // ROW_BLK - first_blk + 1, 0)
    item_end = jnp.cumsum(per_exp)
    n_live = item_end[-1]
    k = jnp.arange(n_items, dtype=jnp.int32)
    valid = k < n_live
    kk = jnp.minimum(k, n_live - 1)
    exp = jnp.sum((kk[:, None] >= item_end[None, :]).astype(jnp.int32), axis=1)
    mine = exp[:, None] == jnp.arange(N_EXPERTS, dtype=jnp.int32)[None, :]
    blk = kk + jnp.sum(jnp.where(mine, (first_blk - (item_end - per_exp))[None, :], 0), axis=1)
    prev_blk = jnp.concatenate([jnp.full((1,), -1, jnp.int32), blk[:-1]])
    prev_exp = jnp.concatenate([jnp.full((1,), -1, jnp.int32), exp[:-1]])
    new_exp = valid & (exp != prev_exp)
    flags = (jnp.where(valid, FLAG_VALID, 0)
             | jnp.where(valid & (blk != prev_blk), FLAG_FIRST_VISIT, 0)
             | jnp.where(new_exp, FLAG_NEW_EXPERT, 0)).astype(jnp.int32)
    first_at = jnp.where(new_exp, k, n_items)
    next_first = jnp.flip(lax.cummin(jnp.flip(first_at)))
    next_first = jnp.concatenate([next_first[1:], jnp.full((1,), n_items, jnp.int32)])
    nexp = jnp.where(next_first < n_items, exp[jnp.minimum(next_first, n_items - 1)], -1)
    starts33 = jnp.concatenate([starts, ends[-1:]]).astype(jnp.int32)
    return blk, exp, nexp.astype(jnp.int32), flags, starts33


def _combine_kernel(slot_cur_ref, slot_next_ref, ys3_ref, ys2d_ref, x1_ref, gate_ref, p_ref,
                    gpg_ref, wpg_ref, wpp_ref, gpp_ref, out_ref, rows_ref, sem):
    i = pl.program_id(0)
    last = pl.num_programs(0) - 1
    cur = i % 2
    nxt = 1 - cur

    def start(slot_ref, buf):
        for k in range(TOP_K):
            for t in range(MOE_TILE):
                pltpu.make_async_copy(ys3_ref.at[slot_ref[k * MOE_TILE + t]],
                                      rows_ref.at[buf, k, pl.ds(t * ROW_TILES, ROW_TILES)],
                                      sem.at[buf]).start(priority=t % 2)

    def wait(buf):
        for k in range(TOP_K):
            pltpu.make_async_copy(ys2d_ref.at[pl.ds(0, MOE_TILE * ROW_TILES)],
                                  rows_ref.at[buf, k], sem.at[buf]).wait()

    @pl.when(i == 0)
    def _():
        start(slot_cur_ref, 0)

    wait(cur)
    gate = jnp.concatenate([gate_ref[...], jnp.zeros((LANES - SUBLANES, MOE_TILE), F32)],
                           axis=0).T
    moe = []
    for c in range(ROW_TILES):
        acc = gate[:, 0:1] * _lane_tile(rows_ref.at[cur, 0], c, MOE_TILE)[...]
        for k in range(1, TOP_K):
            acc = acc + gate[:, k:k + 1] * _lane_tile(rows_ref.at[cur, k], c, MOE_TILE)[...]
        moe.append(acc)
    x2 = x1_ref[...] + jnp.concatenate(moe, axis=1)
    start(slot_next_ref, nxt)
    ple = _rms(jnp.dot(p_ref[...].astype(BF16), wpp_ref[...], preferred_element_type=F32),
               gpp_ref[...])
    pg = jax.nn.sigmoid(jnp.dot(_rms(x2, gpg_ref[...]).astype(BF16), wpg_ref[...],
                                preferred_element_type=F32))
    out_ref[...] = x2 + pg * ple

    @pl.when(i == last)
    def _():
        wait(nxt)


def _combine(slots_flat, ys2d, x1, gate, p2d, gpg, wpg, wpp, gpp):
    n = x1.shape[0]
    n_tiles = n // MOE_TILE
    full = lambda shape: pl.BlockSpec(shape, lambda i: (0,) * len(shape))
    row = lambda w: pl.BlockSpec((MOE_TILE, w), lambda i: (i, 0))
    return pl.pallas_call(
        _combine_kernel,
        grid=(n_tiles,),
        in_specs=[
            pl.BlockSpec((TOP_K * MOE_TILE,), lambda i: (i,), memory_space=pltpu.SMEM),
            pl.BlockSpec((TOP_K * MOE_TILE,), lambda i: (jnp.minimum(i + 1, n_tiles - 1),),
                         memory_space=pltpu.SMEM),
            pl.BlockSpec(memory_space=pl.ANY), pl.BlockSpec(memory_space=pl.ANY),
            row(D_MODEL), pl.BlockSpec((SUBLANES, MOE_TILE), lambda i: (0, i)), row(PLE_DIM),
            full((1, D_MODEL)), full((D_MODEL, D_MODEL)), full((PLE_DIM, D_MODEL)),
            full((1, D_MODEL)),
        ],
        out_specs=row(D_MODEL),
        out_shape=jax.ShapeDtypeStruct((n, D_MODEL), F32),
        scratch_shapes=[pltpu.VMEM((2, TOP_K, MOE_TILE * ROW_TILES, LANES), F32),
                        pltpu.SemaphoreType.DMA((2,))],
        compiler_params=_cparams(1),
        name="combine",
    )(slots_flat, slots_flat, ys2d.reshape(-1, ROW_TILES, LANES), ys2d, x1, gate, p2d,
      gpg, wpg, wpp, gpp)


def kernel(x, p, g_mix, w_in, g_q, g_k, w_attn_branch, a_re, a_im, log_dt, b_re, b_im, c_re, c_im, d_skip, w_glu, b_glu, w_out, g_ffn, w_router, b_router, w_up, b_up, w_down, b_down, g_ple_gate, w_ple_gate, w_ple_proj, g_ple_post):
    bsz, s_len, d = x.shape
    depth = w_in.shape[0]
    n = bsz * s_len
    assert d == D_MODEL and s_len % SCAN_CHUNK == 0
    assert all(s_len % t == 0
               for t in (MERGE_SUBTILES * TOK_TILE, IN_TILE, MOE_TILE, DISPATCH_TILE))
    assert bsz == SUBLANES, "the S5 scan keeps the batch on the sublane axis"

    x2d = x.reshape(n, d)
    for i in range(depth):
        qs, kn, vb, u_tm, gates = _inproj(x2d, g_mix[i][None], w_in[i].astype(BF16),
                                          jnp.tile(g_q[i], 2)[None], jnp.tile(g_k[i], 2)[None],
                                          bsz, s_len)
        o = _attn(qs, kn.reshape(bsz, s_len, SB_WIDTH), vb.reshape(bsz, s_len, SB_WIDTH))
        a_bc, b_cat, c_cat = _s5_params(a_re[i], a_im[i], log_dt[i], b_re[i], b_im[i],
                                        c_re[i], c_im[i], bsz)
        y_tm = _s5(u_tm.reshape(s_len, bsz, SSM_WIDTH), a_bc, b_cat, c_cat, d_skip[i][None])

        wr_t = w_router[i].T
        wr_hi = wr_t.astype(BF16)
        wr_lo = (wr_t - wr_hi.astype(F32)).astype(BF16)
        x1, xn, idx, gate, rank, counts = _merge(
            x2d, o.reshape(n, SB_WIDTH), y_tm.reshape(s_len, bsz * SSM_WIDTH), gates,
            w_attn_branch[i].astype(BF16), w_glu[i].astype(BF16), b_glu[i][None],
            w_out[i].astype(BF16), g_ffn[i][None], jnp.concatenate([wr_hi, wr_lo], axis=0),
            b_router[i][:, None], s_len)

        counts_i = counts[:, 0].astype(jnp.int32)
        blk, exp, nexp, flags, starts33 = _build_items(counts_i, n * TOP_K)
        hot = idx[:TOP_K, :, None] == jnp.arange(N_EXPERTS, dtype=jnp.int32)
        slots = jnp.sum(jnp.where(hot, starts33[:N_EXPERTS], 0), axis=-1) + rank[:TOP_K]
        def per_tile(tile):
            return slots.reshape(TOP_K, n // tile, tile).transpose(1, 0, 2).reshape(-1)

        slots_flat = per_tile(MOE_TILE)

        xs3 = _dispatch(per_tile(DISPATCH_TILE), xn.reshape(n, ROW_TILES, LANES))
        ys2d = _experts(blk, exp, nexp, flags, starts33, xs3.reshape(n * TOP_K * ROW_TILES, LANES),
                        w_up[i], b_up[i][:, None, :], w_down[i], b_down[i][:, None, :])
        x2d = _combine(slots_flat, ys2d, x1, gate, p[i].reshape(n, PLE_DIM),
                       g_ple_gate[i][None], w_ple_gate[i].astype(BF16),
                       w_ple_proj[i].astype(BF16), g_ple_post[i][None])
    return x2d.reshape(bsz, s_len, d)
```
